```python
import math
import jax, jax.numpy as jnp
from jax import lax
import numpy as np

D_MODEL = 1024
BATCH = 8
SEQ = 4096
DEPTH = 2
DEC_BATCH = 32
DEC_SEQ = 8
PAST_LEN = 16384
PAGE_SIZE = 128

HEAD_DIM = 64
N_BRANCH = 4
N_HEADS = D_MODEL // (N_BRANCH * HEAD_DIM)
MIX_W = N_HEADS * HEAD_DIM
IDX_HEADS = 8
IDX_DIM = 64
DSA_TOPK = 256
CMP_LEN = 32
CMP_STRIDE = 16
SEL_BLOCK = 64
N_SEL = 16
WINDOW = 512
N_BUCKETS = 32
MAX_DIST = 128
D_FF = 2816
CONV_W = 3
Q_BLOCK = 128
EPS = 1e-6
NEG = -1e30
FORCE = 1e4
FORGET_BIAS = 3.0

IN_SPLITS = (
    ('a_q', MIX_W), ('a_k', HEAD_DIM), ('a_v', HEAD_DIM),
    ('a_qidx', IDX_HEADS * IDX_DIM), ('a_kidx', IDX_DIM), ('a_widx', IDX_HEADS),
    ('b_q', MIX_W), ('b_cmp_k', HEAD_DIM), ('b_cmp_v', HEAD_DIM),
    ('b_slc_k', HEAD_DIM), ('b_slc_v', HEAD_DIM), ('b_win_k', HEAD_DIM), ('b_win_v', HEAD_DIM),
    ('b_gate', 3 * N_HEADS),
    ('c_q', MIX_W), ('c_k', MIX_W), ('c_v', MIX_W), ('c_f', N_HEADS),
    ('d_q', MIX_W), ('d_k', MIX_W), ('d_v', MIX_W),
    ('merge_gate', N_BRANCH * D_MODEL),
)
N_IN = sum(w for _, w in IN_SPLITS)
STATE_KEYS = ('a_kv', 'a_kidx', 'b_cmp_kv', 'b_slc_kv', 'b_win_kv', 'c_kv', 'c_logf', 'd_kv', 'ffn_conv')

kernel_name = 'hybrid_dsa_nsa_fox_stick_step'


def _split_cols(u):
    out, off = {}, 0
    for name, w in IN_SPLITS:
        out[name] = u[..., off:off + w]
        off += w
    return out


def _rms(x, g):
    xf = x.astype(jnp.float32)
    y = xf * lax.rsqrt(jnp.mean(xf * xf, axis=-1, keepdims=True) + EPS)
    return (y * g.astype(jnp.float32)).astype(x.dtype)


def _bucket(dist):
    n = jnp.maximum(dist, 0)
    exact = N_BUCKETS // 2
    nf = jnp.maximum(n, 1).astype(jnp.float32)
    large = exact + (jnp.log(nf / exact) / math.log(MAX_DIST / exact) * (N_BUCKETS - exact)).astype(jnp.int32)
    return jnp.where(n < exact, n, jnp.minimum(large, N_BUCKETS - 1))


def _masked_softmax(s, valid):
    p = jax.nn.softmax(jnp.where(valid, s, NEG), axis=-1)
    return p * valid


def _take_rows(a, idx):
    return jax.vmap(lambda ab, ib: ab[ib])(a, idx)


def _sweep(fn, qpos, *qs):
    t = qpos.shape[0]
    qb = Q_BLOCK if t % Q_BLOCK == 0 else t
    nb = t // qb
    blocks = tuple(jnp.moveaxis(a.reshape((a.shape[0], nb, qb) + a.shape[2:]), 1, 0) for a in qs)
    out = lax.map(lambda args: fn(*args), (jnp.arange(nb), qpos.reshape(nb, qb)) + blocks)
    out = jnp.moveaxis(out, 0, 1)
    return out.reshape((out.shape[0], t) + out.shape[3:])


def _dsa(q, qidx, widx, qpos, k, v, kidx, tab):
    l = k.shape[1]
    topk = min(DSA_TOPK, l // 4)
    kpos = jnp.arange(l)
    scale = HEAD_DIM ** -0.5

    def blk(bi, qp, qb, qib, wb):
        sc = jnp.einsum('bqhd,bsd->bqhs', qib, kidx)
        score = jnp.einsum('bqhs,bqh->bqs', jax.nn.relu(sc), wb).astype(jnp.float32)
        score = jnp.where(kpos[None, None, :] <= qp[None, :, None], score, NEG)
        _, idx = lax.top_k(score, topk)
        kg = _take_rows(k, idx)
        vg = _take_rows(v, idx)
        dist = qp[None, :, None] - idx
        bias = jnp.moveaxis(tab[_bucket(dist)], -1, 2)
        s = jnp.einsum('bqhd,bqkd->bqhk', qb, kg).astype(jnp.float32) * scale + bias
        p = _masked_softmax(s, (dist >= 0)[:, :, None, :])
        return jnp.einsum('bqhk,bqkd->bqhd', p.astype(vg.dtype), vg)

    return _sweep(blk, qpos, q, qidx, widx)


def _compress(rows, w, pe):
    b, l, d = rows.shape
    nch = l // CMP_STRIDE
    ch = rows[:, :nch * CMP_STRIDE].reshape(b, nch, CMP_STRIDE, d)
    blocks = jnp.concatenate([ch[:, :-1], ch[:, 1:]], axis=2)
    return jnp.einsum('bncd,cde->bne', blocks + pe, w.reshape(CMP_LEN, d, d))


def _nsa(q, gates, qpos, cmp_k, cmp_v, slc_k, slc_v, kw_pad, vw_pad, wpos, cmp_w, cmp_pe, g_k, tab):
    b, l, d = slc_k.shape
    scale = d ** -0.5
    ck = _rms(_compress(cmp_k, cmp_w[0], cmp_pe[0]), g_k)
    cv = _compress(cmp_v, cmp_w[1], cmp_pe[1])
    nc = ck.shape[1]
    cend = jnp.arange(nc) * CMP_STRIDE + (CMP_LEN - 1)
    ns = -(-l // SEL_BLOCK)
    n_sel = min(N_SEL, ns)
    per = SEL_BLOCK // CMP_STRIDE
    pad = ((0, 0), (0, ns * SEL_BLOCK - l), (0, 0))
    kb = jnp.pad(slc_k, pad).reshape(b, ns, SEL_BLOCK, d)
    vb = jnp.pad(slc_v, pad).reshape(b, ns, SEL_BLOCK, d)
    jsel = jnp.arange(ns)

    def blk(bi, qp, qb, gb):
        nq = qp.shape[0]
        cdist = qp[:, None] - cend[None, :]
        cbias = jnp.moveaxis(tab[_bucket(cdist)], -1, 1)[None]
        s = jnp.einsum('bqhd,bnd->bqhn', qb, ck).astype(jnp.float32) * scale + cbias
        pc = _masked_softmax(s, (cdist >= 0)[None, :, None, :])
        o_c = jnp.einsum('bqhn,bnd->bqhd', pc.astype(cv.dtype), cv)
        imp = jnp.pad(pc.sum(axis=2), ((0, 0), (0, 0), (0, ns * per - nc)))
        imp = imp.reshape(b, nq, ns, per).sum(-1)
        cur = (qp // SEL_BLOCK)[:, None]
        forced = (jsel == 0) | (jsel == cur) | (jsel == cur - 1)
        score = jnp.where(jsel <= cur, jnp.where(forced, FORCE, imp), NEG)
        _, bidx = lax.top_k(score, n_sel)
        kg = _take_rows(kb, bidx).reshape(b, nq, n_sel * SEL_BLOCK, d)
        vg = _take_rows(vb, bidx).reshape(b, nq, n_sel * SEL_BLOCK, d)
        tpos = (bidx[..., None] * SEL_BLOCK + jnp.arange(SEL_BLOCK)).reshape(b, nq, n_sel * SEL_BLOCK)
        sdist = qp[None, :, None] - tpos
        sbias = jnp.moveaxis(tab[_bucket(sdist)], -1, 2)
        s = jnp.einsum('bqhd,bqkd->bqhk', qb, kg).astype(jnp.float32) * scale + sbias
        ps = _masked_softmax(s, (sdist >= 0)[:, :, None, :])
        o_s = jnp.einsum('bqhk,bqkd->bqhd', ps.astype(vg.dtype), vg)
        kwb = lax.dynamic_slice_in_dim(kw_pad, bi * nq, WINDOW + nq, axis=1)
        vwb = lax.dynamic_slice_in_dim(vw_pad, bi * nq, WINDOW + nq, axis=1)
        wp = lax.dynamic_slice_in_dim(wpos, bi * nq, WINDOW + nq)
        wdist = qp[:, None] - wp[None, :]
        wvalid = (wdist >= 0) & (wdist <= WINDOW) & (wp >= 0)[None, :]
        wbias = jnp.moveaxis(tab[_bucket(wdist)], -1, 1)[None]
        s = jnp.einsum('bqhd,bkd->bqhk', qb, kwb).astype(jnp.float32) * scale + wbias
        pw = _masked_softmax(s, wvalid[None, :, None, :])
        o_w = jnp.einsum('bqhk,bkd->bqhd', pw.astype(vwb.dtype), vwb)
        return gb[..., 0:1] * o_c + gb[..., 1:2] * o_s + gb[..., 2:3] * o_w

    return _sweep(blk, qpos, q, gates)


def _fox(q, cq, qpos, k, v, c_all):
    kpos = jnp.arange(k.shape[1])
    scale = HEAD_DIM ** -0.5
    ck = jnp.moveaxis(c_all, -1, 1)[:, :, None, :]

    def blk(bi, qp, qb, cqb):
        s = jnp.einsum('bqhd,bkhd->bhqk', qb, k).astype(jnp.float32) * scale
        s = s + jnp.moveaxis(cqb, -1, 1)[..., None] - ck
        p = _masked_softmax(s, (kpos[None, :] <= qp[:, None])[None, None])
        return jnp.einsum('bhqk,bkhd->bqhd', p.astype(v.dtype), v)

    return _sweep(blk, qpos, q, cq)


def _stick(q, qpos, k, v):
    kpos = jnp.arange(k.shape[1])
    scale = HEAD_DIM ** -0.5

    def blk(bi, qp, qb):
        z = jnp.einsum('bqhd,bkhd->bhqk', qb, k).astype(jnp.float32) * scale
        strict = (kpos[None, :] < qp[:, None])[None, None]
        sp = jax.nn.softplus(z)
        neg_sp = jnp.where(strict, -sp, 0.0)
        rest = lax.cumsum(neg_sp, axis=3, reverse=True) - neg_sp
        a = jnp.where(strict, jnp.exp(z - sp + rest), 0.0)
        return jnp.einsum('bhqk,bkhd->bqhd', a.astype(v.dtype), v)

    return _sweep(blk, qpos, q)


def _layer(x, past, p0, weights, tab):
    (g_attn, w_in, b_f, qk_g, cmp_w, cmp_pe, w_br, w_o, g_ffn, w_up, conv_w, conv_b, w_down) = weights
    b, t, _ = x.shape
    qpos = p0 + jnp.arange(t)
    cat = lambda a, c: jnp.concatenate([a, c], axis=1)
    heads = lambda a: a.reshape(b, t, N_HEADS, HEAD_DIM)
    u = _split_cols(_rms(x, g_attn) @ w_in)

    qa = _rms(heads(u['a_q']), qk_g[0, 0])
    ka = _rms(u['a_k'], qk_g[0, 1])
    va = u['a_v']
    kidx = u['a_kidx']
    o_a = _dsa(qa, u['a_qidx'].reshape(b, t, IDX_HEADS, IDX_DIM), u['a_widx'], qpos,
               cat(past['a_k'], ka), cat(past['a_v'], va), cat(past['a_kidx'], kidx), tab[:, :N_HEADS])

    qb = _rms(heads(u['b_q']), qk_g[1, 0])
    ks = _rms(u['b_slc_k'], qk_g[1, 1])
    kw = _rms(u['b_win_k'], qk_g[1, 1])
    gates = jax.nn.sigmoid(u['b_gate'].reshape(b, t, N_HEADS, 3))
    win_k = cat(past['b_win_k'], kw)
    win_v = cat(past['b_win_v'], u['b_win_v'])
    wp_len = past['b_win_k'].shape[1]
    zpad = jnp.zeros((b, WINDOW - wp_len, HEAD_DIM), x.dtype)
    kw_pad = jnp.concatenate([zpad, win_k], axis=1)
    vw_pad = jnp.concatenate([zpad, win_v], axis=1)
    wpos = p0 - WINDOW + jnp.arange(WINDOW + t)
    o_b = _nsa(qb, gates, qpos, cat(past['b_cmp_k'], u['b_cmp_k']), cat(past['b_cmp_v'], u['b_cmp_v']),
               cat(past['b_slc_k'], ks), cat(past['b_slc_v'], u['b_slc_v']), kw_pad, vw_pad, wpos,
               cmp_w, cmp_pe, qk_g[1, 1], tab[:, N_HEADS:])
    keep = min(WINDOW, wp_len + t)

    qc = _rms(heads(u['c_q']), qk_g[2, 0])
    kc = _rms(heads(u['c_k']), qk_g[2, 1])
    vc = heads(u['c_v'])
    logf = jax.nn.log_sigmoid((u['c_f'] + b_f).astype(jnp.float32))
    c_all = jnp.cumsum(jnp.concatenate([past['c_logf'].astype(jnp.float32), logf], axis=1), axis=1)
    o_c = _fox(qc, c_all[:, -t:], qpos, cat(past['c_k'], kc), cat(past['c_v'], vc), c_all)

    qd, kd, vd = heads(u['d_q']), heads(u['d_k']), heads(u['d_v'])
    o_d = _stick(qd, qpos, cat(past['d_k'], kd), cat(past['d_v'], vd))

    mg = jax.nn.sigmoid(u['merge_gate'].reshape(b, t, N_BRANCH, D_MODEL))
    outs = (o_a, o_b, o_c, o_d)
    m = sum(mg[:, :, i] * (outs[i].reshape(b, t, MIX_W) @ w_br[i]) for i in range(N_BRANCH))
    x = x + m @ w_o

    up = _rms(x, g_ffn) @ w_up
    upc = cat(past['ffn_conv'], up)
    conv = conv_b + sum(upc[:, i:i + t] * conv_w[i] for i in range(CONV_W))
    val, gate = jnp.split(conv, 2, axis=-1)
    x = x + (jax.nn.silu(gate) * val) @ w_down

    new = {
        'a_kv': jnp.stack([ka, va], axis=2),
        'a_kidx': kidx,
        'b_cmp_kv': jnp.stack([u['b_cmp_k'], u['b_cmp_v']], axis=2),
        'b_slc_kv': jnp.stack([ks, u['b_slc_v']], axis=2),
        'b_win_kv': jnp.stack([win_k[:, -keep:], win_v[:, -keep:]], axis=2),
        'c_kv': jnp.stack([kc, vc], axis=2),
        'c_logf': logf,
        'd_kv': jnp.stack([kd, vd], axis=2),
        'ffn_conv': upc[:, -(CONV_W - 1):],
    }
    return x, new


def _empty_past(b, dtype):
    e = lambda *tail: jnp.zeros((b, 0) + tail, dtype)
    return {
        'a_k': e(HEAD_DIM), 'a_v': e(HEAD_DIM), 'a_kidx': e(IDX_DIM),
        'b_cmp_k': e(HEAD_DIM), 'b_cmp_v': e(HEAD_DIM), 'b_slc_k': e(HEAD_DIM), 'b_slc_v': e(HEAD_DIM),
        'b_win_k': e(HEAD_DIM), 'b_win_v': e(HEAD_DIM),
        'c_k': e(N_HEADS, HEAD_DIM), 'c_v': e(N_HEADS, HEAD_DIM),
        'c_logf': jnp.zeros((b, 0, N_HEADS), jnp.float32),
        'd_k': e(N_HEADS, HEAD_DIM), 'd_v': e(N_HEADS, HEAD_DIM),
        'ffn_conv': jnp.zeros((b, CONV_W - 1, 2 * D_FF), dtype),
    }


def _gather_past(l, page_table, cache_a_kv, cache_a_kidx, cache_b_cmp_kv, cache_b_slc_kv,
                 state_b_win_kv, cache_c_kv, cache_c_logf, cache_d_kv, state_ffn_conv):
    nb, npg = page_table.shape

    def pg(pool):
        g = pool[l, page_table]
        return g.reshape((nb, npg * pool.shape[2]) + pool.shape[3:])

    a, bc, bs, c, d = pg(cache_a_kv), pg(cache_b_cmp_kv), pg(cache_b_slc_kv), pg(cache_c_kv), pg(cache_d_kv)
    w = state_b_win_kv[l]
    return {
        'a_k': a[:, :, 0], 'a_v': a[:, :, 1], 'a_kidx': pg(cache_a_kidx),
        'b_cmp_k': bc[:, :, 0], 'b_cmp_v': bc[:, :, 1], 'b_slc_k': bs[:, :, 0], 'b_slc_v': bs[:, :, 1],
        'b_win_k': w[:, :, 0], 'b_win_v': w[:, :, 1],
        'c_k': c[:, :, 0], 'c_v': c[:, :, 1], 'c_logf': pg(cache_c_logf),
        'd_k': d[:, :, 0], 'd_v': d[:, :, 1],
        'ffn_conv': state_ffn_conv[l],
    }


def setup_inputs(seed: int = 0) -> dict:
    key = jax.random.key(seed)
    ks = jax.random.split(key, 26)
    n_pages = PAST_LEN // PAGE_SIZE
    n_pool = (DEC_BATCH * n_pages * 5) // 4
    win_len = min(WINDOW, PAST_LEN)
    nrm = lambda k, shape, s=1.0: s * jax.random.normal(k, shape, jnp.float32)
    perm = jax.random.permutation(ks[11], n_pool)
    page_table = perm[:DEC_BATCH * n_pages].reshape(DEC_BATCH, n_pages).astype(jnp.int32)
    return {
        'x_prompt': nrm(ks[0], (BATCH, SEQ, D_MODEL)),
        'x_sample': nrm(ks[1], (DEC_BATCH, DEC_SEQ, D_MODEL)),
        'cache_a_kv': nrm(ks[2], (DEPTH, n_pool, PAGE_SIZE, 2, HEAD_DIM)),
        'cache_a_kidx': nrm(ks[3], (DEPTH, n_pool, PAGE_SIZE, IDX_DIM)),
        'cache_b_cmp_kv': nrm(ks[4], (DEPTH, n_pool, PAGE_SIZE, 2, HEAD_DIM)),
        'cache_b_slc_kv': nrm(ks[5], (DEPTH, n_pool, PAGE_SIZE, 2, HEAD_DIM)),
        'state_b_win_kv': nrm(ks[6], (DEPTH, DEC_BATCH, win_len, 2, HEAD_DIM)),
        'cache_c_kv': nrm(ks[7], (DEPTH, n_pool, PAGE_SIZE, 2, N_HEADS, HEAD_DIM)),
        'cache_c_logf': jax.nn.log_sigmoid(FORGET_BIAS + nrm(ks[8], (DEPTH, n_pool, PAGE_SIZE, N_HEADS))),
        'cache_d_kv': nrm(ks[9], (DEPTH, n_pool, PAGE_SIZE, 2, N_HEADS, HEAD_DIM)),
        'state_ffn_conv': nrm(ks[10], (DEPTH, DEC_BATCH, CONV_W - 1, 2 * D_FF)),
        'page_table': page_table,
        'rel_bias': nrm(ks[12], (N_BUCKETS, 2 * N_HEADS), 0.5),
        'norm_attn': 1.0 + nrm(ks[13], (DEPTH, D_MODEL), 0.1),
        'w_in': nrm(ks[14], (DEPTH, D_MODEL, N_IN), D_MODEL ** -0.5),
        'b_forget': FORGET_BIAS + nrm(ks[15], (DEPTH, N_HEADS), 0.5),
        'qk_gain': 1.0 + nrm(ks[16], (DEPTH, 3, 2, HEAD_DIM), 0.1),
        'cmp_w': nrm(ks[17], (DEPTH, 2, CMP_LEN * HEAD_DIM, HEAD_DIM), (CMP_LEN * HEAD_DIM) ** -0.5),
        'cmp_pe': nrm(ks[18], (DEPTH, 2, CMP_LEN, HEAD_DIM), 0.1),
        'w_branch': nrm(ks[19], (DEPTH, N_BRANCH, MIX_W, D_MODEL), MIX_W ** -0.5),
        'w_out': nrm(ks[20], (DEPTH, D_MODEL, D_MODEL), D_MODEL ** -0.5),
        'norm_ffn': 1.0 + nrm(ks[21], (DEPTH, D_MODEL), 0.1),
        'w_up': nrm(ks[22], (DEPTH, D_MODEL, 2 * D_FF), D_MODEL ** -0.5),
        'conv_w': nrm(ks[23], (DEPTH, CONV_W, 2 * D_FF), CONV_W ** -0.5),
        'conv_b': nrm(ks[24], (DEPTH, 2 * D_FF), 0.02),
        'w_down': nrm(ks[25], (DEPTH, D_FF, D_MODEL), D_FF ** -0.5),
    }


def reference(x_prompt, x_sample, cache_a_kv, cache_a_kidx, cache_b_cmp_kv, cache_b_slc_kv,
              state_b_win_kv, cache_c_kv, cache_c_logf, cache_d_kv, state_ffn_conv, page_table,
              rel_bias, norm_attn, w_in, b_forget, qk_gain, cmp_w, cmp_pe, w_branch, w_out,
              norm_ffn, w_up, conv_w, conv_b, w_down):
    def weights(l):
        return (norm_attn[l], w_in[l], b_forget[l], qk_gain[l], cmp_w[l], cmp_pe[l], w_branch[l],
                w_out[l], norm_ffn[l], w_up[l], conv_w[l], conv_b[l], w_down[l])

    past_len = page_table.shape[1] * PAGE_SIZE
    y_prompt, y_sample = x_prompt, x_sample
    new_p = {k: [] for k in STATE_KEYS}
    new_s = {k: [] for k in STATE_KEYS}
    for l in range(DEPTH):
        y_prompt, st = _layer(y_prompt, _empty_past(x_prompt.shape[0], x_prompt.dtype), 0, weights(l), rel_bias)
        for k in STATE_KEYS:
            new_p[k].append(st[k])
        past = _gather_past(l, page_table, cache_a_kv, cache_a_kidx, cache_b_cmp_kv, cache_b_slc_kv,
                            state_b_win_kv, cache_c_kv, cache_c_logf, cache_d_kv, state_ffn_conv)
        y_sample, st = _layer(y_sample, past, past_len, weights(l), rel_bias)
        for k in STATE_KEYS:
            new_s[k].append(st[k])
    sp = {k: jnp.stack(v) for k, v in new_p.items()}
    ss = {k: jnp.stack(v) for k, v in new_s.items()}
    return (y_prompt, y_sample,
            sp['a_kv'], ss['a_kv'], sp['a_kidx'], ss['a_kidx'],
            sp['b_cmp_kv'], ss['b_cmp_kv'], sp['b_slc_kv'], ss['b_slc_kv'],
            sp['b_win_kv'], ss['b_win_kv'], sp['c_kv'], ss['c_kv'],
            sp['c_logf'], ss['c_logf'], sp['d_kv'], ss['d_kv'],
            sp['ffn_conv'], ss['ffn_conv'])
```

```python
import functools
import math

import jax
import jax.numpy as jnp
import numpy as np
from jax import lax
from jax.experimental import pallas as pl
from jax.experimental.pallas import tpu as pltpu

HEAD_DIM = 64
N_HEADS = 4
MIX_W = N_HEADS * HEAD_DIM
IDX_HEADS = 8
IDX_DIM = 64
DSA_TOPK = 256
CMP_LEN = 32
CMP_STRIDE = 16
SEL_BLOCK = 64
N_SEL = 16
WINDOW = 512
N_BUCKETS = 32
MAX_DIST = 128
CONV_W = 3
EPS = 1e-6
NEG = -1e30
FORCE = 1e4
QK_SCALE = HEAD_DIM ** -0.5

LANES = 128
SUBLANES = 8
VMEM_LIMIT = 56 * 1024 * 1024
INT_MIN = -2 ** 31

TQ = 128
TKB = 128
PAGES_PER_STEP = 8

BF = jnp.bfloat16
F32 = jnp.float32

C_AQ, C_AKV, C_AQIDX, C_AMISC = 0, 256, 384, 896
C_BQ, C_BCMP, C_BSLC, C_BWIN, C_BGATE = 1024, 1280, 1408, 1536, 1664
C_CQ, C_CK, C_CV, C_CF = 1792, 2048, 2304, 2560
C_DQ, C_DKV = 2688, 2944
N_PROJ = 3456


def _cparams(sem):
    return pltpu.CompilerParams(dimension_semantics=sem, vmem_limit_bytes=VMEM_LIMIT)


def _dot(a, b):
    return jnp.dot(a, b, preferred_element_type=F32)


def _dot_nt(a, b):
    return lax.dot_general(a, b, (((1,), (1,)), ((), ())), preferred_element_type=F32)


def _split3(x):
    hi = x.astype(BF)
    r1 = x - hi.astype(F32)
    mid = r1.astype(BF)
    lo = (r1 - mid.astype(F32)).astype(BF)
    return hi, mid, lo


def _dot3(x, m):
    hi, mid, lo = _split3(x)
    return _dot(hi, m) + _dot(mid, m) + _dot(lo, m)


def _dot3_l(m, x):
    hi, mid, lo = _split3(x)
    return _dot(m, hi) + _dot(m, mid) + _dot(m, lo)


def _sortable(x):
    x = jnp.where(x == 0.0, 0.0, x)
    b = lax.bitcast_convert_type(x, jnp.int32)
    return b ^ ((b >> 31) & 0x7FFFFFFF)


def _log_sigmoid(v):
    return jnp.minimum(v, 0.0) - jnp.log1p(jnp.exp(-jnp.abs(v)))


def _rms_rows(x, g):
    ms = jnp.mean(x * x, axis=-1, keepdims=True)
    return (x * lax.rsqrt(ms + EPS)) * g


def _proj_kernel(x_ref, g_ref, w_ref, gain_ref, nmask_ref, bf_ref, bd_ref, tri_ref,
                 qa_o, akv_o, akv_b, qidx_o, amisc_o, kidx_b, qb_o, bcmp_o, bslc_o, bslc_b, bwin_o, bwin_b,
                 gates_o, qc_o, ckv_o, ckv_b, logf_o, call_o, qd_o, dkv_o, dkv_b, carry_scr,
                 *, tiles_per_seq, with_cumsum):
    h = _rms_rows(x_ref[...], g_ref[...]).astype(BF)

    def cols(c0, width):
        return _dot(h, w_ref[:, c0:c0 + width])

    def normed(c0, width):
        slab = cols(c0, width)
        ms = _dot((slab * slab).astype(BF), bd_ref[0:width, 0:width])
        scale = lax.rsqrt(ms + EPS) * gain_ref[:, c0:c0 + width]
        return slab * jnp.where(nmask_ref[:, c0:c0 + width] > 0.0, scale, 1.0)

    qa_o[...] = (normed(C_AQ, MIX_W) * QK_SCALE).astype(BF)
    akv = normed(C_AKV, 128)
    akv_o[...] = akv
    akv_b[...] = akv.astype(BF)
    qidx_o[...] = cols(C_AQIDX, IDX_HEADS * IDX_DIM).astype(BF)
    amisc = cols(C_AMISC, 128)
    amisc_o[...] = amisc
    kidx_b[...] = amisc[:, 0:IDX_DIM].astype(BF)
    qb_o[...] = (normed(C_BQ, MIX_W) * QK_SCALE).astype(BF)
    bcmp_o[...] = cols(C_BCMP, 128)
    bslc = normed(C_BSLC, 128)
    bslc_o[...] = bslc
    bslc_b[...] = bslc.astype(BF)
    bwin = normed(C_BWIN, 128)
    bwin_o[...] = bwin
    bwin_b[...] = bwin.astype(BF)
    gates_o[...] = jax.nn.sigmoid(cols(C_BGATE, 128))
    qc_o[...] = (normed(C_CQ, MIX_W) * QK_SCALE).astype(BF)
    ck = normed(C_CK, MIX_W)
    cv = cols(C_CV, MIX_W)
    ckv_o[:, 0:MIX_W] = ck
    ckv_o[:, MIX_W:2 * MIX_W] = cv
    ckv_b[:, 0:MIX_W] = ck.astype(BF)
    ckv_b[:, MIX_W:2 * MIX_W] = cv.astype(BF)
    logf = _log_sigmoid(cols(C_CF, 128) + bf_ref[...])
    logf_o[...] = logf
    if with_cumsum:
        t = pl.program_id(0)

        @pl.when(t % tiles_per_seq == 0)
        def _():
            carry_scr[...] = jnp.zeros_like(carry_scr)

        c = _dot3_l(tri_ref[...], logf) + carry_scr[0:1, :]
        call_o[...] = c
        carry_scr[...] = jnp.broadcast_to(c[-1:, :], carry_scr.shape)
    else:
        call_o[...] = logf
    qd_o[...] = (cols(C_DQ, MIX_W) * QK_SCALE).astype(BF)
    dkv = cols(C_DKV, 2 * MIX_W)
    dkv_o[...] = dkv
    dkv_b[...] = dkv.astype(BF)


def _proj(x2d, lw, *, tm, tiles_per_seq, with_cumsum):
    m, d = x2d.shape
    assert m % tm == 0
    row = lambda i: (i, 0)
    const = lambda i: (0, 0)

    def o(width, dtype):
        return jax.ShapeDtypeStruct((m, width), dtype), pl.BlockSpec((tm, width), row)

    outs = [o(MIX_W, BF), o(128, F32), o(128, BF), o(IDX_HEADS * IDX_DIM, BF), o(128, F32), o(IDX_DIM, BF),
            o(MIX_W, BF), o(128, F32), o(128, F32), o(128, BF), o(128, F32), o(128, BF),
            o(128, F32), o(MIX_W, BF), o(2 * MIX_W, F32), o(2 * MIX_W, BF), o(128, F32), o(128, F32),
            o(MIX_W, BF), o(2 * MIX_W, F32), o(2 * MIX_W, BF)]
    names = ('qa', 'akv', 'akv_b', 'qidx', 'amisc', 'kidx_b', 'qb', 'bcmp', 'bslc', 'bslc_b', 'bwin', 'bwin_b',
             'gates', 'qc', 'ckv', 'ckv_b', 'logf', 'call', 'qd', 'dkv', 'dkv_b')
    res = pl.pallas_call(
        functools.partial(_proj_kernel, tiles_per_seq=tiles_per_seq, with_cumsum=with_cumsum),
        grid=(m // tm,),
        in_specs=[pl.BlockSpec((tm, d), row), pl.BlockSpec((1, d), const), pl.BlockSpec((d, N_PROJ), const),
                  pl.BlockSpec((1, N_PROJ), const), pl.BlockSpec((1, N_PROJ), const), pl.BlockSpec((1, 128), const),
                  pl.BlockSpec((MIX_W, MIX_W), const), pl.BlockSpec((tm, tm), const)],
        out_specs=[s for _, s in outs],
        out_shape=[s for s, _ in outs],
        scratch_shapes=[pltpu.VMEM((SUBLANES, 128), F32)],
        compiler_params=_cparams(("arbitrary",)),
        name="proj",
    )(x2d, lw['g_attn'], lw['w_proj'], lw['gain'], lw['nmask'], lw['b_f'], lw['bd'], lw['tri'][:tm, :tm])
    return dict(zip(names, res))


def _merge_kernel(x_ref, g_ref, wg_ref, oa_ref, ob_ref, oc_ref, od_ref, wbr_ref, wo_ref, y_ref):
    x = x_ref[...]
    d = x.shape[1]
    h = _rms_rows(x, g_ref[...]).astype(BF)
    m = None
    for i, o_ref in enumerate((oa_ref, ob_ref, oc_ref, od_ref)):
        gate = jax.nn.sigmoid(_dot(h, wg_ref[:, i * d:(i + 1) * d]))
        term = gate * _dot(o_ref[...], wbr_ref[i])
        m = term if m is None else m + term
    y_ref[...] = x + _dot(m.astype(BF), wo_ref[...])


def _merge(x2d, o_a, o_b, o_c, o_d, lw, *, tm):
    m, d = x2d.shape
    row = lambda i: (i, 0)
    const = lambda i: (0, 0)
    return pl.pallas_call(
        _merge_kernel,
        grid=(m // tm,),
        in_specs=[pl.BlockSpec((tm, d), row), pl.BlockSpec((1, d), const), pl.BlockSpec((d, N_HEADS * d), const),
                  pl.BlockSpec((tm, MIX_W), row), pl.BlockSpec((tm, MIX_W), row), pl.BlockSpec((tm, MIX_W), row),
                  pl.BlockSpec((tm, MIX_W), row), pl.BlockSpec((4, MIX_W, d), lambda i: (0, 0, 0)),
                  pl.BlockSpec((d, d), const)],
        out_specs=pl.BlockSpec((tm, d), row),
        out_shape=jax.ShapeDtypeStruct((m, d), F32),
        compiler_params=_cparams(("arbitrary",)),
        name="merge",
    )(x2d, lw['g_attn'], lw['w_gate'], o_a, o_b, o_c, o_d, lw['w_br'], lw['w_o'])


def _ffn_kernel(x_ref, g_ref, wup_ref, cw_ref, cb_ref, wdn_ref, st0_ref, st1_ref, y_ref, conv_o, prev_scr,
                *, carry_mode, tiles_per_seq, d_ff, cw):
    x = x_ref[...]
    tm = x.shape[0]
    h = _rms_rows(x, g_ref[...]).astype(BF)
    row = lax.broadcasted_iota(jnp.int32, (tm, cw), 0)
    if carry_mode:
        t = pl.program_id(0)

        @pl.when(t % tiles_per_seq == 0)
        def _():
            prev_scr[...] = jnp.zeros_like(prev_scr)
    else:
        rowmod = row % SUBLANES

    def conv_cols(c0):
        up = _dot(h, wup_ref[:, c0:c0 + cw])
        r1 = pltpu.roll(up, 1, axis=0)
        r2 = pltpu.roll(up, 2, axis=0)
        if carry_mode:
            p6 = prev_scr[6:7, c0:c0 + cw]
            p7 = prev_scr[7:8, c0:c0 + cw]
            u1 = jnp.where(row == 0, p7, r1)
            u2 = jnp.where(row == 0, p6, jnp.where(row == 1, p7, r2))
            prev_scr[:, c0:c0 + cw] = up[tm - SUBLANES:tm, :]
            conv_o[:, c0:c0 + cw] = up[tm - SUBLANES:tm, :]
        else:
            s0 = st0_ref[:, c0:c0 + cw]
            s1 = st1_ref[:, c0:c0 + cw]
            u1 = jnp.where(rowmod == 0, s1, r1)
            u2 = jnp.where(rowmod == 0, s0, jnp.where(rowmod == 1, s1, r2))
            conv_o[:, c0:c0 + cw] = up
        conv = (u2 * cw_ref[0:1, c0:c0 + cw] + u1 * cw_ref[1:2, c0:c0 + cw]) + up * cw_ref[2:3, c0:c0 + cw]
        return cb_ref[:, c0:c0 + cw] + conv

    acc = None
    for c in range(d_ff // cw):
        val = conv_cols(c * cw)
        gate = conv_cols(d_ff + c * cw)
        act = (gate * jax.nn.sigmoid(gate)) * val
        part = _dot(act.astype(BF), wdn_ref[c * cw:(c + 1) * cw, :])
        acc = part if acc is None else acc + part
    y_ref[...] = x + acc


def _ffn(x2d, lw, st0, st1, *, tm, carry_mode, tiles_per_seq):
    m, d = x2d.shape
    d_ff = lw['w_down'].shape[0]
    cw = 256
    assert d_ff % cw == 0 and m % tm == 0
    row = lambda i: (i, 0)
    const = lambda i: (0, 0)
    if carry_mode:
        n_seq = m // (tm * tiles_per_seq)
        conv_shape = jax.ShapeDtypeStruct((n_seq * SUBLANES, 2 * d_ff), F32)
        conv_spec = pl.BlockSpec((SUBLANES, 2 * d_ff), lambda i: (i // tiles_per_seq, 0))
        st_spec = pl.BlockSpec((SUBLANES, 2 * d_ff), const)
    else:
        conv_shape = jax.ShapeDtypeStruct((m, 2 * d_ff), F32)
        conv_spec = pl.BlockSpec((tm, 2 * d_ff), row)
        st_spec = pl.BlockSpec((tm, 2 * d_ff), row)
    return pl.pallas_call(
        functools.partial(_ffn_kernel, carry_mode=carry_mode, tiles_per_seq=tiles_per_seq, d_ff=d_ff, cw=cw),
        grid=(m // tm,),
        in_specs=[pl.BlockSpec((tm, d), row), pl.BlockSpec((1, d), const), pl.BlockSpec((d, 2 * d_ff), const),
                  pl.BlockSpec((CONV_W, 2 * d_ff), const), pl.BlockSpec((1, 2 * d_ff), const),
                  pl.BlockSpec((d_ff, d), const), st_spec, st_spec],
        out_specs=[pl.BlockSpec((tm, d), row), conv_spec],
        out_shape=[jax.ShapeDtypeStruct((m, d), F32), conv_shape],
        scratch_shapes=[pltpu.VMEM((SUBLANES, 2 * d_ff), F32)],
        compiler_params=_cparams(("arbitrary",)),
        name="ffn",
    )(x2d, lw['g_ffn'], lw['w_up'], lw['conv_w'], lw['conv_b'], lw['w_down'], st0, st1)


def _osm_update(s, mask, m, l, acc, v):
    if mask is not None:
        s = jnp.where(mask, s, NEG)
    m_new = jnp.maximum(m, jnp.max(s, axis=1, keepdims=True))
    p = jnp.exp(s - m_new)
    if mask is not None:
        p = jnp.where(mask, p, 0.0)
    alpha = jnp.exp(m - m_new)
    l = alpha * l + jnp.sum(p, axis=1, keepdims=True)
    acc = alpha * acc + _dot(p.astype(BF), v)
    return m_new, l, acc


def _heads_to_rows(q):
    return jnp.concatenate([q[:, h * HEAD_DIM:(h + 1) * HEAD_DIM] for h in range(N_HEADS)], axis=0)


def _rows_to_heads(o, r):
    return jnp.concatenate([o[h * r:(h + 1) * r, :] for h in range(N_HEADS)], axis=1)


def _tile_rows(a, n):
    return jnp.concatenate([a] * n, axis=0)


def _kth_threshold(count_ge, k, rows):
    zero = jnp.zeros((rows, 1), jnp.int32)
    t0 = jnp.where(count_ge(zero) >= k, zero, jnp.full((rows, 1), INT_MIN, jnp.int32))

    def body(b, t):
        cand = t | jnp.left_shift(jnp.int32(1), 30 - b)
        return jnp.where(count_ge(cand) >= k, cand, t)

    return lax.fori_loop(0, 31, body, t0)


def _dsa_prompt_kernel(qidx_ref, amisc_ref, qa_ref, kidx_ref, akv_ref, bias_ref, o_ref, key_scr, *, topk, chunk):
    i = pl.program_id(1)
    nb = i + 1
    nchunk = (nb + chunk - 1) // chunk
    w = amisc_ref[:, IDX_DIM:IDX_DIM + IDX_HEADS]
    qidx = qidx_ref[...]
    r_io = lax.broadcasted_iota(jnp.int32, (TQ, TKB), 0)
    c_io = lax.broadcasted_iota(jnp.int32, (TQ, TKB), 1)
    diag_ok = c_io <= r_io

    def score_blk(j, carry):
        off = pl.multiple_of(j * TKB, TKB)
        kb = kidx_ref[pl.ds(off, TKB), :]
        acc = jnp.zeros((TQ, TKB), F32)
        for h in range(IDX_HEADS):
            sc = _dot_nt(qidx[:, h * IDX_DIM:(h + 1) * IDX_DIM], kb)
            acc = acc + jnp.maximum(sc, 0.0) * w[:, h:h + 1]
        key = jnp.where((j < i) | diag_ok, _sortable(acc), INT_MIN)
        key_scr[:, pl.ds(off, TKB)] = key
        return carry

    lax.fori_loop(0, nb, score_blk, 0)

    def fill_blk(j, carry):
        key_scr[:, pl.ds(pl.multiple_of(j * TKB, TKB), TKB)] = jnp.full((TQ, TKB), INT_MIN, jnp.int32)
        return carry

    lax.fori_loop(nb, nchunk * chunk, fill_blk, 0)

    cw = chunk * TKB

    def count(pred):
        def body(c, acc):
            kc = key_scr[:, pl.ds(pl.multiple_of(c * cw, cw), cw)]
            hit = jnp.where(pred(kc, c * cw), 1.0, 0.0)
            part = hit[:, 0:TKB]
            for u in range(1, chunk):
                part = part + hit[:, u * TKB:(u + 1) * TKB]
            return acc + part
        acc = lax.fori_loop(0, nchunk, body, jnp.zeros((TQ, TKB), F32))
        return jnp.sum(acc, axis=1, keepdims=True).astype(jnp.int32)

    thr = _kth_threshold(lambda cand: count(lambda kc, o: kc >= cand), topk, TQ)

    n_gt = count(lambda kc, o: kc > thr)
    n_eq = count(lambda kc, o: kc == thr)
    need = topk - n_gt
    tie = (n_eq > need) & (thr > INT_MIN)
    any_tie = jnp.max(jnp.where(tie, 1.0, 0.0)) > 0.0
    lane_c = lax.broadcasted_iota(jnp.int32, (TQ, cw), 1)

    @pl.when(any_tie)
    def _():
        nbits = max(1, int(math.ceil(math.log2(key_scr.shape[1]))))

        def bit_body(b, lo):
            cand = lo + jnp.left_shift(jnp.int32(1), nbits - 1 - b)
            cnt = count(lambda kc, o: (kc == thr) & ((lane_c + o) < cand))
            return jnp.where(cnt < need, cand, lo)

        jmax = lax.fori_loop(0, nbits, bit_body, jnp.zeros((TQ, 1), jnp.int32))

        def demote(c, carry):
            sl = pl.ds(pl.multiple_of(c * cw, cw), cw)
            kc = key_scr[:, sl]
            drop = tie & (kc == thr) & ((lane_c + c * cw) > jmax)
            key_scr[:, sl] = jnp.where(drop, thr - 1, kc)
            return carry

        lax.fori_loop(0, nchunk, demote, 0)

    thr_eff = jnp.maximum(thr, INT_MIN + 1)
    q4 = _heads_to_rows(qa_ref[...])

    def att_blk(j, carry):
        m, l, acc = carry
        off = pl.multiple_of(j * TKB, TKB)
        kv = akv_ref[pl.ds(off, TKB), :]
        sel = key_scr[:, pl.ds(off, TKB)] >= thr_eff
        dd = jnp.minimum(i - j, 2)
        s = _dot_nt(q4, kv[:, 0:HEAD_DIM]) + bias_ref[dd].reshape(N_HEADS * TQ, TKB)
        return _osm_update(s, _tile_rows(sel, N_HEADS), m, l, acc, kv[:, HEAD_DIM:2 * HEAD_DIM])

    init = (jnp.full((N_HEADS * TQ, 1), NEG, F32), jnp.zeros((N_HEADS * TQ, 1), F32),
            jnp.zeros((N_HEADS * TQ, HEAD_DIM), F32))
    m, l, acc = lax.fori_loop(0, nb, att_blk, init)
    o_ref[...] = _rows_to_heads(acc / l, TQ).astype(BF)


def _dsa_prompt(pr, bias, b, t):
    nq = t // TQ
    chunk = 4 if nq % 4 == 0 else 1
    topk = min(DSA_TOPK, t // 4)
    q_spec = lambda w: pl.BlockSpec((TQ, w), lambda bi, qi: (bi * nq + qi, 0))
    kv_spec = lambda w: pl.BlockSpec((t, w), lambda bi, qi: (bi, 0))
    return pl.pallas_call(
        functools.partial(_dsa_prompt_kernel, topk=topk, chunk=chunk),
        grid=(b, nq),
        in_specs=[q_spec(IDX_HEADS * IDX_DIM), q_spec(128), q_spec(MIX_W), kv_spec(IDX_DIM), kv_spec(128),
                  pl.BlockSpec((3, N_HEADS, TQ, TKB), lambda bi, qi: (0, 0, 0, 0))],
        out_specs=q_spec(MIX_W),
        out_shape=jax.ShapeDtypeStruct((b * t, MIX_W), BF),
        scratch_shapes=[pltpu.VMEM((TQ, t), jnp.int32)],
        compiler_params=_cparams(("arbitrary", "arbitrary")),
        name="dsa_prompt",
    )(pr['qidx'], pr['amisc'], pr['qa'], pr['kidx_b'], pr['akv_b'], bias)


def _compress_kernel(*refs, n_in):
    x_refs, (pe_ref, w_ref, o_ref) = refs[:n_in], refs[n_in:]
    x = x_refs[0][...] if n_in == 1 else jnp.concatenate([r[...] for r in x_refs], axis=0)
    o_ref[:, 0:128] = _dot((x + pe_ref[0:1, :]).astype(BF), w_ref[0])
    o_ref[:, 128:256] = _dot((x + pe_ref[1:2, :]).astype(BF), w_ref[1])


def _compress_weights(lw):
    w4 = lw['cmp_w'].reshape(2, CMP_LEN, HEAD_DIM, HEAD_DIM)
    pe = lw['cmp_pe']

    def half(rs):
        wk, wv = w4[0, rs], w4[1, rs]
        z = jnp.zeros_like(wk)
        rows = jnp.stack([jnp.concatenate([wk, z], axis=-1), jnp.concatenate([z, wv], axis=-1)], axis=1)
        return rows.reshape(CMP_STRIDE * 2 * HEAD_DIM, 2 * HEAD_DIM)

    top, bot = slice(0, CMP_STRIDE), slice(CMP_STRIDE, CMP_LEN)
    w = jnp.stack([half(top), half(bot)]).astype(BF)
    pef = jnp.stack([jnp.transpose(pe[:, top], (1, 0, 2)).reshape(-1), jnp.transpose(pe[:, bot], (1, 0, 2)).reshape(-1)])
    return pef, w


def _compress_dense(chunks, lw, rows_per_step):
    n, width = chunks.shape
    pef, w = _compress_weights(lw)
    return pl.pallas_call(
        functools.partial(_compress_kernel, n_in=1),
        grid=(n // rows_per_step,),
        in_specs=[pl.BlockSpec((rows_per_step, width), lambda i: (i, 0)), pl.BlockSpec((2, width), lambda i: (0, 0)),
                  pl.BlockSpec((2, width, 128), lambda i: (0, 0, 0))],
        out_specs=pl.BlockSpec((rows_per_step, 256), lambda i: (i, 0)),
        out_shape=jax.ShapeDtypeStruct((n, 256), F32),
        compiler_params=_cparams(("arbitrary",)),
        name="compress_prompt",
    )(chunks, pef, w)


def _combine_compressed(ab, gk):
    n = ab.shape[0]
    kv = ab[:, 0:128] + pltpu.roll(ab[:, 128:256], n - 1, axis=0)
    ck_raw = kv[:, 0:HEAD_DIM]
    ms = jnp.mean(ck_raw * ck_raw, axis=-1, keepdims=True)
    ck = (ck_raw * lax.rsqrt(ms + EPS)) * gk
    return ck.astype(BF), kv[:, HEAD_DIM:2 * HEAD_DIM].astype(BF)


def _masked_softmax_rows(s, valid):
    s = jnp.where(valid, s, NEG)
    m = jnp.max(s, axis=1, keepdims=True)
    p = jnp.where(valid, jnp.exp(s - m), 0.0)
    l = jnp.sum(p, axis=1, keepdims=True)
    return p * (1.0 / jnp.where(l > 0.0, l, 1.0))


def _gate_cols(g, c):
    return jnp.concatenate([g[:, h * 3 + c:h * 3 + c + 1] for h in range(N_HEADS)], axis=0)


def _nsa_prompt_kernel(qb_ref, gates_ref, ab_ref, gk_ref, cbias_ref, gmat_ref, emat_ref, slc_ref, win_ref, bias_ref,
                       o_ref, ck_scr, cv_scr, selm_scr, *, n_sel, ns, nch):
    i = pl.program_id(1)

    @pl.when(i == 0)
    def _():
        ck, cv = _combine_compressed(ab_ref[...], gk_ref[...])
        ck_scr[...] = ck
        cv_scr[...] = cv

    q4 = _heads_to_rows(qb_ref[...])
    r_io = lax.broadcasted_iota(jnp.int32, (TQ, TKB), 0)
    c_io = lax.broadcasted_iota(jnp.int32, (TQ, TKB), 1)
    qpos = i * TQ + lax.broadcasted_iota(jnp.int32, (TQ, 1), 0)

    cend = lax.broadcasted_iota(jnp.int32, (TQ, nch), 1) * CMP_STRIDE + (CMP_LEN - 1)
    cvalid = cend <= qpos
    s = _dot_nt(q4, ck_scr[...]) + cbias_ref[...].reshape(N_HEADS * TQ, nch)
    pc = _masked_softmax_rows(s, _tile_rows(cvalid, N_HEADS))
    o_c = _dot(pc.astype(BF), cv_scr[...])
    pcs = pc[0:TQ] + pc[TQ:2 * TQ] + pc[2 * TQ:3 * TQ] + pc[3 * TQ:4 * TQ]
    imp = _dot3(pcs, gmat_ref[...])

    j_io = lax.broadcasted_iota(jnp.int32, (TQ, ns), 1)
    cur = qpos // SEL_BLOCK
    forced = (j_io == 0) | (j_io == cur) | (j_io == cur - 1)
    score = jnp.where(j_io <= cur, jnp.where(forced, FORCE, imp), NEG)
    rank = jnp.zeros((TQ, ns), F32)
    for jj in range(ns):
        col = score[:, jj:jj + 1]
        ahead = (col > score) | ((col == score) & (j_io > jj))
        rank = rank + jnp.where(ahead, 1.0, 0.0)
    sel = (rank < n_sel) & (j_io <= cur)
    selm_scr[...] = _dot(jnp.where(sel, 1.0, 0.0).astype(BF), emat_ref[...])

    def osm_init():
        return (jnp.full((N_HEADS * TQ, 1), NEG, F32), jnp.zeros((N_HEADS * TQ, 1), F32),
                jnp.zeros((N_HEADS * TQ, HEAD_DIM), F32))

    def sel_blk(j, carry):
        off = pl.multiple_of(j * TKB, TKB)
        kv = slc_ref[pl.ds(off, TKB), :]
        mask = (selm_scr[:, pl.ds(off, TKB)] > 0.5) & ((j < i) | (c_io <= r_io))
        dd = jnp.minimum(i - j, 2)
        s = _dot_nt(q4, kv[:, 0:HEAD_DIM]) + bias_ref[dd].reshape(N_HEADS * TQ, TKB)
        return _osm_update(s, _tile_rows(mask, N_HEADS), *carry, kv[:, HEAD_DIM:2 * HEAD_DIM])

    _, l_s, acc_s = lax.fori_loop(0, i + 1, sel_blk, osm_init())

    carry = osm_init()
    for dj in range(WINDOW // TKB + 1):
        j = i - dj
        off = pl.multiple_of(jnp.maximum(j, 0) * TKB, TKB)
        kv = win_ref[pl.ds(off, TKB), :]
        wd = dj * TKB + r_io - c_io
        mask = (wd >= 0) & (wd <= WINDOW) & (j >= 0)
        s = _dot_nt(q4, kv[:, 0:HEAD_DIM]) + bias_ref[min(dj, 2)].reshape(N_HEADS * TQ, TKB)
        carry = _osm_update(s, _tile_rows(mask, N_HEADS), *carry, kv[:, HEAD_DIM:2 * HEAD_DIM])
    _, l_w, acc_w = carry

    g = gates_ref[...]
    o = _gate_cols(g, 0) * o_c + _gate_cols(g, 1) * (acc_s / l_s) + _gate_cols(g, 2) * (acc_w / l_w)
    o_ref[...] = _rows_to_heads(o, TQ).astype(BF)


def _nsa_prompt(pr, ab, lw, tabs, b, t):
    nq = t // TQ
    nch = t // CMP_STRIDE
    ns = t // SEL_BLOCK
    n_sel = min(N_SEL, ns)
    q_spec = lambda w: pl.BlockSpec((TQ, w), lambda bi, qi: (bi * nq + qi, 0))
    kv_spec = lambda w: pl.BlockSpec((t, w), lambda bi, qi: (bi, 0))
    const2 = lambda bi, qi: (0, 0)
    return pl.pallas_call(
        functools.partial(_nsa_prompt_kernel, n_sel=n_sel, ns=ns, nch=nch),
        grid=(b, nq),
        in_specs=[q_spec(MIX_W), q_spec(128), pl.BlockSpec((nch, 256), lambda bi, qi: (bi, 0)),
                  pl.BlockSpec((1, HEAD_DIM), const2),
                  pl.BlockSpec((None, N_HEADS, TQ, nch), lambda bi, qi: (qi, 0, 0, 0)),
                  pl.BlockSpec((nch, ns), const2), pl.BlockSpec((ns, t), const2),
                  kv_spec(128), kv_spec(128),
                  pl.BlockSpec((3, N_HEADS, TQ, TKB), lambda bi, qi: (0, 0, 0, 0))],
        out_specs=q_spec(MIX_W),
        out_shape=jax.ShapeDtypeStruct((b * t, MIX_W), BF),
        scratch_shapes=[pltpu.VMEM((nch, HEAD_DIM), BF), pltpu.VMEM((nch, HEAD_DIM), BF), pltpu.VMEM((TQ, t), F32)],
        compiler_params=_cparams(("arbitrary", "arbitrary")),
        name="nsa_prompt",
    )(pr['qb'], pr['gates'], ab, lw['gk_b'], tabs['cbias_p'], tabs['gmat_p'], tabs['emat_p'],
      pr['bslc_b'], pr['bwin_b'], tabs['nsa_tiles'])


def _fox_prompt_kernel(q_ref, cq_ref, ckv_ref, ckt_ref, o_ref):
    i = pl.program_id(1)
    q = q_ref[...]
    cq = cq_ref[...]
    r_io = lax.broadcasted_iota(jnp.int32, (TQ, TKB), 0)
    c_io = lax.broadcasted_iota(jnp.int32, (TQ, TKB), 1)
    diag_ok = c_io <= r_io
    outs = []
    for h in range(N_HEADS):
        qh = q[:, h * HEAD_DIM:(h + 1) * HEAD_DIM]
        cqh = cq[:, h:h + 1]

        def blk(j, carry, masked, h=h, qh=qh, cqh=cqh):
            off = pl.multiple_of(j * TKB, TKB)
            k = ckv_ref[pl.ds(off, TKB), h * HEAD_DIM:(h + 1) * HEAD_DIM]
            v = ckv_ref[pl.ds(off, TKB), MIX_W + h * HEAD_DIM:MIX_W + (h + 1) * HEAD_DIM]
            s = (_dot_nt(qh, k) + cqh) - ckt_ref[h:h + 1, pl.ds(off, TKB)]
            return _osm_update(s, diag_ok if masked else None, *carry, v)

        init = (jnp.full((TQ, 1), NEG, F32), jnp.zeros((TQ, 1), F32), jnp.zeros((TQ, HEAD_DIM), F32))
        carry = lax.fori_loop(0, i, lambda j, c, blk=blk: blk(j, c, False), init)
        _, l, acc = blk(i, carry, True)
        outs.append(acc / l)
    o_ref[...] = jnp.concatenate(outs, axis=1).astype(BF)


def _fox_prompt(pr, ckt, b, t):
    nq = t // TQ
    q_spec = lambda w: pl.BlockSpec((TQ, w), lambda bi, qi: (bi * nq + qi, 0))
    return pl.pallas_call(
        _fox_prompt_kernel,
        grid=(b, nq),
        in_specs=[q_spec(MIX_W), q_spec(128), pl.BlockSpec((t, 2 * MIX_W), lambda bi, qi: (bi, 0)),
                  pl.BlockSpec((SUBLANES, t), lambda bi, qi: (bi, 0))],
        out_specs=q_spec(MIX_W),
        out_shape=jax.ShapeDtypeStruct((b * t, MIX_W), BF),
        compiler_params=_cparams(("arbitrary", "arbitrary")),
        name="fox_prompt",
    )(pr['qc'], pr['call'], pr['ckv_b'], ckt)


def _stick_terms(z):
    e = jnp.log1p(jnp.exp(-jnp.abs(z)))
    return -(jnp.maximum(z, 0.0) + e), jnp.minimum(z, 0.0) - e


def _stick_prompt_kernel(q_ref, dkv_ref, u_ref, o_ref):
    i = pl.program_id(1)
    q = q_ref[...]
    umat = u_ref[...]
    r_io = lax.broadcasted_iota(jnp.int32, (TQ, TKB), 0)
    c_io = lax.broadcasted_iota(jnp.int32, (TQ, TKB), 1)
    strict = c_io < r_io
    outs = []
    for h in range(N_HEADS):
        qh = q[:, h * HEAD_DIM:(h + 1) * HEAD_DIM]

        def blk(jj, carry, h=h, qh=qh):
            acc, run = carry
            off = pl.multiple_of((i - jj) * TKB, TKB)
            k = dkv_ref[pl.ds(off, TKB), h * HEAD_DIM:(h + 1) * HEAD_DIM]
            v = dkv_ref[pl.ds(off, TKB), MIX_W + h * HEAD_DIM:MIX_W + (h + 1) * HEAD_DIM]
            ok = (jj > 0) | strict
            nsp, lsig = _stick_terms(_dot_nt(qh, k))
            nsp = jnp.where(ok, nsp, 0.0)
            rest_in = _dot3(nsp, umat)
            a = jnp.where(ok, jnp.exp(lsig + (rest_in + run)), 0.0)
            acc = acc + _dot(a.astype(BF), v)
            return acc, run + jnp.sum(nsp, axis=1, keepdims=True)

        acc, _ = lax.fori_loop(0, i + 1, blk, (jnp.zeros((TQ, HEAD_DIM), F32), jnp.zeros((TQ, 1), F32)))
        outs.append(acc)
    o_ref[...] = jnp.concatenate(outs, axis=1).astype(BF)


def _stick_prompt(pr, umat, b, t):
    nq = t // TQ
    q_spec = lambda w: pl.BlockSpec((TQ, w), lambda bi, qi: (bi * nq + qi, 0))
    return pl.pallas_call(
        _stick_prompt_kernel,
        grid=(b, nq),
        in_specs=[q_spec(MIX_W), pl.BlockSpec((t, 2 * MIX_W), lambda bi, qi: (bi, 0)),
                  pl.BlockSpec((TKB, TKB), lambda bi, qi: (0, 0))],
        out_specs=q_spec(MIX_W),
        out_shape=jax.ShapeDtypeStruct((b * t, MIX_W), BF),
        compiler_params=_cparams(("arbitrary", "arbitrary")),
        name="stick_prompt",
    )(pr['qd'], pr['dkv_b'], umat)


PAGE = 128
TD = SUBLANES


def _page_specs(page_shape, l, pg, page_of):
    tail = (0,) * len(page_shape)

    def spec(i):
        def index_map(*args):
            pt = args[-1]
            return (l, pt[args[0], page_of(*args[1:-1], i)]) + tail
        return pl.BlockSpec((None, None) + page_shape, index_map)

    return [spec(i) for i in range(pg)]


def _pad_rows(a, n):
    return jnp.concatenate([a, jnp.zeros((n - a.shape[0], a.shape[1]), a.dtype)], axis=0)


def _topk_select_ref(key_ref, k):
    rows, width = key_ref.shape

    def count(pred):
        return jnp.sum(jnp.where(pred(key_ref[...]), 1.0, 0.0), axis=1, keepdims=True).astype(jnp.int32)

    thr = _kth_threshold(lambda cand: count(lambda kk: kk >= cand), k, rows)
    n_gt = count(lambda kk: kk > thr)
    n_eq = count(lambda kk: kk == thr)
    need = k - n_gt
    tie = (n_eq > need) & (thr > INT_MIN)
    any_tie = jnp.max(jnp.where(tie, 1.0, 0.0)) > 0.0

    @pl.when(any_tie)
    def _():
        idx = lax.broadcasted_iota(jnp.int32, (rows, width), 1)
        nbits = max(1, int(math.ceil(math.log2(width))))

        def bit_body(b, lo):
            cand = lo + jnp.left_shift(jnp.int32(1), nbits - 1 - b)
            cnt = count(lambda kk: (kk == thr) & (idx < cand))
            return jnp.where(cnt < need, cand, lo)

        jmax = lax.fori_loop(0, nbits, bit_body, jnp.zeros((rows, 1), jnp.int32))
        kk = key_ref[...]
        key_ref[...] = jnp.where(tie & (kk == thr) & (idx > jmax), thr - 1, kk)

    return jnp.maximum(thr, INT_MIN + 1)


def _osm_scratch_update(s, mask, v, m_scr, l_scr, acc_scr):
    m, l, acc = _osm_update(s, mask, m_scr[...], l_scr[...], acc_scr[...], v)
    m_scr[...] = m
    l_scr[...] = l
    acc_scr[...] = acc


def _osm_scratch_init(m_scr, l_scr, acc_scr):
    m_scr[...] = jnp.full(m_scr.shape, NEG, F32)
    l_scr[...] = jnp.zeros(l_scr.shape, F32)
    acc_scr[...] = jnp.zeros(acc_scr.shape, F32)


def _head_block_diag(q):
    lane = lax.broadcasted_iota(jnp.int32, q.shape, 1)
    return jnp.concatenate([jnp.where(lane // HEAD_DIM == h, q, jnp.zeros_like(q)) for h in range(N_HEADS)], axis=0)


def _head_diag_pick(o):
    lane = lax.broadcasted_iota(jnp.int32, (TD, o.shape[1]), 1)
    out = None
    for h in range(N_HEADS):
        part = jnp.where(lane // HEAD_DIM == h, o[h * TD:(h + 1) * TD, :], 0.0)
        out = part if out is None else out + part
    return out


def _causal_new_mask(strict):
    lane = lax.broadcasted_iota(jnp.int32, (TD, PAGE), 1)
    rowi = lax.broadcasted_iota(jnp.int32, (TD, PAGE), 0)
    return lane < rowi if strict else lane <= rowi


def _compress_paged(cache, page_table, l, lw, nstep, pg):
    s = page_table.shape[0]
    pef, w = _compress_weights(lw)
    width = pef.shape[1]
    rows = PAGE // CMP_STRIDE

    def body(pt_ref, *refs):
        _compress_kernel(*refs, n_in=pg)

    return pl.pallas_call(
        body,
        grid_spec=pltpu.PrefetchScalarGridSpec(
            num_scalar_prefetch=1, grid=(s, nstep),
            in_specs=_page_specs((rows, width), l, pg, lambda p, i: p * pg + i)
            + [pl.BlockSpec((2, width), lambda si, p, pt: (0, 0)), pl.BlockSpec((2, width, 128), lambda si, p, pt: (0, 0, 0))],
            out_specs=pl.BlockSpec((None, pg * rows, 256), lambda si, p, pt: (si, p, 0))),
        out_shape=jax.ShapeDtypeStruct((s, nstep * pg * rows, 256), F32),
        compiler_params=_cparams(("arbitrary", "arbitrary")),
        name="compress_decode",
    )(page_table, *([cache] * pg), pef, w)


def _dsa_decode_kernel(pt_ref, qidx_ref, amisc_ref, qa_ref, kidxn_ref, akvn_ref, blast_ref, bfar_ref, bnew_ref, *rest,
                       topk, nstep, pg):
    kid_refs, kv_refs = rest[:pg], rest[pg:2 * pg]
    o_ref, key_scr, thr_scr, m_scr, l_scr, acc_scr = rest[2 * pg:]
    ph, p = pl.program_id(1), pl.program_id(2)
    wstep = pg * PAGE
    npast = nstep * wstep
    last = p == nstep - 1

    @pl.when(ph == 0)
    def _():
        qidx = qidx_ref[...]
        w = amisc_ref[:, IDX_DIM:IDX_DIM + IDX_HEADS]
        q64 = jnp.concatenate([qidx[:, h * IDX_DIM:(h + 1) * IDX_DIM] for h in range(IDX_HEADS)], axis=0)

        def scores(keys):
            sc = _dot_nt(q64, keys)
            acc = None
            for h in range(IDX_HEADS):
                term = jnp.maximum(sc[h * TD:(h + 1) * TD, :], 0.0) * w[:, h:h + 1]
                acc = term if acc is None else acc + term
            return acc

        kid = jnp.concatenate([r[...] for r in kid_refs], axis=0).astype(BF)
        key_scr[:, pl.ds(pl.multiple_of(p * wstep, wstep), wstep)] = _sortable(scores(kid))

        @pl.when(last)
        def _():
            acc = scores(_pad_rows(kidxn_ref[...], PAGE))
            key_scr[:, npast:npast + PAGE] = jnp.where(_causal_new_mask(False), _sortable(acc), INT_MIN)
            thr_scr[...] = jnp.broadcast_to(_topk_select_ref(key_scr, topk), thr_scr.shape)

    @pl.when(ph == 1)
    def _():
        @pl.when(p == 0)
        def _():
            _osm_scratch_init(m_scr, l_scr, acc_scr)

        thr = thr_scr[:, 0:1]
        q4 = _heads_to_rows(qa_ref[...])
        kv = jnp.concatenate([r[...] for r in kv_refs], axis=0)
        bias = jnp.where(last, blast_ref[...], bfar_ref[...])
        s = _dot_nt(q4, kv[:, 0:HEAD_DIM].astype(BF)) + bias
        sel = key_scr[:, pl.ds(pl.multiple_of(p * wstep, wstep), wstep)] >= thr
        _osm_scratch_update(s, _tile_rows(sel, N_HEADS), kv[:, HEAD_DIM:2 * HEAD_DIM].astype(BF), m_scr, l_scr, acc_scr)

        @pl.when(last)
        def _():
            kvn = _pad_rows(akvn_ref[...], PAGE)
            s = _dot_nt(q4, kvn[:, 0:HEAD_DIM]) + bnew_ref[...]
            sel = key_scr[:, npast:npast + PAGE] >= thr
            m, l, acc = _osm_update(s, _tile_rows(sel, N_HEADS), m_scr[...], l_scr[...], acc_scr[...],
                                    kvn[:, HEAD_DIM:2 * HEAD_DIM])
            o_ref[...] = _rows_to_heads(acc / l, TD).astype(BF)


def _seq_spec(width, ngrid):
    if ngrid == 2:
        return pl.BlockSpec((None, TD, width), lambda si, p, pt: (si, 0, 0))
    return pl.BlockSpec((None, TD, width), lambda si, ph, p, pt: (si, 0, 0))


def _const_spec(shape, ngrid):
    zeros = (0,) * len(shape)
    if ngrid == 2:
        return pl.BlockSpec(shape, lambda si, p, pt: zeros)
    return pl.BlockSpec(shape, lambda si, ph, p, pt: zeros)


def _dsa_decode(prs, cache_kidx, cache_akv, page_table, l, tabs, nstep, pg):
    s = page_table.shape[0]
    npast = nstep * pg * PAGE
    topk = min(DSA_TOPK, (npast + TD) // 4)
    wstep = pg * PAGE
    kid_specs = _page_specs((PAGE, IDX_DIM), l, pg, lambda ph, p, i: jnp.where(ph == 0, p, nstep - 1) * pg + i)
    kv_specs = _page_specs((PAGE, 128), l, pg, lambda ph, p, i: jnp.where(ph == 0, 0, p) * pg + i)
    return pl.pallas_call(
        functools.partial(_dsa_decode_kernel, topk=topk, nstep=nstep, pg=pg),
        grid_spec=pltpu.PrefetchScalarGridSpec(
            num_scalar_prefetch=1, grid=(s, 2, nstep),
            in_specs=[_seq_spec(IDX_HEADS * IDX_DIM, 3), _seq_spec(128, 3), _seq_spec(MIX_W, 3), _seq_spec(IDX_DIM, 3),
                      _seq_spec(128, 3), _const_spec((N_HEADS * TD, wstep), 3), _const_spec((N_HEADS * TD, 1), 3),
                      _const_spec((N_HEADS * TD, PAGE), 3)] + kid_specs + kv_specs,
            out_specs=_seq_spec(MIX_W, 3),
            scratch_shapes=[pltpu.VMEM((TD, npast + PAGE), jnp.int32), pltpu.VMEM((TD, 128), jnp.int32),
                            pltpu.VMEM((N_HEADS * TD, 1), F32), pltpu.VMEM((N_HEADS * TD, 1), F32),
                            pltpu.VMEM((N_HEADS * TD, HEAD_DIM), F32)]),
        out_shape=jax.ShapeDtypeStruct((s, TD, MIX_W), BF),
        compiler_params=_cparams(("arbitrary", "arbitrary", "arbitrary")),
        name="dsa_decode",
    )(page_table, prs['qidx'], prs['amisc'], prs['qa'], prs['kidx_b'], prs['akv_b'],
      tabs['a_last'], tabs['a_far'], tabs['a_new'], *([cache_kidx] * pg), *([cache_akv] * pg))


def _nsa_decode_kernel(pt_ref, qb_ref, gates_ref, ab_ref, gk_ref, cbias_ref, gmat_ref, slcn_ref, winp_ref, winn_ref,
                       blast_ref, bfar_ref, bnew_ref, wbp_ref, *rest, n_sel, nstep, pg):
    slc_refs = rest[:pg]
    o_ref, selm_scr, bkey_scr, oc_scr, m_scr, l_scr, acc_scr = rest[pg:]
    p = pl.program_id(1)
    wstep = pg * PAGE
    npast = nstep * wstep
    nch = npast // CMP_STRIDE
    nsb = npast // SEL_BLOCK
    last = p == nstep - 1
    q4 = _heads_to_rows(qb_ref[...])

    @pl.when(p == 0)
    def _():
        ck, cv = _combine_compressed(ab_ref[...], gk_ref[...])
        qpos = npast + lax.broadcasted_iota(jnp.int32, (TD, 1), 0)
        cend = lax.broadcasted_iota(jnp.int32, (TD, nch), 1) * CMP_STRIDE + (CMP_LEN - 1)
        s = _dot_nt(q4, ck) + cbias_ref[...]
        pc = _masked_softmax_rows(s, _tile_rows(cend <= qpos, N_HEADS))
        oc_scr[...] = _dot(pc.astype(BF), cv)
        pcs = pc[0:TD] + pc[TD:2 * TD] + pc[2 * TD:3 * TD] + pc[3 * TD:4 * TD]
        imp = _dot3(pcs, gmat_ref[...])
        j_io = lax.broadcasted_iota(jnp.int32, (TD, nsb), 1)
        forced = (j_io == 0) | (j_io == nsb - 1)
        bkey_scr[...] = _sortable(jnp.where(forced, FORCE, imp))
        thr = _topk_select_ref(bkey_scr, n_sel - 1)
        selb = jnp.where(bkey_scr[...] >= thr, 1.0, 0.0).astype(BF)
        blk = lax.broadcasted_iota(jnp.int32, (nsb, wstep), 0)
        col = lax.broadcasted_iota(jnp.int32, (nsb, wstep), 1)
        for c in range(nstep):
            expand = jnp.where(blk == (c * wstep + col) // SEL_BLOCK, 1.0, 0.0).astype(BF)
            selm_scr[:, c * wstep:(c + 1) * wstep] = _dot(selb, expand)
        _osm_scratch_init(m_scr, l_scr, acc_scr)

    kv = jnp.concatenate([r[...] for r in slc_refs], axis=0)
    bias = jnp.where(last, blast_ref[...], bfar_ref[...])
    s = _dot_nt(q4, kv[:, 0:HEAD_DIM].astype(BF)) + bias
    mask = selm_scr[:, pl.ds(pl.multiple_of(p * wstep, wstep), wstep)] > 0.5
    _osm_scratch_update(s, _tile_rows(mask, N_HEADS), kv[:, HEAD_DIM:2 * HEAD_DIM].astype(BF), m_scr, l_scr, acc_scr)

    @pl.when(last)
    def _():
        new_mask = _tile_rows(_causal_new_mask(False), N_HEADS)
        kvn = _pad_rows(slcn_ref[...], PAGE)
        s = _dot_nt(q4, kvn[:, 0:HEAD_DIM]) + bnew_ref[...]
        _, l_s, acc_s = _osm_update(s, new_mask, m_scr[...], l_scr[...], acc_scr[...], kvn[:, HEAD_DIM:2 * HEAD_DIM])
        wp = winp_ref[...]
        nw = wp.shape[0]
        wd = nw + lax.broadcasted_iota(jnp.int32, (TD, nw), 0) - lax.broadcasted_iota(jnp.int32, (TD, nw), 1)
        s = _dot_nt(q4, wp[:, 0:HEAD_DIM].astype(BF)) + wbp_ref[...]
        carry = _osm_update(s, _tile_rows(wd <= WINDOW, N_HEADS), jnp.full((N_HEADS * TD, 1), NEG, F32),
                            jnp.zeros((N_HEADS * TD, 1), F32), jnp.zeros((N_HEADS * TD, HEAD_DIM), F32),
                            wp[:, HEAD_DIM:2 * HEAD_DIM].astype(BF))
        kvw = _pad_rows(winn_ref[...], PAGE)
        s = _dot_nt(q4, kvw[:, 0:HEAD_DIM]) + bnew_ref[...]
        _, l_w, acc_w = _osm_update(s, new_mask, *carry, kvw[:, HEAD_DIM:2 * HEAD_DIM])
        g = gates_ref[...]
        o = _gate_cols(g, 0) * oc_scr[...] + _gate_cols(g, 1) * (acc_s / l_s) + _gate_cols(g, 2) * (acc_w / l_w)
        o_ref[...] = _rows_to_heads(o, TD).astype(BF)


def _nsa_decode(prs, ab, win_past, cache_slc, page_table, l, lw, tabs, nstep, pg):
    s = page_table.shape[0]
    wstep = pg * PAGE
    npast = nstep * wstep
    nch, nsb = npast // CMP_STRIDE, npast // SEL_BLOCK
    n_sel = min(N_SEL, nsb + 1)
    assert n_sel >= 2 and win_past.shape[1] == WINDOW
    return pl.pallas_call(
        functools.partial(_nsa_decode_kernel, n_sel=n_sel, nstep=nstep, pg=pg),
        grid_spec=pltpu.PrefetchScalarGridSpec(
            num_scalar_prefetch=1, grid=(s, nstep),
            in_specs=[_seq_spec(MIX_W, 2), _seq_spec(128, 2),
                      pl.BlockSpec((None, nch, 256), lambda si, p, pt: (si, 0, 0)), _const_spec((1, HEAD_DIM), 2),
                      _const_spec((N_HEADS * TD, nch), 2), _const_spec((nch, nsb), 2), _seq_spec(128, 2),
                      pl.BlockSpec((None, WINDOW, 128), lambda si, p, pt: (si, 0, 0)), _seq_spec(128, 2),
                      _const_spec((N_HEADS * TD, wstep), 2), _const_spec((N_HEADS * TD, 1), 2),
                      _const_spec((N_HEADS * TD, PAGE), 2), _const_spec((N_HEADS * TD, WINDOW), 2)]
            + _page_specs((PAGE, 128), l, pg, lambda p, i: p * pg + i),
            out_specs=_seq_spec(MIX_W, 2),
            scratch_shapes=[pltpu.VMEM((TD, npast), F32), pltpu.VMEM((TD, nsb), jnp.int32),
                            pltpu.VMEM((N_HEADS * TD, HEAD_DIM), F32), pltpu.VMEM((N_HEADS * TD, 1), F32),
                            pltpu.VMEM((N_HEADS * TD, 1), F32), pltpu.VMEM((N_HEADS * TD, HEAD_DIM), F32)]),
        out_shape=jax.ShapeDtypeStruct((s, TD, MIX_W), BF),
        compiler_params=_cparams(("arbitrary", "arbitrary")),
        name="nsa_decode",
    )(page_table, prs['qb'], prs['gates'], ab, lw['gk_b'], tabs['cbias_d'], tabs['gmat_d'], prs['bslc_b'], win_past,
      prs['bwin_b'], tabs['b_last'], tabs['b_far'], tabs['b_new'], tabs['wb_past'], *([cache_slc] * pg))


def _fox_decode_kernel(pt_ref, q_ref, lfn_ref, lftn_ref, ckvn_ref, u_ref, tinc_ref, *rest, nstep, pg):
    kv_refs, lf_refs = rest[:pg], rest[pg:2 * pg]
    o_ref, qbd_scr, cq_scr, m_scr, l_scr, acc_scr, carry_scr = rest[2 * pg:]
    p = pl.program_id(1)

    @pl.when(p == 0)
    def _():
        qbd = _head_block_diag(q_ref[...])
        qbd_scr[...] = qbd
        lf = lfn_ref[...]
        rows = [lf[0:1, :]]
        for r in range(1, TD):
            rows.append(rows[-1] + lf[r:r + 1, :])
        npf = jnp.concatenate(rows, axis=0)
        cq4 = jnp.concatenate([npf[:, h:h + 1] for h in range(N_HEADS)], axis=0)
        cq_scr[...] = cq4
        npt = _dot3(lftn_ref[...], tinc_ref[...])
        ck4 = jnp.concatenate([jnp.broadcast_to(npt[h:h + 1, :], (TD, PAGE)) for h in range(N_HEADS)], axis=0)
        kvn = _pad_rows(ckvn_ref[...], PAGE)
        s = (_dot_nt(qbd, kvn[:, 0:MIX_W]) + cq4) - ck4
        _osm_scratch_init(m_scr, l_scr, acc_scr)
        _osm_scratch_update(s, _tile_rows(_causal_new_mask(False), N_HEADS), kvn[:, MIX_W:2 * MIX_W], m_scr, l_scr, acc_scr)
        carry_scr[...] = jnp.zeros_like(carry_scr)

    x = jnp.concatenate([r[...] for r in lf_refs], axis=0)
    rs = _dot3(x, u_ref[...])
    tot = jnp.sum(x, axis=1, keepdims=True)
    off = carry_scr[:, 0:1]
    pieces = [None] * pg
    for i in reversed(range(pg)):
        s_i = rs[i * SUBLANES:(i + 1) * SUBLANES, :] + off
        pieces[i] = jnp.concatenate([jnp.broadcast_to(s_i[h:h + 1, :], (TD, PAGE)) for h in range(N_HEADS)], axis=0)
        off = off + tot[i * SUBLANES:(i + 1) * SUBLANES, :]
    carry_scr[...] = jnp.broadcast_to(off, carry_scr.shape)
    kv = jnp.concatenate([r[...] for r in kv_refs], axis=0)
    s = (_dot_nt(qbd_scr[...], kv[:, 0:MIX_W].astype(BF)) + cq_scr[...]) + jnp.concatenate(pieces, axis=1)
    _osm_scratch_update(s, None, kv[:, MIX_W:2 * MIX_W].astype(BF), m_scr, l_scr, acc_scr)

    @pl.when(p == nstep - 1)
    def _():
        o_ref[...] = _head_diag_pick(acc_scr[...] / l_scr[...]).astype(BF)


def _fox_decode(prs, lft_new, cache_ckv, cache_lft, page_table, l, tabs, nstep, pg):
    s = page_table.shape[0]
    rev = lambda p, i: (nstep - 1 - p) * pg + i
    return pl.pallas_call(
        functools.partial(_fox_decode_kernel, nstep=nstep, pg=pg),
        grid_spec=pltpu.PrefetchScalarGridSpec(
            num_scalar_prefetch=1, grid=(s, nstep),
            in_specs=[_seq_spec(MIX_W, 2), _seq_spec(128, 2), _seq_spec(128, 2), _seq_spec(2 * MIX_W, 2),
                      _const_spec((PAGE, PAGE), 2), _const_spec((PAGE, PAGE), 2)]
            + _page_specs((PAGE, 2 * MIX_W), l, pg, rev) + _page_specs((SUBLANES, PAGE), l, pg, rev),
            out_specs=_seq_spec(MIX_W, 2),
            scratch_shapes=[pltpu.VMEM((N_HEADS * TD, MIX_W), BF), pltpu.VMEM((N_HEADS * TD, 1), F32),
                            pltpu.VMEM((N_HEADS * TD, 1), F32), pltpu.VMEM((N_HEADS * TD, 1), F32),
                            pltpu.VMEM((N_HEADS * TD, MIX_W), F32), pltpu.VMEM((SUBLANES, 128), F32)]),
        out_shape=jax.ShapeDtypeStruct((s, TD, MIX_W), BF),
        compiler_params=_cparams(("arbitrary", "arbitrary")),
        name="fox_decode",
    )(page_table, prs['qc'], prs['logf'], lft_new, prs['ckv_b'], tabs['umat'], tabs['tinc'],
      *([cache_ckv] * pg), *([cache_lft] * pg))


def _stick_decode_kernel(pt_ref, q_ref, dkvn_ref, u_ref, *rest, nstep, pg):
    kv_refs = rest[:pg]
    o_ref, qbd_scr, acc_scr, run_scr = rest[pg:]
    p = pl.program_id(1)
    umat = u_ref[...]
    rows = N_HEADS * TD

    @pl.when(p == 0)
    def _():
        qbd = _head_block_diag(q_ref[...])
        qbd_scr[...] = qbd
        kvn = _pad_rows(dkvn_ref[...], PAGE)
        strict = _tile_rows(_causal_new_mask(True), N_HEADS)
        nsp, lsig = _stick_terms(_dot_nt(qbd, kvn[:, 0:MIX_W]))
        nsp = jnp.where(strict, nsp, 0.0)
        a = jnp.where(strict, jnp.exp(lsig + _dot3(nsp, umat)), 0.0)
        acc_scr[...] = _dot(a.astype(BF), kvn[:, MIX_W:2 * MIX_W])
        run_scr[...] = jnp.sum(nsp, axis=1, keepdims=True)

    kv = jnp.concatenate([r[...] for r in kv_refs], axis=0)
    nsp, lsig = _stick_terms(_dot_nt(qbd_scr[...], kv[:, 0:MIX_W].astype(BF)))
    stack = jnp.concatenate([nsp[:, i * PAGE:(i + 1) * PAGE] for i in range(pg)], axis=0)
    rs = _dot3(stack, umat)
    tot = jnp.sum(stack, axis=1, keepdims=True)
    off = run_scr[...]
    pieces = [None] * pg
    for i in reversed(range(pg)):
        pieces[i] = rs[i * rows:(i + 1) * rows, :] + off
        off = off + tot[i * rows:(i + 1) * rows, :]
    run_scr[...] = off
    a = jnp.exp(lsig + jnp.concatenate(pieces, axis=1))
    acc_scr[...] = acc_scr[...] + _dot(a.astype(BF), kv[:, MIX_W:2 * MIX_W].astype(BF))

    @pl.when(p == nstep - 1)
    def _():
        o_ref[...] = _head_diag_pick(acc_scr[...]).astype(BF)


def _stick_decode(prs, cache_dkv, page_table, l, tabs, nstep, pg):
    s = page_table.shape[0]
    return pl.pallas_call(
        functools.partial(_stick_decode_kernel, nstep=nstep, pg=pg),
        grid_spec=pltpu.PrefetchScalarGridSpec(
            num_scalar_prefetch=1, grid=(s, nstep),
            in_specs=[_seq_spec(MIX_W, 2), _seq_spec(2 * MIX_W, 2), _const_spec((PAGE, PAGE), 2)]
            + _page_specs((PAGE, 2 * MIX_W), l, pg, lambda p, i: (nstep - 1 - p) * pg + i),
            out_specs=_seq_spec(MIX_W, 2),
            scratch_shapes=[pltpu.VMEM((N_HEADS * TD, MIX_W), BF), pltpu.VMEM((N_HEADS * TD, MIX_W), F32),
                            pltpu.VMEM((N_HEADS * TD, 1), F32)]),
        out_shape=jax.ShapeDtypeStruct((s, TD, MIX_W), BF),
        compiler_params=_cparams(("arbitrary", "arbitrary")),
        name="stick_decode",
    )(page_table, prs['qd'], prs['dkv_b'], tabs['umat'], *([cache_dkv] * pg))


_IN_A, _IN_B, _IN_C, _IN_D = 968, 1620, 2392, 3160


def _bucket(dist):
    n = jnp.maximum(dist, 0)
    exact = N_BUCKETS // 2
    nf = jnp.maximum(n, 1).astype(F32)
    large = exact + (jnp.log(nf / exact) / math.log(MAX_DIST / exact) * (N_BUCKETS - exact)).astype(jnp.int32)
    return jnp.where(n < exact, n, jnp.minimum(large, N_BUCKETS - 1))


def _bias_table(tab, dist):
    return jnp.moveaxis(tab[_bucket(dist)], -1, 0)


def _prep_layer(l, p, tm_max):
    w_in = p['w_in'][l]
    d = w_in.shape[0]
    z = lambda n: jnp.zeros((d, n), w_in.dtype)
    w_proj = jnp.concatenate([w_in[:, 0:_IN_A], z(1024 - _IN_A), w_in[:, _IN_A:_IN_B], z(768 - (_IN_B - _IN_A)),
                              w_in[:, _IN_B:_IN_C], z(896 - (_IN_C - _IN_B)), w_in[:, _IN_C:_IN_D]], axis=1)
    assert w_proj.shape[1] == N_PROJ
    qk = p['qk_gain'][l]
    gain = jnp.ones((N_PROJ,), F32)
    nmask = jnp.zeros((N_PROJ,), F32)
    for c0, g, rep in ((C_AQ, qk[0, 0], 4), (C_AKV, qk[0, 1], 1), (C_BQ, qk[1, 0], 4), (C_BSLC, qk[1, 1], 1),
                       (C_BWIN, qk[1, 1], 1), (C_CQ, qk[2, 0], 4), (C_CK, qk[2, 1], 4)):
        gain = gain.at[c0:c0 + rep * HEAD_DIM].set(jnp.tile(g, rep))
        nmask = nmask.at[c0:c0 + rep * HEAD_DIM].set(1.0)
    gidx = np.arange(MIX_W) // HEAD_DIM
    bd = jnp.asarray((gidx[:, None] == gidx[None, :]).astype(np.float32) / HEAD_DIM, BF)
    tri = jnp.asarray(np.tril(np.ones((tm_max, tm_max), np.float32)), BF)
    return {
        'g_attn': p['norm_attn'][l][None, :], 'w_proj': w_proj.astype(BF), 'gain': gain[None, :],
        'nmask': nmask[None, :], 'b_f': jnp.zeros((1, 128), F32).at[0, :N_HEADS].set(p['b_forget'][l]),
        'bd': bd, 'tri': tri,
        'w_gate': w_in[:, _IN_D:].astype(BF), 'w_br': p['w_branch'][l].astype(BF), 'w_o': p['w_out'][l].astype(BF),
        'g_ffn': p['norm_ffn'][l][None, :], 'w_up': p['w_up'][l].astype(BF), 'conv_w': p['conv_w'][l],
        'conv_b': p['conv_b'][l][None, :], 'w_down': p['w_down'][l].astype(BF),
        'gk_b': qk[1, 1][None, :], 'cmp_w': p['cmp_w'][l], 'cmp_pe': p['cmp_pe'][l],
    }


def _toeplitz_bias(tab):
    r = jnp.arange(TQ)[:, None]
    c = jnp.arange(TKB)[None, :]
    return jnp.stack([_bias_table(tab, dd * TKB + r - c) for dd in range(3)])


def _prompt_tables(rel_bias, t):
    tab_a, tab_b = rel_bias[:, :N_HEADS], rel_bias[:, N_HEADS:]
    nq, nch, ns = t // TQ, t // CMP_STRIDE, t // SEL_BLOCK
    qpos = jnp.arange(t).reshape(nq, TQ)
    cend = jnp.arange(nch) * CMP_STRIDE + (CMP_LEN - 1)
    cbias = jnp.transpose(_bias_table(tab_b, qpos[:, :, None] - cend[None, None, :]), (1, 0, 2, 3))
    n = np.arange(nch)
    gmat = ((n[:, None] // (SEL_BLOCK // CMP_STRIDE) == np.arange(ns)[None, :]) & (n[:, None] < nch - 1))
    emat = np.arange(t)[None, :] // SEL_BLOCK == np.arange(ns)[:, None]
    c = np.arange(TKB)
    return {
        'dsa_tiles': _toeplitz_bias(tab_a), 'nsa_tiles': _toeplitz_bias(tab_b), 'cbias_p': cbias,
        'gmat_p': jnp.asarray(gmat.astype(np.float32), BF), 'emat_p': jnp.asarray(emat.astype(np.float32), BF),
        'umat': jnp.asarray((c[:, None] > c[None, :]).astype(np.float32), BF),
    }


def _layer_prompt(x, lw, tabs):
    b, t, d = x.shape
    assert t % TQ == 0
    tm = 256 if t % 256 == 0 else TQ
    x2 = x.reshape(b * t, d)
    pr = _proj(x2, lw, tm=tm, tiles_per_seq=t // tm, with_cumsum=True)
    o_a = _dsa_prompt(pr, tabs['dsa_tiles'], b, t)
    ab = _compress_dense(pr['bcmp'].reshape(b * t // CMP_STRIDE, CMP_STRIDE * 2 * HEAD_DIM), lw, t // CMP_STRIDE)
    o_b = _nsa_prompt(pr, ab, lw, tabs, b, t)
    ckt = jnp.transpose(pr['call'].reshape(b, t, 128)[:, :, :SUBLANES], (0, 2, 1)).reshape(b * SUBLANES, t)
    o_c = _fox_prompt(pr, ckt, b, t)
    o_d = _stick_prompt(pr, tabs['umat'], b, t)
    xm = _merge(x2, o_a, o_b, o_c, o_d, lw, tm=tm)
    dummy = jnp.zeros((SUBLANES, lw['w_up'].shape[1]), F32)
    y2, conv = _ffn(xm, lw, dummy, dummy, tm=tm, carry_mode=True, tiles_per_seq=t // tm)
    keep = min(WINDOW, t)
    new = {
        'a_kv': pr['akv'].reshape(b, t, 2, HEAD_DIM),
        'a_kidx': pr['amisc'][:, :IDX_DIM].reshape(b, t, IDX_DIM),
        'b_cmp_kv': pr['bcmp'].reshape(b, t, 2, HEAD_DIM),
        'b_slc_kv': pr['bslc'].reshape(b, t, 2, HEAD_DIM),
        'b_win_kv': pr['bwin'].reshape(b, t, 2, HEAD_DIM)[:, t - keep:],
        'c_kv': pr['ckv'].reshape(b, t, 2, N_HEADS, HEAD_DIM),
        'c_logf': pr['logf'][:, :N_HEADS].reshape(b, t, N_HEADS),
        'd_kv': pr['dkv'].reshape(b, t, 2, N_HEADS, HEAD_DIM),
        'ffn_conv': conv.reshape(b, SUBLANES, -1)[:, SUBLANES - (CONV_W - 1):],
    }
    return y2.reshape(b, t, d), new, (o_a, o_b, o_c, o_d)


def _decode_tables(rel_bias, npast, pg):
    tab_a, tab_b = rel_bias[:, :N_HEADS], rel_bias[:, N_HEADS:]
    wstep = pg * PAGE
    tq = jnp.arange(TD)
    rows = lambda tab, dist: _bias_table(tab, dist).reshape(N_HEADS * TD, -1)
    far = lambda tab: jnp.repeat(tab[N_BUCKETS - 1], TD)[:, None]
    d_last = (npast + tq)[:, None] - (npast - wstep + jnp.arange(wstep))[None, :]
    d_new = tq[:, None] - jnp.arange(PAGE)[None, :]
    nch, nsb = npast // CMP_STRIDE, npast // SEL_BLOCK
    d_cmp = (npast + tq)[:, None] - (jnp.arange(nch) * CMP_STRIDE + (CMP_LEN - 1))[None, :]
    d_win = WINDOW + tq[:, None] - jnp.arange(WINDOW)[None, :]
    n = np.arange(nch)
    gmat = (n[:, None] // (SEL_BLOCK // CMP_STRIDE) == np.arange(nsb)[None, :]) & (n[:, None] < nch - 1)
    c = np.arange(PAGE)
    return {
        'a_last': rows(tab_a, d_last), 'a_far': far(tab_a), 'a_new': rows(tab_a, d_new),
        'b_last': rows(tab_b, d_last), 'b_far': far(tab_b), 'b_new': rows(tab_b, d_new),
        'cbias_d': rows(tab_b, d_cmp), 'wb_past': rows(tab_b, d_win),
        'gmat_d': jnp.asarray(gmat.astype(np.float32), BF),
        'umat': jnp.asarray((c[:, None] > c[None, :]).astype(np.float32), BF),
        'tinc': jnp.asarray((c[:, None] <= c[None, :]).astype(np.float32), BF),
    }


def _layer_decode(x, lw, tabs, l, caches, ffn_state, page_table, nstep, pg):
    s, td, d = x.shape
    assert td == TD
    m = s * td
    x2 = x.reshape(m, d)
    pr = _proj(x2, lw, tm=m, tiles_per_seq=1, with_cumsum=False)
    prs = {k: v.reshape(s, td, v.shape[-1]) for k, v in pr.items()}
    o_a = _dsa_decode(prs, caches['a_kidx'], caches['a_kv'], page_table, l, tabs, nstep, pg)
    ab = _compress_paged(caches['b_cmp'], page_table, l, lw, nstep, pg)
    o_b = _nsa_decode(prs, ab, caches['b_win'][l], caches['b_slc'], page_table, l, lw, tabs, nstep, pg)
    lft_new = jnp.pad(jnp.transpose(prs['logf'][:, :, :SUBLANES], (0, 2, 1)), ((0, 0), (0, 0), (0, PAGE - td)))
    o_c = _fox_decode(prs, lft_new, caches['c_kv'], caches['c_lft'], page_table, l, tabs, nstep, pg)
    o_d = _stick_decode(prs, caches['d_kv'], page_table, l, tabs, nstep, pg)
    flat = lambda o: o.reshape(m, MIX_W)
    xm = _merge(x2, flat(o_a), flat(o_b), flat(o_c), flat(o_d), lw, tm=m)
    st0 = jnp.repeat(ffn_state[:, 0], td, axis=0)
    st1 = jnp.repeat(ffn_state[:, 1], td, axis=0)
    y2, conv = _ffn(xm, lw, st0, st1, tm=m, carry_mode=False, tiles_per_seq=1)
    win_new = pr['bwin'].reshape(s, td, 2, HEAD_DIM)
    win_all = jnp.concatenate([caches['b_win'][l].reshape(s, -1, 2, HEAD_DIM), win_new], axis=1)
    keep = min(WINDOW, win_all.shape[1])
    new = {
        'a_kv': pr['akv'].reshape(s, td, 2, HEAD_DIM),
        'a_kidx': pr['amisc'][:, :IDX_DIM].reshape(s, td, IDX_DIM),
        'b_cmp_kv': pr['bcmp'].reshape(s, td, 2, HEAD_DIM),
        'b_slc_kv': pr['bslc'].reshape(s, td, 2, HEAD_DIM),
        'b_win_kv': win_all[:, win_all.shape[1] - keep:],
        'c_kv': pr['ckv'].reshape(s, td, 2, N_HEADS, HEAD_DIM),
        'c_logf': pr['logf'][:, :N_HEADS].reshape(s, td, N_HEADS),
        'd_kv': pr['dkv'].reshape(s, td, 2, N_HEADS, HEAD_DIM),
        'ffn_conv': conv.reshape(s, td, -1)[:, td - (CONV_W - 1):],
    }
    return y2.reshape(s, td, d), new, (o_a, o_b, o_c, o_d)


_STATE_KEYS = ('a_kv', 'a_kidx', 'b_cmp_kv', 'b_slc_kv', 'b_win_kv', 'c_kv', 'c_logf', 'd_kv', 'ffn_conv')


def kernel(x_prompt, x_sample, cache_a_kv, cache_a_kidx, cache_b_cmp_kv, cache_b_slc_kv, state_b_win_kv,
           cache_c_kv, cache_c_logf, cache_d_kv, state_ffn_conv, page_table, rel_bias, norm_attn, w_in,
           b_forget, qk_gain, cmp_w, cmp_pe, w_branch, w_out, norm_ffn, w_up, conv_w, conv_b, w_down):
    params = dict(norm_attn=norm_attn, w_in=w_in, b_forget=b_forget, qk_gain=qk_gain, cmp_w=cmp_w, cmp_pe=cmp_pe,
                  w_branch=w_branch, w_out=w_out, norm_ffn=norm_ffn, w_up=w_up, conv_w=conv_w, conv_b=conv_b,
                  w_down=w_down)
    depth, n_pool = w_in.shape[0], cache_a_kv.shape[1]
    n_seq, n_pages = page_table.shape
    assert cache_a_kv.shape[2] == PAGE
    pg = PAGES_PER_STEP if n_pages % PAGES_PER_STEP == 0 else 1
    nstep = n_pages // pg
    npast = n_pages * PAGE
    caches = {
        'a_kv': cache_a_kv.reshape(depth, n_pool, PAGE, 2 * HEAD_DIM),
        'a_kidx': cache_a_kidx,
        'b_cmp': cache_b_cmp_kv.reshape(depth, n_pool, PAGE // CMP_STRIDE, CMP_STRIDE * 2 * HEAD_DIM),
        'b_slc': cache_b_slc_kv.reshape(depth, n_pool, PAGE, 2 * HEAD_DIM),
        'b_win': state_b_win_kv.reshape(depth, n_seq, state_b_win_kv.shape[2], 2 * HEAD_DIM),
        'c_kv': cache_c_kv.reshape(depth, n_pool, PAGE, 2 * MIX_W),
        'c_lft': jnp.pad(jnp.transpose(cache_c_logf.astype(F32), (0, 1, 3, 2)),
                         ((0, 0), (0, 0), (0, SUBLANES - N_HEADS), (0, 0))),
        'd_kv': cache_d_kv.reshape(depth, n_pool, PAGE, 2 * MIX_W),
    }
    tabs_p = _prompt_tables(rel_bias, x_prompt.shape[1])
    tabs_d = _decode_tables(rel_bias, npast, pg)
    y_p, y_s = x_prompt, x_sample
    new_p = {k: [] for k in _STATE_KEYS}
    new_s = {k: [] for k in _STATE_KEYS}
    for l in range(depth):
        lw = _prep_layer(l, params, 256)
        y_p, st, _ = _layer_prompt(y_p, lw, tabs_p)
        for k in _STATE_KEYS:
            new_p[k].append(st[k])
        y_s, st, _ = _layer_decode(y_s, lw, tabs_d, l, caches, state_ffn_conv[l], page_table, nstep, pg)
        for k in _STATE_KEYS:
            new_s[k].append(st[k])
    sp = {k: jnp.stack(v) for k, v in new_p.items()}
    ss = {k: jnp.stack(v) for k, v in new_s.items()}
    return (y_p, y_s,
            sp['a_kv'], ss['a_kv'], sp['a_kidx'], ss['a_kidx'],
            sp['b_cmp_kv'], ss['b_cmp_kv'], sp['b_slc_kv'], ss['b_slc_kv'],
            sp['b_win_kv'], ss['b_win_kv'], sp['c_kv'], ss['c_kv'],
            sp['c_logf'], ss['c_logf'], sp['d_kv'], ss['d_kv'],
            sp['ffn_conv'], ss['ffn_conv'])
```

```python
import functools
import math

import jax
import jax.numpy as jnp
import numpy as np
from jax import lax
from jax.experimental import pallas as pl
from jax.experimental.pallas import tpu as pltpu

HEAD_DIM = 64
N_HEADS = 4
MIX_W = N_HEADS * HEAD_DIM
IDX_HEADS = 8
IDX_DIM = 64
DSA_TOPK = 256
CMP_LEN = 32
CMP_STRIDE = 16
SEL_BLOCK = 64
N_SEL = 16
WINDOW = 512
N_BUCKETS = 32
MAX_DIST = 128
CONV_W = 3
EPS = 1e-6
NEG = -1e30
FORCE = 1e4
QK_SCALE = HEAD_DIM ** -0.5

LANES = 128
SUBLANES = 8
VMEM_LIMIT = 56 * 1024 * 1024
INT_MIN = -2 ** 31

TQ = 128
TKB = 128
PAGES_PER_STEP = 8

BF = jnp.bfloat16
F32 = jnp.float32

C_AQ, C_AKV, C_AQIDX, C_AMISC = 0, 256, 384, 896
C_BQ, C_BCMP, C_BSLC, C_BWIN, C_BGATE = 1024, 1280, 1408, 1536, 1664
C_CQ, C_CK, C_CV, C_CF = 1792, 2048, 2304, 2560
C_DQ, C_DKV = 2688, 2944
N_PROJ = 3456


def _cparams(sem):
    return pltpu.CompilerParams(dimension_semantics=sem, vmem_limit_bytes=VMEM_LIMIT)


def _dot(a, b):
    return jnp.dot(a, b, preferred_element_type=F32)


def _dot_nt(a, b):
    return lax.dot_general(a, b, (((1,), (1,)), ((), ())), preferred_element_type=F32)


def _split3(x):
    hi = x.astype(BF)
    r1 = x - hi.astype(F32)
    mid = r1.astype(BF)
    lo = (r1 - mid.astype(F32)).astype(BF)
    return hi, mid, lo


def _dot3(x, m):
    hi, mid, lo = _split3(x)
    return _dot(hi, m) + _dot(mid, m) + _dot(lo, m)


def _dot3_l(m, x):
    hi, mid, lo = _split3(x)
    return _dot(m, hi) + _dot(m, mid) + _dot(m, lo)


def _sortable(x):
    x = jnp.where(x == 0.0, 0.0, x)
    b = lax.bitcast_convert_type(x, jnp.int32)
    return b ^ ((b >> 31) & 0x7FFFFFFF)


def _log_sigmoid(v):
    return jnp.minimum(v, 0.0) - jnp.log1p(jnp.exp(-jnp.abs(v)))


def _rms_rows(x, g):
    ms = jnp.mean(x * x, axis=-1, keepdims=True)
    return (x * lax.rsqrt(ms + EPS)) * g


def _proj_kernel(x_ref, g_ref, w_ref, gain_ref, nmask_ref, bf_ref, bd_ref, tri_ref,
                 qa_o, akv_o, akv_b, qidx_o, amisc_o, kidx_b, qb_o, bcmp_o, bslc_o, bslc_b, bwin_o, bwin_b,
                 gates_o, qc_o, ckv_o, ckv_b, logf_o, call_o, qd_o, dkv_o, dkv_b, carry_scr,
                 *, tiles_per_seq, with_cumsum):
    h = _rms_rows(x_ref[...], g_ref[...]).astype(BF)

    def cols(c0, width):
        return _dot(h, w_ref[:, c0:c0 + width])

    def normed(c0, width):
        slab = cols(c0, width)
        ms = _dot((slab * slab).astype(BF), bd_ref[0:width, 0:width])
        scale = lax.rsqrt(ms + EPS) * gain_ref[:, c0:c0 + width]
        return slab * jnp.where(nmask_ref[:, c0:c0 + width] > 0.0, scale, 1.0)

    qa_o[...] = (normed(C_AQ, MIX_W) * QK_SCALE).astype(BF)
    akv = normed(C_AKV, 128)
    akv_o[...] = akv
    akv_b[...] = akv.astype(BF)
    qidx_o[...] = cols(C_AQIDX, IDX_HEADS * IDX_DIM).astype(BF)
    amisc = cols(C_AMISC, 128)
    amisc_o[...] = amisc
    kidx_b[...] = amisc[:, 0:IDX_DIM].astype(BF)
    qb_o[...] = (normed(C_BQ, MIX_W) * QK_SCALE).astype(BF)
    bcmp_o[...] = cols(C_BCMP, 128)
    bslc = normed(C_BSLC, 128)
    bslc_o[...] = bslc
    bslc_b[...] = bslc.astype(BF)
    bwin = normed(C_BWIN, 128)
    bwin_o[...] = bwin
    bwin_b[...] = bwin.astype(BF)
    gates_o[...] = jax.nn.sigmoid(cols(C_BGATE, 128))
    qc_o[...] = (normed(C_CQ, MIX_W) * QK_SCALE).astype(BF)
    ck = normed(C_CK, MIX_W)
    cv = cols(C_CV, MIX_W)
    ckv_o[:, 0:MIX_W] = ck
    ckv_o[:, MIX_W:2 * MIX_W] = cv
    ckv_b[:, 0:MIX_W] = ck.astype(BF)
    ckv_b[:, MIX_W:2 * MIX_W] = cv.astype(BF)
    logf = _log_sigmoid(cols(C_CF, 128) + bf_ref[...])
    logf_o[...] = logf
    if with_cumsum:
        t = pl.program_id(0)

        @pl.when(t % tiles_per_seq == 0)
        def _():
            carry_scr[...] = jnp.zeros_like(carry_scr)

        c = _dot3_l(tri_ref[...], logf) + carry_scr[0:1, :]
        call_o[...] = c
        carry_scr[...] = jnp.broadcast_to(c[-1:, :], carry_scr.shape)
    else:
        call_o[...] = logf
    qd_o[...] = (cols(C_DQ, MIX_W) * QK_SCALE).astype(BF)
    dkv = cols(C_DKV, 2 * MIX_W)
    dkv_o[...] = dkv
    dkv_b[...] = dkv.astype(BF)


def _proj(x2d, lw, *, tm, tiles_per_seq, with_cumsum):
    m, d = x2d.shape
    assert m % tm == 0
    row = lambda i: (i, 0)
    const = lambda i: (0, 0)

    def o(width, dtype):
        return jax.ShapeDtypeStruct((m, width), dtype), pl.BlockSpec((tm, width), row)

    outs = [o(MIX_W, BF), o(128, F32), o(128, BF), o(IDX_HEADS * IDX_DIM, BF), o(128, F32), o(IDX_DIM, BF),
            o(MIX_W, BF), o(128, F32), o(128, F32), o(128, BF), o(128, F32), o(128, BF),
            o(128, F32), o(MIX_W, BF), o(2 * MIX_W, F32), o(2 * MIX_W, BF), o(128, F32), o(128, F32),
            o(MIX_W, BF), o(2 * MIX_W, F32), o(2 * MIX_W, BF)]
    names = ('qa', 'akv', 'akv_b', 'qidx', 'amisc', 'kidx_b', 'qb', 'bcmp', 'bslc', 'bslc_b', 'bwin', 'bwin_b',
             'gates', 'qc', 'ckv', 'ckv_b', 'logf', 'call', 'qd', 'dkv', 'dkv_b')
    res = pl.pallas_call(
        functools.partial(_proj_kernel, tiles_per_seq=tiles_per_seq, with_cumsum=with_cumsum),
        grid=(m // tm,),
        in_specs=[pl.BlockSpec((tm, d), row), pl.BlockSpec((1, d), const), pl.BlockSpec((d, N_PROJ), const),
                  pl.BlockSpec((1, N_PROJ), const), pl.BlockSpec((1, N_PROJ), const), pl.BlockSpec((1, 128), const),
                  pl.BlockSpec((MIX_W, MIX_W), const), pl.BlockSpec((tm, tm), const)],
        out_specs=[s for _, s in outs],
        out_shape=[s for s, _ in outs],
        scratch_shapes=[pltpu.VMEM((SUBLANES, 128), F32)],
        compiler_params=_cparams(("arbitrary",)),
        name="proj",
    )(x2d, lw['g_attn'], lw['w_proj'], lw['gain'], lw['nmask'], lw['b_f'], lw['bd'], lw['tri'][:tm, :tm])
    return dict(zip(names, res))


def _merge_kernel(x_ref, g_ref, wg_ref, oa_ref, ob_ref, oc_ref, od_ref, wbr_ref, wo_ref, y_ref):
    x = x_ref[...]
    d = x.shape[1]
    h = _rms_rows(x, g_ref[...]).astype(BF)
    m = None
    for i, o_ref in enumerate((oa_ref, ob_ref, oc_ref, od_ref)):
        gate = jax.nn.sigmoid(_dot(h, wg_ref[:, i * d:(i + 1) * d]))
        term = gate * _dot(o_ref[...], wbr_ref[i])
        m = term if m is None else m + term
    y_ref[...] = x + _dot(m.astype(BF), wo_ref[...])


def _merge(x2d, o_a, o_b, o_c, o_d, lw, *, tm):
    m, d = x2d.shape
    row = lambda i: (i, 0)
    const = lambda i: (0, 0)
    return pl.pallas_call(
        _merge_kernel,
        grid=(m // tm,),
        in_specs=[pl.BlockSpec((tm, d), row), pl.BlockSpec((1, d), const), pl.BlockSpec((d, N_HEADS * d), const),
                  pl.BlockSpec((tm, MIX_W), row), pl.BlockSpec((tm, MIX_W), row), pl.BlockSpec((tm, MIX_W), row),
                  pl.BlockSpec((tm, MIX_W), row), pl.BlockSpec((4, MIX_W, d), lambda i: (0, 0, 0)),
                  pl.BlockSpec((d, d), const)],
        out_specs=pl.BlockSpec((tm, d), row),
        out_shape=jax.ShapeDtypeStruct((m, d), F32),
        compiler_params=_cparams(("arbitrary",)),
        name="merge",
    )(x2d, lw['g_attn'], lw['w_gate'], o_a, o_b, o_c, o_d, lw['w_br'], lw['w_o'])


def _ffn_kernel(x_ref, g_ref, wup_ref, cw_ref, cb_ref, wdn_ref, st0_ref, st1_ref, y_ref, conv_o, prev_scr,
                *, carry_mode, tiles_per_seq, d_ff, cw):
    x = x_ref[...]
    tm = x.shape[0]
    h = _rms_rows(x, g_ref[...]).astype(BF)
    row = lax.broadcasted_iota(jnp.int32, (tm, cw), 0)
    if carry_mode:
        t = pl.program_id(0)

        @pl.when(t % tiles_per_seq == 0)
        def _():
            prev_scr[...] = jnp.zeros_like(prev_scr)
    else:
        rowmod = row % SUBLANES

    def conv_cols(c0):
        up = _dot(h, wup_ref[:, c0:c0 + cw])
        r1 = pltpu.roll(up, 1, axis=0)
        r2 = pltpu.roll(up, 2, axis=0)
        if carry_mode:
            p6 = prev_scr[6:7, c0:c0 + cw]
            p7 = prev_scr[7:8, c0:c0 + cw]
            u1 = jnp.where(row == 0, p7, r1)
            u2 = jnp.where(row == 0, p6, jnp.where(row == 1, p7, r2))
            prev_scr[:, c0:c0 + cw] = up[tm - SUBLANES:tm, :]
            conv_o[:, c0:c0 + cw] = up[tm - SUBLANES:tm, :]
        else:
            s0 = st0_ref[:, c0:c0 + cw]
            s1 = st1_ref[:, c0:c0 + cw]
            u1 = jnp.where(rowmod == 0, s1, r1)
            u2 = jnp.where(rowmod == 0, s0, jnp.where(rowmod == 1, s1, r2))
            conv_o[:, c0:c0 + cw] = up
        conv = (u2 * cw_ref[0:1, c0:c0 + cw] + u1 * cw_ref[1:2, c0:c0 + cw]) + up * cw_ref[2:3, c0:c0 + cw]
        return cb_ref[:, c0:c0 + cw] + conv

    acc = None
    for c in range(d_ff // cw):
        val = conv_cols(c * cw)
        gate = conv_cols(d_ff + c * cw)
        act = (gate * jax.nn.sigmoid(gate)) * val
        part = _dot(act.astype(BF), wdn_ref[c * cw:(c + 1) * cw, :])
        acc = part if acc is None else acc + part
    y_ref[...] = x + acc


def _ffn(x2d, lw, st0, st1, *, tm, carry_mode, tiles_per_seq):
    m, d = x2d.shape
    d_ff = lw['w_down'].shape[0]
    cw = 256
    assert d_ff % cw == 0 and m % tm == 0
    row = lambda i: (i, 0)
    const = lambda i: (0, 0)
    if carry_mode:
        n_seq = m // (tm * tiles_per_seq)
        conv_shape = jax.ShapeDtypeStruct((n_seq * SUBLANES, 2 * d_ff), F32)
        conv_spec = pl.BlockSpec((SUBLANES, 2 * d_ff), lambda i: (i // tiles_per_seq, 0))
        st_spec = pl.BlockSpec((SUBLANES, 2 * d_ff), const)
    else:
        conv_shape = jax.ShapeDtypeStruct((m, 2 * d_ff), F32)
        conv_spec = pl.BlockSpec((tm, 2 * d_ff), row)
        st_spec = pl.BlockSpec((tm, 2 * d_ff), row)
    return pl.pallas_call(
        functools.partial(_ffn_kernel, carry_mode=carry_mode, tiles_per_seq=tiles_per_seq, d_ff=d_ff, cw=cw),
        grid=(m // tm,),
        in_specs=[pl.BlockSpec((tm, d), row), pl.BlockSpec((1, d), const), pl.BlockSpec((d, 2 * d_ff), const),
                  pl.BlockSpec((CONV_W, 2 * d_ff), const), pl.BlockSpec((1, 2 * d_ff), const),
                  pl.BlockSpec((d_ff, d), const), st_spec, st_spec],
        out_specs=[pl.BlockSpec((tm, d), row), conv_spec],
        out_shape=[jax.ShapeDtypeStruct((m, d), F32), conv_shape],
        scratch_shapes=[pltpu.VMEM((SUBLANES, 2 * d_ff), F32)],
        compiler_params=_cparams(("arbitrary",)),
        name="ffn",
    )(x2d, lw['g_ffn'], lw['w_up'], lw['conv_w'], lw['conv_b'], lw['w_down'], st0, st1)


def _osm_update(s, mask, m, l, acc, v):
    if mask is not None:
        s = jnp.where(mask, s, NEG)
    m_new = jnp.maximum(m, jnp.max(s, axis=1, keepdims=True))
    p = jnp.exp(s - m_new)
    if mask is not None:
        p = jnp.where(mask, p, 0.0)
    alpha = jnp.exp(m - m_new)
    l = alpha * l + jnp.sum(p, axis=1, keepdims=True)
    acc = alpha * acc + _dot(p.astype(BF), v)
    return m_new, l, acc


def _osm_ref_init(m_ref, l_ref, acc_ref):
    m_ref[...] = jnp.full(m_ref.shape, NEG, F32)
    l_ref[...] = jnp.zeros(l_ref.shape, F32)
    acc_ref[...] = jnp.zeros(acc_ref.shape, F32)


def _osm_ref_step(s, mask, v, m_ref, l_ref, acc_ref):
    reps = s.shape[1] // LANES
    if mask is not None:
        s = jnp.where(mask, s, NEG)
    m_prev = m_ref[...]
    m_new = jnp.maximum(m_prev, jnp.max(s, axis=1, keepdims=True))
    p = jnp.exp(s - jnp.tile(m_new, (1, reps)))
    if mask is not None:
        p = jnp.where(mask, p, 0.0)
    alpha = jnp.exp(m_prev - m_new)
    l_ref[...] = alpha * l_ref[...] + jnp.sum(p, axis=1, keepdims=True)
    d = acc_ref.shape[1]
    a = alpha[:, 0:d] if d <= LANES else jnp.tile(alpha, (1, d // LANES))
    acc_ref[...] = a * acc_ref[...] + _dot(p.astype(BF), v)
    m_ref[...] = m_new


def _key_chunk(t):
    return 4 * TKB if t % (4 * TKB) == 0 else TKB


def _tile_bias(bias_ref, i, first_tile, ntiles):
    parts = [bias_ref[jnp.clip(i - (first_tile + u), 0, 2)].reshape(N_HEADS * TQ, TKB) for u in range(ntiles)]
    return parts[0] if ntiles == 1 else jnp.concatenate(parts, axis=1)


def _pad_lanes(q):
    return jnp.concatenate([q, jnp.zeros_like(q)], axis=1)


def _heads_to_rows(q):
    return jnp.concatenate([q[:, h * HEAD_DIM:(h + 1) * HEAD_DIM] for h in range(N_HEADS)], axis=0)


def _rows_to_heads(o, r):
    return jnp.concatenate([o[h * r:(h + 1) * r, :] for h in range(N_HEADS)], axis=1)


def _tile_rows(a, n):
    return jnp.concatenate([a] * n, axis=0)


def _kth_threshold(count_ge, k, rows):
    zero = jnp.zeros((rows, 1), jnp.int32)
    t0 = jnp.where(count_ge(zero) >= k, zero, jnp.full((rows, 1), INT_MIN, jnp.int32))

    def body(b, t):
        cand = t | jnp.left_shift(jnp.int32(1), 30 - b)
        return jnp.where(count_ge(cand) >= k, cand, t)

    return lax.fori_loop(0, 31, body, t0)


def _dsa_prompt_kernel(qidx_ref, amisc_ref, qa_ref, kidx_ref, akv_ref, bias_ref, o_ref, key_scr, m_scr, l_scr, acc_scr,
                       *, topk, kt):
    i = pl.program_id(1)
    chunk = kt // TKB
    nchunk = i // chunk + 1
    amisc = amisc_ref[...]
    qidx = qidx_ref[...]
    qh = [qidx[:, h * IDX_DIM:(h + 1) * IDX_DIM] for h in range(IDX_HEADS)]
    wb = [jnp.broadcast_to(amisc[:, IDX_DIM + h:IDX_DIM + h + 1], (TQ, LANES)) for h in range(IDX_HEADS)]
    qpos = i * TQ + lax.broadcasted_iota(jnp.int32, (TQ, kt), 0)
    kcol = lax.broadcasted_iota(jnp.int32, (TQ, kt), 1)

    def score_chunk(c, carry):
        off = pl.multiple_of(c * kt, kt)
        kb = kidx_ref[pl.ds(off, kt), :]
        acc = None
        for h in range(IDX_HEADS):
            term = jnp.maximum(_dot_nt(qh[h], kb), 0.0) * jnp.tile(wb[h], (1, chunk))
            acc = term if acc is None else acc + term
        key_scr[:, pl.ds(off, kt)] = jnp.where(off + kcol <= qpos, _sortable(acc), INT_MIN)
        return carry

    lax.fori_loop(0, nchunk, score_chunk, 0)

    cw = kt

    def count(pred):
        def body(c, acc):
            kc = key_scr[:, pl.ds(pl.multiple_of(c * cw, cw), cw)]
            hit = jnp.where(pred(kc, c * cw), 1.0, 0.0)
            part = hit[:, 0:TKB]
            for u in range(1, chunk):
                part = part + hit[:, u * TKB:(u + 1) * TKB]
            return acc + part
        acc = lax.fori_loop(0, nchunk, body, jnp.zeros((TQ, TKB), F32))
        return jnp.sum(acc, axis=1, keepdims=True).astype(jnp.int32)

    thr = _kth_threshold(lambda cand: count(lambda kc, o: kc >= cand), topk, TQ)

    n_gt = count(lambda kc, o: kc > thr)
    n_eq = count(lambda kc, o: kc == thr)
    need = topk - n_gt
    tie = (n_eq > need) & (thr > INT_MIN)
    any_tie = jnp.max(jnp.where(tie, 1.0, 0.0)) > 0.0
    lane_c = lax.broadcasted_iota(jnp.int32, (TQ, cw), 1)

    @pl.when(any_tie)
    def _():
        nbits = max(1, int(math.ceil(math.log2(key_scr.shape[1]))))

        def bit_body(b, lo):
            cand = lo + jnp.left_shift(jnp.int32(1), nbits - 1 - b)
            cnt = count(lambda kc, o: (kc == thr) & ((lane_c + o) < cand))
            return jnp.where(cnt < need, cand, lo)

        jmax = lax.fori_loop(0, nbits, bit_body, jnp.zeros((TQ, 1), jnp.int32))

        def demote(c, carry):
            sl = pl.ds(pl.multiple_of(c * cw, cw), cw)
            kc = key_scr[:, sl]
            drop = tie & (kc == thr) & ((lane_c + c * cw) > jmax)
            key_scr[:, sl] = jnp.where(drop, thr - 1, kc)
            return carry

        lax.fori_loop(0, nchunk, demote, 0)

    thr_eff = jnp.maximum(thr, INT_MIN + 1)
    q4 = _pad_lanes(_heads_to_rows(qa_ref[...]))
    _osm_ref_init(m_scr, l_scr, acc_scr)

    def att_chunk(c, carry):
        off = pl.multiple_of(c * kt, kt)
        kv = akv_ref[pl.ds(off, kt), :]
        sel = key_scr[:, pl.ds(off, kt)] >= thr_eff
        s = _dot_nt(q4, kv) + _tile_bias(bias_ref, i, c * chunk, chunk)
        _osm_ref_step(s, _tile_rows(sel, N_HEADS), kv, m_scr, l_scr, acc_scr)
        return carry

    lax.fori_loop(0, nchunk, att_chunk, 0)
    o4 = acc_scr[...] / l_scr[...]
    o_ref[...] = _rows_to_heads(o4[:, HEAD_DIM:2 * HEAD_DIM], TQ).astype(BF)


def _dsa_prompt(pr, bias, b, t):
    nq = t // TQ
    topk = min(DSA_TOPK, t // 4)
    rows = N_HEADS * TQ
    q_spec = lambda w: pl.BlockSpec((TQ, w), lambda bi, qi: (bi * nq + qi, 0))
    kv_spec = lambda w: pl.BlockSpec((t, w), lambda bi, qi: (bi, 0))
    return pl.pallas_call(
        functools.partial(_dsa_prompt_kernel, topk=topk, kt=_key_chunk(t)),
        grid=(b, nq),
        in_specs=[q_spec(IDX_HEADS * IDX_DIM), q_spec(128), q_spec(MIX_W), kv_spec(IDX_DIM), kv_spec(128),
                  pl.BlockSpec((3, N_HEADS, TQ, TKB), lambda bi, qi: (0, 0, 0, 0))],
        out_specs=q_spec(MIX_W),
        out_shape=jax.ShapeDtypeStruct((b * t, MIX_W), BF),
        scratch_shapes=[pltpu.VMEM((TQ, t), jnp.int32), pltpu.VMEM((rows, LANES), F32),
                        pltpu.VMEM((rows, LANES), F32), pltpu.VMEM((rows, LANES), F32)],
        compiler_params=_cparams(("arbitrary", "arbitrary")),
        name="dsa_prompt",
    )(pr['qidx'], pr['amisc'], pr['qa'], pr['kidx_b'], pr['akv_b'], bias)


def _compress_kernel(*refs, n_in):
    x_refs, (pe_ref, w_ref, o_ref) = refs[:n_in], refs[n_in:]
    x = x_refs[0][...] if n_in == 1 else jnp.concatenate([r[...] for r in x_refs], axis=0)
    o_ref[:, 0:128] = _dot((x + pe_ref[0:1, :]).astype(BF), w_ref[0])
    o_ref[:, 128:256] = _dot((x + pe_ref[1:2, :]).astype(BF), w_ref[1])


def _compress_weights(lw):
    w4 = lw['cmp_w'].reshape(2, CMP_LEN, HEAD_DIM, HEAD_DIM)
    pe = lw['cmp_pe']

    def half(rs):
        wk, wv = w4[0, rs], w4[1, rs]
        z = jnp.zeros_like(wk)
        rows = jnp.stack([jnp.concatenate([wk, z], axis=-1), jnp.concatenate([z, wv], axis=-1)], axis=1)
        return rows.reshape(CMP_STRIDE * 2 * HEAD_DIM, 2 * HEAD_DIM)

    top, bot = slice(0, CMP_STRIDE), slice(CMP_STRIDE, CMP_LEN)
    w = jnp.stack([half(top), half(bot)]).astype(BF)
    pef = jnp.stack([jnp.transpose(pe[:, top], (1, 0, 2)).reshape(-1), jnp.transpose(pe[:, bot], (1, 0, 2)).reshape(-1)])
    return pef, w


def _compress_dense(chunks, lw, rows_per_step):
    n, width = chunks.shape
    pef, w = _compress_weights(lw)
    return pl.pallas_call(
        functools.partial(_compress_kernel, n_in=1),
        grid=(n // rows_per_step,),
        in_specs=[pl.BlockSpec((rows_per_step, width), lambda i: (i, 0)), pl.BlockSpec((2, width), lambda i: (0, 0)),
                  pl.BlockSpec((2, width, 128), lambda i: (0, 0, 0))],
        out_specs=pl.BlockSpec((rows_per_step, 256), lambda i: (i, 0)),
        out_shape=jax.ShapeDtypeStruct((n, 256), F32),
        compiler_params=_cparams(("arbitrary",)),
        name="compress_prompt",
    )(chunks, pef, w)


def _combine_compressed(ab, gk):
    n = ab.shape[0]
    kv = ab[:, 0:128] + pltpu.roll(ab[:, 128:256], n - 1, axis=0)
    ck_raw = kv[:, 0:HEAD_DIM]
    ms = jnp.mean(ck_raw * ck_raw, axis=-1, keepdims=True)
    ck = (ck_raw * lax.rsqrt(ms + EPS)) * gk
    return ck.astype(BF), kv[:, HEAD_DIM:2 * HEAD_DIM].astype(BF)


def _masked_softmax_rows(s, valid):
    s = jnp.where(valid, s, NEG)
    m = jnp.max(s, axis=1, keepdims=True)
    p = jnp.where(valid, jnp.exp(s - m), 0.0)
    l = jnp.sum(p, axis=1, keepdims=True)
    return p * (1.0 / jnp.where(l > 0.0, l, 1.0))


def _gate_cols(g, c):
    return jnp.concatenate([g[:, h * 3 + c:h * 3 + c + 1] for h in range(N_HEADS)], axis=0)


def _nsa_prompt_kernel(qb_ref, gates_ref, ab_ref, gk_ref, cbias_ref, gmat_ref, emat_ref, slc_ref, win_ref, bias_ref,
                       o_ref, ck_scr, cv_scr, selm_scr, m_scr, l_scr, acc_scr, *, n_sel, ns, nch, kt, wt, nq):
    i = pl.program_id(1)

    @pl.when(i == 0)
    def _():
        ck, cv = _combine_compressed(ab_ref[...], gk_ref[...])
        ck_scr[...] = ck
        cv_scr[...] = cv

    q4 = _heads_to_rows(qb_ref[...])
    qpos = i * TQ + lax.broadcasted_iota(jnp.int32, (TQ, 1), 0)

    cend = lax.broadcasted_iota(jnp.int32, (TQ, nch), 1) * CMP_STRIDE + (CMP_LEN - 1)
    cvalid = cend <= qpos
    s = _dot_nt(q4, ck_scr[...]) + cbias_ref[...].reshape(N_HEADS * TQ, nch)
    pc = _masked_softmax_rows(s, _tile_rows(cvalid, N_HEADS))
    o_c = _dot(pc.astype(BF), cv_scr[...])
    pcs = pc[0:TQ] + pc[TQ:2 * TQ] + pc[2 * TQ:3 * TQ] + pc[3 * TQ:4 * TQ]
    imp = _dot3(pcs, gmat_ref[...])

    j_io = lax.broadcasted_iota(jnp.int32, (TQ, ns), 1)
    cur = qpos // SEL_BLOCK
    forced = (j_io == 0) | (j_io == cur) | (j_io == cur - 1)
    score = jnp.where(j_io <= cur, jnp.where(forced, FORCE, imp), NEG)
    rank = jnp.zeros((TQ, ns), F32)
    for jj in range(ns):
        col = score[:, jj:jj + 1]
        ahead = (col > score) | ((col == score) & (j_io > jj))
        rank = rank + jnp.where(ahead, 1.0, 0.0)
    sel = (rank < n_sel) & (j_io <= cur)
    selm_scr[...] = _dot(jnp.where(sel, 1.0, 0.0).astype(BF), emat_ref[...])

    q4p = _pad_lanes(q4)
    chunk = kt // TKB
    qrow = i * TQ + lax.broadcasted_iota(jnp.int32, (TQ, kt), 0)
    kcol = lax.broadcasted_iota(jnp.int32, (TQ, kt), 1)
    _osm_ref_init(m_scr, l_scr, acc_scr)

    def sel_chunk(c, carry):
        off = pl.multiple_of(c * kt, kt)
        kv = slc_ref[pl.ds(off, kt), :]
        mask = (selm_scr[:, pl.ds(off, kt)] > 0.5) & (off + kcol <= qrow)
        s = _dot_nt(q4p, kv) + _tile_bias(bias_ref, i, c * chunk, chunk)
        _osm_ref_step(s, _tile_rows(mask, N_HEADS), kv, m_scr, l_scr, acc_scr)
        return carry

    lax.fori_loop(0, i // chunk + 1, sel_chunk, 0)
    o_s = (acc_scr[...] / l_scr[...])[:, HEAD_DIM:2 * HEAD_DIM]

    w0 = jnp.clip(i - (wt - 1), 0, nq - wt)
    woff = pl.multiple_of(w0 * TKB, TKB)
    kv = win_ref[pl.ds(woff, wt * TKB), :]
    wd = (i * TQ + lax.broadcasted_iota(jnp.int32, (TQ, wt * TKB), 0)) - (
        woff + lax.broadcasted_iota(jnp.int32, (TQ, wt * TKB), 1))
    s = _dot_nt(q4p, kv) + _tile_bias(bias_ref, i, w0, wt)
    pw = _masked_softmax_rows(s, _tile_rows((wd >= 0) & (wd <= WINDOW), N_HEADS))
    o_w = _dot(pw.astype(BF), kv)[:, HEAD_DIM:2 * HEAD_DIM]

    g = gates_ref[...]
    o = _gate_cols(g, 0) * o_c + _gate_cols(g, 1) * o_s + _gate_cols(g, 2) * o_w
    o_ref[...] = _rows_to_heads(o, TQ).astype(BF)


def _nsa_prompt(pr, ab, lw, tabs, b, t):
    nq = t // TQ
    nch = t // CMP_STRIDE
    ns = t // SEL_BLOCK
    n_sel = min(N_SEL, ns)
    q_spec = lambda w: pl.BlockSpec((TQ, w), lambda bi, qi: (bi * nq + qi, 0))
    kv_spec = lambda w: pl.BlockSpec((t, w), lambda bi, qi: (bi, 0))
    const2 = lambda bi, qi: (0, 0)
    return pl.pallas_call(
        functools.partial(_nsa_prompt_kernel, n_sel=n_sel, ns=ns, nch=nch, kt=_key_chunk(t),
                          wt=min(WINDOW // TKB + 1, nq), nq=nq),
        grid=(b, nq),
        in_specs=[q_spec(MIX_W), q_spec(128), pl.BlockSpec((nch, 256), lambda bi, qi: (bi, 0)),
                  pl.BlockSpec((1, HEAD_DIM), const2),
                  pl.BlockSpec((None, N_HEADS, TQ, nch), lambda bi, qi: (qi, 0, 0, 0)),
                  pl.BlockSpec((nch, ns), const2), pl.BlockSpec((ns, t), const2),
                  kv_spec(128), kv_spec(128),
                  pl.BlockSpec((3, N_HEADS, TQ, TKB), lambda bi, qi: (0, 0, 0, 0))],
        out_specs=q_spec(MIX_W),
        out_shape=jax.ShapeDtypeStruct((b * t, MIX_W), BF),
        scratch_shapes=[pltpu.VMEM((nch, HEAD_DIM), BF), pltpu.VMEM((nch, HEAD_DIM), BF), pltpu.VMEM((TQ, t), F32),
                        pltpu.VMEM((N_HEADS * TQ, LANES), F32), pltpu.VMEM((N_HEADS * TQ, LANES), F32),
                        pltpu.VMEM((N_HEADS * TQ, LANES), F32)],
        compiler_params=_cparams(("arbitrary", "arbitrary")),
        name="nsa_prompt",
    )(pr['qb'], pr['gates'], ab, lw['gk_b'], tabs['cbias_p'], tabs['gmat_p'], tabs['emat_p'],
      pr['bslc_b'], pr['bwin_b'], tabs['nsa_tiles'])


def _fox_prompt_kernel(q_ref, cq_ref, ckv_ref, ckt_ref, o_ref, m_scr, l_scr, acc_scr, *, kt):
    i = pl.program_id(1)
    chunk = kt // TKB
    nfull = i // chunk
    q4 = _head_block_diag(q_ref[...])
    cq = cq_ref[...]
    cq4 = jnp.concatenate([jnp.broadcast_to(cq[:, h:h + 1], (TQ, LANES)) for h in range(N_HEADS)], axis=0)
    qpos = i * TQ + lax.broadcasted_iota(jnp.int32, (TQ, kt), 0)
    kcol = lax.broadcasted_iota(jnp.int32, (TQ, kt), 1)
    _osm_ref_init(m_scr, l_scr, acc_scr)

    def do_chunk(c, masked):
        off = pl.multiple_of(c * kt, kt)
        kv = ckv_ref[pl.ds(off, kt), :]
        ck4 = jnp.concatenate([jnp.broadcast_to(ckt_ref[h:h + 1, pl.ds(off, kt)], (TQ, kt)) for h in range(N_HEADS)],
                              axis=0)
        s = (_dot_nt(q4, kv[:, 0:MIX_W]) + jnp.tile(cq4, (1, chunk))) - ck4
        mask = _tile_rows(off + kcol <= qpos, N_HEADS) if masked else None
        _osm_ref_step(s, mask, kv[:, MIX_W:2 * MIX_W], m_scr, l_scr, acc_scr)

    def body(c, carry):
        do_chunk(c, False)
        return carry

    lax.fori_loop(0, nfull, body, 0)
    do_chunk(nfull, True)
    o4 = acc_scr[...] / jnp.tile(l_scr[...], (1, MIX_W // LANES))
    o_ref[...] = _head_diag_pick(o4, TQ).astype(BF)


def _fox_prompt(pr, ckt, b, t):
    nq = t // TQ
    rows = N_HEADS * TQ
    q_spec = lambda w: pl.BlockSpec((TQ, w), lambda bi, qi: (bi * nq + qi, 0))
    return pl.pallas_call(
        functools.partial(_fox_prompt_kernel, kt=_key_chunk(t)),
        grid=(b, nq),
        in_specs=[q_spec(MIX_W), q_spec(128), pl.BlockSpec((t, 2 * MIX_W), lambda bi, qi: (bi, 0)),
                  pl.BlockSpec((SUBLANES, t), lambda bi, qi: (bi, 0))],
        out_specs=q_spec(MIX_W),
        out_shape=jax.ShapeDtypeStruct((b * t, MIX_W), BF),
        scratch_shapes=[pltpu.VMEM((rows, LANES), F32), pltpu.VMEM((rows, LANES), F32), pltpu.VMEM((rows, MIX_W), F32)],
        compiler_params=_cparams(("arbitrary", "arbitrary")),
        name="fox_prompt",
    )(pr['qc'], pr['call'], pr['ckv_b'], ckt)


def _stick_terms(z):
    e = jnp.log(1.0 + jnp.exp(-jnp.abs(z)))
    return -(jnp.maximum(z, 0.0) + e), jnp.minimum(z, 0.0) - e


def _suffix_sums(x, uaug, ntiles):
    r = x.shape[0]
    stack = x if ntiles == 1 else jnp.concatenate([x[:, u * TKB:(u + 1) * TKB] for u in range(ntiles)], axis=0)
    hi = stack.astype(BF)
    lo = (stack - hi.astype(F32)).astype(BF)
    rs = _dot(hi, uaug) + _dot(lo, uaug)
    return [(rs[u * r:(u + 1) * r, 0:TKB], rs[u * r:(u + 1) * r, TKB:2 * TKB]) for u in range(ntiles)]


def _stick_prompt_kernel(q_ref, dkv_ref, u_ref, o_ref, acc_scr, run_scr, *, kt):
    i = pl.program_id(1)
    chunk = kt // TKB
    last = i // chunk
    q4 = _head_block_diag(q_ref[...])
    uaug = u_ref[...]
    qpos = i * TQ + lax.broadcasted_iota(jnp.int32, (TQ, kt), 0)
    kcol = lax.broadcasted_iota(jnp.int32, (TQ, kt), 1)
    acc_scr[...] = jnp.zeros(acc_scr.shape, F32)
    run_scr[...] = jnp.zeros(run_scr.shape, F32)

    def do_chunk(c, masked):
        off = pl.multiple_of(c * kt, kt)
        kv = dkv_ref[pl.ds(off, kt), :]
        nsp, lsig = _stick_terms(_dot_nt(q4, kv[:, 0:MIX_W]))
        if masked:
            ok = _tile_rows(off + kcol < qpos, N_HEADS)
            nsp = jnp.where(ok, nsp, 0.0)
        sums = _suffix_sums(nsp, uaug, chunk)
        run = run_scr[...]
        pieces = [None] * chunk
        for u in reversed(range(chunk)):
            pieces[u] = sums[u][0] + run
            run = run + sums[u][1]
        run_scr[...] = run
        a = jnp.exp(lsig + (pieces[0] if chunk == 1 else jnp.concatenate(pieces, axis=1)))
        if masked:
            a = jnp.where(ok, a, 0.0)
        acc_scr[...] = acc_scr[...] + _dot(a.astype(BF), kv[:, MIX_W:2 * MIX_W])

    def body(jj, carry):
        @pl.when(jj == 0)
        def _():
            do_chunk(last, True)

        @pl.when(jj > 0)
        def _():
            do_chunk(last - jj, False)

        return carry

    lax.fori_loop(0, last + 1, body, 0)
    o_ref[...] = _head_diag_pick(acc_scr[...], TQ).astype(BF)


def _stick_prompt(pr, uaug, b, t):
    nq = t // TQ
    rows = N_HEADS * TQ
    q_spec = lambda w: pl.BlockSpec((TQ, w), lambda bi, qi: (bi * nq + qi, 0))
    return pl.pallas_call(
        functools.partial(_stick_prompt_kernel, kt=_key_chunk(t)),
        grid=(b, nq),
        in_specs=[q_spec(MIX_W), pl.BlockSpec((t, 2 * MIX_W), lambda bi, qi: (bi, 0)),
                  pl.BlockSpec((TKB, 2 * TKB), lambda bi, qi: (0, 0))],
        out_specs=q_spec(MIX_W),
        out_shape=jax.ShapeDtypeStruct((b * t, MIX_W), BF),
        scratch_shapes=[pltpu.VMEM((rows, MIX_W), F32), pltpu.VMEM((rows, LANES), F32)],
        compiler_params=_cparams(("arbitrary", "arbitrary")),
        name="stick_prompt",
    )(pr['qd'], pr['dkv_b'], uaug)


PAGE = 128
TD = SUBLANES


def _page_specs(page_shape, l, pg, page_of):
    tail = (0,) * len(page_shape)

    def spec(i):
        def index_map(*args):
            pt = args[-1]
            return (l, pt[args[0], page_of(*args[1:-1], i)]) + tail
        return pl.BlockSpec((None, None) + page_shape, index_map)

    return [spec(i) for i in range(pg)]


def _pad_rows(a, n):
    return jnp.concatenate([a, jnp.zeros((n - a.shape[0], a.shape[1]), a.dtype)], axis=0)


def _topk_select_ref(key_ref, k):
    rows, width = key_ref.shape

    def count(pred):
        return jnp.sum(jnp.where(pred(key_ref[...]), 1.0, 0.0), axis=1, keepdims=True).astype(jnp.int32)

    thr = _kth_threshold(lambda cand: count(lambda kk: kk >= cand), k, rows)
    n_gt = count(lambda kk: kk > thr)
    n_eq = count(lambda kk: kk == thr)
    need = k - n_gt
    tie = (n_eq > need) & (thr > INT_MIN)
    any_tie = jnp.max(jnp.where(tie, 1.0, 0.0)) > 0.0

    @pl.when(any_tie)
    def _():
        idx = lax.broadcasted_iota(jnp.int32, (rows, width), 1)
        nbits = max(1, int(math.ceil(math.log2(width))))

        def bit_body(b, lo):
            cand = lo + jnp.left_shift(jnp.int32(1), nbits - 1 - b)
            cnt = count(lambda kk: (kk == thr) & (idx < cand))
            return jnp.where(cnt < need, cand, lo)

        jmax = lax.fori_loop(0, nbits, bit_body, jnp.zeros((rows, 1), jnp.int32))
        kk = key_ref[...]
        key_ref[...] = jnp.where(tie & (kk == thr) & (idx > jmax), thr - 1, kk)

    return jnp.maximum(thr, INT_MIN + 1)


def _osm_scratch_update(s, mask, v, m_scr, l_scr, acc_scr):
    m, l, acc = _osm_update(s, mask, m_scr[...], l_scr[...], acc_scr[...], v)
    m_scr[...] = m
    l_scr[...] = l
    acc_scr[...] = acc


def _osm_scratch_init(m_scr, l_scr, acc_scr):
    m_scr[...] = jnp.full(m_scr.shape, NEG, F32)
    l_scr[...] = jnp.zeros(l_scr.shape, F32)
    acc_scr[...] = jnp.zeros(acc_scr.shape, F32)


def _head_block_diag(q):
    lane = lax.broadcasted_iota(jnp.int32, q.shape, 1)
    return jnp.concatenate([jnp.where(lane // HEAD_DIM == h, q, jnp.zeros_like(q)) for h in range(N_HEADS)], axis=0)


def _head_diag_pick(o, r):
    lane = lax.broadcasted_iota(jnp.int32, (r, o.shape[1]), 1)
    out = None
    for h in range(N_HEADS):
        part = jnp.where(lane // HEAD_DIM == h, o[h * r:(h + 1) * r, :], 0.0)
        out = part if out is None else out + part
    return out


def _causal_new_mask(strict):
    lane = lax.broadcasted_iota(jnp.int32, (TD, PAGE), 1)
    rowi = lax.broadcasted_iota(jnp.int32, (TD, PAGE), 0)
    return lane < rowi if strict else lane <= rowi


def _compress_paged(cache, page_table, l, lw, nstep, pg):
    s = page_table.shape[0]
    pef, w = _compress_weights(lw)
    width = pef.shape[1]
    rows = PAGE // CMP_STRIDE

    def body(pt_ref, *refs):
        _compress_kernel(*refs, n_in=pg)

    return pl.pallas_call(
        body,
        grid_spec=pltpu.PrefetchScalarGridSpec(
            num_scalar_prefetch=1, grid=(s, nstep),
            in_specs=_page_specs((rows, width), l, pg, lambda p, i: p * pg + i)
            + [pl.BlockSpec((2, width), lambda si, p, pt: (0, 0)), pl.BlockSpec((2, width, 128), lambda si, p, pt: (0, 0, 0))],
            out_specs=pl.BlockSpec((None, pg * rows, 256), lambda si, p, pt: (si, p, 0))),
        out_shape=jax.ShapeDtypeStruct((s, nstep * pg * rows, 256), F32),
        compiler_params=_cparams(("arbitrary", "arbitrary")),
        name="compress_decode",
    )(page_table, *([cache] * pg), pef, w)


def _dsa_decode_kernel(pt_ref, qidx_ref, amisc_ref, qa_ref, kidxn_ref, akvn_ref, blast_ref, bfar_ref, bnew_ref, *rest,
                       topk, nstep, pg):
    kid_refs, kv_refs = rest[:pg], rest[pg:2 * pg]
    o_ref, key_scr, thr_scr, m_scr, l_scr, acc_scr = rest[2 * pg:]
    ph, p = pl.program_id(1), pl.program_id(2)
    wstep = pg * PAGE
    npast = nstep * wstep
    last = p == nstep - 1

    @pl.when(ph == 0)
    def _():
        qidx = qidx_ref[...]
        w = amisc_ref[:, IDX_DIM:IDX_DIM + IDX_HEADS]
        q64 = jnp.concatenate([qidx[:, h * IDX_DIM:(h + 1) * IDX_DIM] for h in range(IDX_HEADS)], axis=0)

        def scores(keys):
            sc = _dot_nt(q64, keys)
            acc = None
            for h in range(IDX_HEADS):
                term = jnp.maximum(sc[h * TD:(h + 1) * TD, :], 0.0) * w[:, h:h + 1]
                acc = term if acc is None else acc + term
            return acc

        kid = jnp.concatenate([r[...] for r in kid_refs], axis=0).astype(BF)
        key_scr[:, pl.ds(pl.multiple_of(p * wstep, wstep), wstep)] = _sortable(scores(kid))

        @pl.when(last)
        def _():
            acc = scores(_pad_rows(kidxn_ref[...], PAGE))
            key_scr[:, npast:npast + PAGE] = jnp.where(_causal_new_mask(False), _sortable(acc), INT_MIN)
            thr_scr[...] = jnp.broadcast_to(_topk_select_ref(key_scr, topk), thr_scr.shape)

    @pl.when(ph == 1)
    def _():
        @pl.when(p == 0)
        def _():
            _osm_scratch_init(m_scr, l_scr, acc_scr)

        thr = thr_scr[:, 0:1]
        q4 = _heads_to_rows(qa_ref[...])
        kv = jnp.concatenate([r[...] for r in kv_refs], axis=0)
        bias = jnp.where(last, blast_ref[...], bfar_ref[...])
        s = _dot_nt(q4, kv[:, 0:HEAD_DIM].astype(BF)) + bias
        sel = key_scr[:, pl.ds(pl.multiple_of(p * wstep, wstep), wstep)] >= thr
        _osm_scratch_update(s, _tile_rows(sel, N_HEADS), kv[:, HEAD_DIM:2 * HEAD_DIM].astype(BF), m_scr, l_scr, acc_scr)

        @pl.when(last)
        def _():
            kvn = _pad_rows(akvn_ref[...], PAGE)
            s = _dot_nt(q4, kvn[:, 0:HEAD_DIM]) + bnew_ref[...]
            sel = key_scr[:, npast:npast + PAGE] >= thr
            m, l, acc = _osm_update(s, _tile_rows(sel, N_HEADS), m_scr[...], l_scr[...], acc_scr[...],
                                    kvn[:, HEAD_DIM:2 * HEAD_DIM])
            o_ref[...] = _rows_to_heads(acc / l, TD).astype(BF)


def _seq_spec(width, ngrid):
    if ngrid == 2:
        return pl.BlockSpec((None, TD, width), lambda si, p, pt: (si, 0, 0))
    return pl.BlockSpec((None, TD, width), lambda si, ph, p, pt: (si, 0, 0))


def _const_spec(shape, ngrid):
    zeros = (0,) * len(shape)
    if ngrid == 2:
        return pl.BlockSpec(shape, lambda si, p, pt: zeros)
    return pl.BlockSpec(shape, lambda si, ph, p, pt: zeros)


def _dsa_decode(prs, cache_kidx, cache_akv, page_table, l, tabs, nstep, pg):
    s = page_table.shape[0]
    npast = nstep * pg * PAGE
    topk = min(DSA_TOPK, (npast + TD) // 4)
    wstep = pg * PAGE
    kid_specs = _page_specs((PAGE, IDX_DIM), l, pg, lambda ph, p, i: jnp.where(ph == 0, p, nstep - 1) * pg + i)
    kv_specs = _page_specs((PAGE, 128), l, pg, lambda ph, p, i: jnp.where(ph == 0, 0, p) * pg + i)
    return pl.pallas_call(
        functools.partial(_dsa_decode_kernel, topk=topk, nstep=nstep, pg=pg),
        grid_spec=pltpu.PrefetchScalarGridSpec(
            num_scalar_prefetch=1, grid=(s, 2, nstep),
            in_specs=[_seq_spec(IDX_HEADS * IDX_DIM, 3), _seq_spec(128, 3), _seq_spec(MIX_W, 3), _seq_spec(IDX_DIM, 3),
                      _seq_spec(128, 3), _const_spec((N_HEADS * TD, wstep), 3), _const_spec((N_HEADS * TD, 1), 3),
                      _const_spec((N_HEADS * TD, PAGE), 3)] + kid_specs + kv_specs,
            out_specs=_seq_spec(MIX_W, 3),
            scratch_shapes=[pltpu.VMEM((TD, npast + PAGE), jnp.int32), pltpu.VMEM((TD, 128), jnp.int32),
                            pltpu.VMEM((N_HEADS * TD, 1), F32), pltpu.VMEM((N_HEADS * TD, 1), F32),
                            pltpu.VMEM((N_HEADS * TD, HEAD_DIM), F32)]),
        out_shape=jax.ShapeDtypeStruct((s, TD, MIX_W), BF),
        compiler_params=_cparams(("arbitrary", "arbitrary", "arbitrary")),
        name="dsa_decode",
    )(page_table, prs['qidx'], prs['amisc'], prs['qa'], prs['kidx_b'], prs['akv_b'],
      tabs['a_last'], tabs['a_far'], tabs['a_new'], *([cache_kidx] * pg), *([cache_akv] * pg))


def _nsa_decode_kernel(pt_ref, qb_ref, gates_ref, ab_ref, gk_ref, cbias_ref, gmat_ref, slcn_ref, winp_ref, winn_ref,
                       blast_ref, bfar_ref, bnew_ref, wbp_ref, *rest, n_sel, nstep, pg):
    slc_refs = rest[:pg]
    o_ref, selm_scr, bkey_scr, oc_scr, m_scr, l_scr, acc_scr = rest[pg:]
    p = pl.program_id(1)
    wstep = pg * PAGE
    npast = nstep * wstep
    nch = npast // CMP_STRIDE
    nsb = npast // SEL_BLOCK
    last = p == nstep - 1
    q4 = _heads_to_rows(qb_ref[...])

    @pl.when(p == 0)
    def _():
        ck, cv = _combine_compressed(ab_ref[...], gk_ref[...])
        qpos = npast + lax.broadcasted_iota(jnp.int32, (TD, 1), 0)
        cend = lax.broadcasted_iota(jnp.int32, (TD, nch), 1) * CMP_STRIDE + (CMP_LEN - 1)
        s = _dot_nt(q4, ck) + cbias_ref[...]
        pc = _masked_softmax_rows(s, _tile_rows(cend <= qpos, N_HEADS))
        oc_scr[...] = _dot(pc.astype(BF), cv)
        pcs = pc[0:TD] + pc[TD:2 * TD] + pc[2 * TD:3 * TD] + pc[3 * TD:4 * TD]
        imp = _dot3(pcs, gmat_ref[...])
        j_io = lax.broadcasted_iota(jnp.int32, (TD, nsb), 1)
        forced = (j_io == 0) | (j_io == nsb - 1)
        bkey_scr[...] = _sortable(jnp.where(forced, FORCE, imp))
        thr = _topk_select_ref(bkey_scr, n_sel - 1)
        selb = jnp.where(bkey_scr[...] >= thr, 1.0, 0.0).astype(BF)
        blk = lax.broadcasted_iota(jnp.int32, (nsb, wstep), 0)
        col = lax.broadcasted_iota(jnp.int32, (nsb, wstep), 1)
        for c in range(nstep):
            expand = jnp.where(blk == (c * wstep + col) // SEL_BLOCK, 1.0, 0.0).astype(BF)
            selm_scr[:, c * wstep:(c + 1) * wstep] = _dot(selb, expand)
        _osm_scratch_init(m_scr, l_scr, acc_scr)

    kv = jnp.concatenate([r[...] for r in slc_refs], axis=0)
    bias = jnp.where(last, blast_ref[...], bfar_ref[...])
    s = _dot_nt(q4, kv[:, 0:HEAD_DIM].astype(BF)) + bias
    mask = selm_scr[:, pl.ds(pl.multiple_of(p * wstep, wstep), wstep)] > 0.5
    _osm_scratch_update(s, _tile_rows(mask, N_HEADS), kv[:, HEAD_DIM:2 * HEAD_DIM].astype(BF), m_scr, l_scr, acc_scr)

    @pl.when(last)
    def _():
        new_mask = _tile_rows(_causal_new_mask(False), N_HEADS)
        kvn = _pad_rows(slcn_ref[...], PAGE)
        s = _dot_nt(q4, kvn[:, 0:HEAD_DIM]) + bnew_ref[...]
        _, l_s, acc_s = _osm_update(s, new_mask, m_scr[...], l_scr[...], acc_scr[...], kvn[:, HEAD_DIM:2 * HEAD_DIM])
        wp = winp_ref[...]
        nw = wp.shape[0]
        wd = nw + lax.broadcasted_iota(jnp.int32, (TD, nw), 0) - lax.broadcasted_iota(jnp.int32, (TD, nw), 1)
        s = _dot_nt(q4, wp[:, 0:HEAD_DIM].astype(BF)) + wbp_ref[...]
        carry = _osm_update(s, _tile_rows(wd <= WINDOW, N_HEADS), jnp.full((N_HEADS * TD, 1), NEG, F32),
                            jnp.zeros((N_HEADS * TD, 1), F32), jnp.zeros((N_HEADS * TD, HEAD_DIM), F32),
                            wp[:, HEAD_DIM:2 * HEAD_DIM].astype(BF))
        kvw = _pad_rows(winn_ref[...], PAGE)
        s = _dot_nt(q4, kvw[:, 0:HEAD_DIM]) + bnew_ref[...]
        _, l_w, acc_w = _osm_update(s, new_mask, *carry, kvw[:, HEAD_DIM:2 * HEAD_DIM])
        g = gates_ref[...]
        o = _gate_cols(g, 0) * oc_scr[...] + _gate_cols(g, 1) * (acc_s / l_s) + _gate_cols(g, 2) * (acc_w / l_w)
        o_ref[...] = _rows_to_heads(o, TD).astype(BF)


def _nsa_decode(prs, ab, win_past, cache_slc, page_table, l, lw, tabs, nstep, pg):
    s = page_table.shape[0]
    wstep = pg * PAGE
    npast = nstep * wstep
    nch, nsb = npast // CMP_STRIDE, npast // SEL_BLOCK
    n_sel = min(N_SEL, nsb + 1)
    assert n_sel >= 2 and win_past.shape[1] == WINDOW
    return pl.pallas_call(
        functools.partial(_nsa_decode_kernel, n_sel=n_sel, nstep=nstep, pg=pg),
        grid_spec=pltpu.PrefetchScalarGridSpec(
            num_scalar_prefetch=1, grid=(s, nstep),
            in_specs=[_seq_spec(MIX_W, 2), _seq_spec(128, 2),
                      pl.BlockSpec((None, nch, 256), lambda si, p, pt: (si, 0, 0)), _const_spec((1, HEAD_DIM), 2),
                      _const_spec((N_HEADS * TD, nch), 2), _const_spec((nch, nsb), 2), _seq_spec(128, 2),
                      pl.BlockSpec((None, WINDOW, 128), lambda si, p, pt: (si, 0, 0)), _seq_spec(128, 2),
                      _const_spec((N_HEADS * TD, wstep), 2), _const_spec((N_HEADS * TD, 1), 2),
                      _const_spec((N_HEADS * TD, PAGE), 2), _const_spec((N_HEADS * TD, WINDOW), 2)]
            + _page_specs((PAGE, 128), l, pg, lambda p, i: p * pg + i),
            out_specs=_seq_spec(MIX_W, 2),
            scratch_shapes=[pltpu.VMEM((TD, npast), F32), pltpu.VMEM((TD, nsb), jnp.int32),
                            pltpu.VMEM((N_HEADS * TD, HEAD_DIM), F32), pltpu.VMEM((N_HEADS * TD, 1), F32),
                            pltpu.VMEM((N_HEADS * TD, 1), F32), pltpu.VMEM((N_HEADS * TD, HEAD_DIM), F32)]),
        out_shape=jax.ShapeDtypeStruct((s, TD, MIX_W), BF),
        compiler_params=_cparams(("arbitrary", "arbitrary")),
        name="nsa_decode",
    )(page_table, prs['qb'], prs['gates'], ab, lw['gk_b'], tabs['cbias_d'], tabs['gmat_d'], prs['bslc_b'], win_past,
      prs['bwin_b'], tabs['b_last'], tabs['b_far'], tabs['b_new'], tabs['wb_past'], *([cache_slc] * pg))


def _fox_decode_kernel(pt_ref, q_ref, lfn_ref, lftn_ref, ckvn_ref, u_ref, tinc_ref, *rest, nstep, pg):
    kv_refs, lf_refs = rest[:pg], rest[pg:2 * pg]
    o_ref, qbd_scr, cq_scr, m_scr, l_scr, acc_scr, carry_scr = rest[2 * pg:]
    p = pl.program_id(1)

    @pl.when(p == 0)
    def _():
        qbd = _head_block_diag(q_ref[...])
        qbd_scr[...] = qbd
        lf = lfn_ref[...]
        rows = [lf[0:1, :]]
        for r in range(1, TD):
            rows.append(rows[-1] + lf[r:r + 1, :])
        npf = jnp.concatenate(rows, axis=0)
        cq4 = jnp.concatenate([npf[:, h:h + 1] for h in range(N_HEADS)], axis=0)
        cq_scr[...] = cq4
        npt = _dot3(lftn_ref[...], tinc_ref[...])
        ck4 = jnp.concatenate([jnp.broadcast_to(npt[h:h + 1, :], (TD, PAGE)) for h in range(N_HEADS)], axis=0)
        kvn = _pad_rows(ckvn_ref[...], PAGE)
        s = (_dot_nt(qbd, kvn[:, 0:MIX_W]) + cq4) - ck4
        _osm_scratch_init(m_scr, l_scr, acc_scr)
        _osm_scratch_update(s, _tile_rows(_causal_new_mask(False), N_HEADS), kvn[:, MIX_W:2 * MIX_W], m_scr, l_scr, acc_scr)
        carry_scr[...] = jnp.zeros_like(carry_scr)

    x = jnp.concatenate([r[...] for r in lf_refs], axis=0)
    rs = _dot3(x, u_ref[...])
    tot = jnp.sum(x, axis=1, keepdims=True)
    off = carry_scr[:, 0:1]
    pieces = [None] * pg
    for i in reversed(range(pg)):
        s_i = rs[i * SUBLANES:(i + 1) * SUBLANES, :] + off
        pieces[i] = jnp.concatenate([jnp.broadcast_to(s_i[h:h + 1, :], (TD, PAGE)) for h in range(N_HEADS)], axis=0)
        off = off + tot[i * SUBLANES:(i + 1) * SUBLANES, :]
    carry_scr[...] = jnp.broadcast_to(off, carry_scr.shape)
    kv = jnp.concatenate([r[...] for r in kv_refs], axis=0)
    s = (_dot_nt(qbd_scr[...], kv[:, 0:MIX_W].astype(BF)) + cq_scr[...]) + jnp.concatenate(pieces, axis=1)
    _osm_scratch_update(s, None, kv[:, MIX_W:2 * MIX_W].astype(BF), m_scr, l_scr, acc_scr)

    @pl.when(p == nstep - 1)
    def _():
        o_ref[...] = _head_diag_pick(acc_scr[...] / l_scr[...], TD).astype(BF)


def _fox_decode(prs, lft_new, cache_ckv, cache_lft, page_table, l, tabs, nstep, pg):
    s = page_table.shape[0]
    rev = lambda p, i: (nstep - 1 - p) * pg + i
    return pl.pallas_call(
        functools.partial(_fox_decode_kernel, nstep=nstep, pg=pg),
        grid_spec=pltpu.PrefetchScalarGridSpec(
            num_scalar_prefetch=1, grid=(s, nstep),
            in_specs=[_seq_spec(MIX_W, 2), _seq_spec(128, 2), _seq_spec(128, 2), _seq_spec(2 * MIX_W, 2),
                      _const_spec((PAGE, PAGE), 2), _const_spec((PAGE, PAGE), 2)]
            + _page_specs((PAGE, 2 * MIX_W), l, pg, rev) + _page_specs((SUBLANES, PAGE), l, pg, rev),
            out_specs=_seq_spec(MIX_W, 2),
            scratch_shapes=[pltpu.VMEM((N_HEADS * TD, MIX_W), BF), pltpu.VMEM((N_HEADS * TD, 1), F32),
                            pltpu.VMEM((N_HEADS * TD, 1), F32), pltpu.VMEM((N_HEADS * TD, 1), F32),
                            pltpu.VMEM((N_HEADS * TD, MIX_W), F32), pltpu.VMEM((SUBLANES, 128), F32)]),
        out_shape=jax.ShapeDtypeStruct((s, TD, MIX_W), BF),
        compiler_params=_cparams(("arbitrary", "arbitrary")),
        name="fox_decode",
    )(page_table, prs['qc'], prs['logf'], lft_new, prs['ckv_b'], tabs['umat'], tabs['tinc'],
      *([cache_ckv] * pg), *([cache_lft] * pg))


def _stick_decode_kernel(pt_ref, q_ref, dkvn_ref, u_ref, *rest, nstep, pg):
    kv_refs = rest[:pg]
    o_ref, qbd_scr, acc_scr, run_scr = rest[pg:]
    p = pl.program_id(1)
    umat = u_ref[...]
    rows = N_HEADS * TD

    @pl.when(p == 0)
    def _():
        qbd = _head_block_diag(q_ref[...])
        qbd_scr[...] = qbd
        kvn = _pad_rows(dkvn_ref[...], PAGE)
        strict = _tile_rows(_causal_new_mask(True), N_HEADS)
        nsp, lsig = _stick_terms(_dot_nt(qbd, kvn[:, 0:MIX_W]))
        nsp = jnp.where(strict, nsp, 0.0)
        a = jnp.where(strict, jnp.exp(lsig + _dot3(nsp, umat)), 0.0)
        acc_scr[...] = _dot(a.astype(BF), kvn[:, MIX_W:2 * MIX_W])
        run_scr[...] = jnp.sum(nsp, axis=1, keepdims=True)

    kv = jnp.concatenate([r[...] for r in kv_refs], axis=0)
    nsp, lsig = _stick_terms(_dot_nt(qbd_scr[...], kv[:, 0:MIX_W].astype(BF)))
    stack = jnp.concatenate([nsp[:, i * PAGE:(i + 1) * PAGE] for i in range(pg)], axis=0)
    rs = _dot3(stack, umat)
    tot = jnp.sum(stack, axis=1, keepdims=True)
    off = run_scr[...]
    pieces = [None] * pg
    for i in reversed(range(pg)):
        pieces[i] = rs[i * rows:(i + 1) * rows, :] + off
        off = off + tot[i * rows:(i + 1) * rows, :]
    run_scr[...] = off
    a = jnp.exp(lsig + jnp.concatenate(pieces, axis=1))
    acc_scr[...] = acc_scr[...] + _dot(a.astype(BF), kv[:, MIX_W:2 * MIX_W].astype(BF))

    @pl.when(p == nstep - 1)
    def _():
        o_ref[...] = _head_diag_pick(acc_scr[...], TD).astype(BF)


def _stick_decode(prs, cache_dkv, page_table, l, tabs, nstep, pg):
    s = page_table.shape[0]
    return pl.pallas_call(
        functools.partial(_stick_decode_kernel, nstep=nstep, pg=pg),
        grid_spec=pltpu.PrefetchScalarGridSpec(
            num_scalar_prefetch=1, grid=(s, nstep),
            in_specs=[_seq_spec(MIX_W, 2), _seq_spec(2 * MIX_W, 2), _const_spec((PAGE, PAGE), 2)]
            + _page_specs((PAGE, 2 * MIX_W), l, pg, lambda p, i: (nstep - 1 - p) * pg + i),
            out_specs=_seq_spec(MIX_W, 2),
            scratch_shapes=[pltpu.VMEM((N_HEADS * TD, MIX_W), BF), pltpu.VMEM((N_HEADS * TD, MIX_W), F32),
                            pltpu.VMEM((N_HEADS * TD, 1), F32)]),
        out_shape=jax.ShapeDtypeStruct((s, TD, MIX_W), BF),
        compiler_params=_cparams(("arbitrary", "arbitrary")),
        name="stick_decode",
    )(page_table, prs['qd'], prs['dkv_b'], tabs['umat'], *([cache_dkv] * pg))


_IN_A, _IN_B, _IN_C, _IN_D = 968, 1620, 2392, 3160


def _bucket(dist):
    n = jnp.maximum(dist, 0)
    exact = N_BUCKETS // 2
    nf = jnp.maximum(n, 1).astype(F32)
    large = exact + (jnp.log(nf / exact) / math.log(MAX_DIST / exact) * (N_BUCKETS - exact)).astype(jnp.int32)
    return jnp.where(n < exact, n, jnp.minimum(large, N_BUCKETS - 1))


def _bias_table(tab, dist):
    b = _bucket(dist)
    out = jnp.zeros((tab.shape[1],) + dist.shape, F32)
    for j in range(N_BUCKETS):
        out = jnp.where(b[None] == j, tab[j].reshape((-1,) + (1,) * dist.ndim), out)
    return out


def _prep_layer(l, p, tm_max):
    w_in = p['w_in'][l]
    d = w_in.shape[0]
    z = lambda n: jnp.zeros((d, n), w_in.dtype)
    w_proj = jnp.concatenate([w_in[:, 0:_IN_A], z(1024 - _IN_A), w_in[:, _IN_A:_IN_B], z(768 - (_IN_B - _IN_A)),
                              w_in[:, _IN_B:_IN_C], z(896 - (_IN_C - _IN_B)), w_in[:, _IN_C:_IN_D]], axis=1)
    assert w_proj.shape[1] == N_PROJ
    qk = p['qk_gain'][l]
    gain = jnp.ones((N_PROJ,), F32)
    nmask = jnp.zeros((N_PROJ,), F32)
    for c0, g, rep in ((C_AQ, qk[0, 0], 4), (C_AKV, qk[0, 1], 1), (C_BQ, qk[1, 0], 4), (C_BSLC, qk[1, 1], 1),
                       (C_BWIN, qk[1, 1], 1), (C_CQ, qk[2, 0], 4), (C_CK, qk[2, 1], 4)):
        gain = gain.at[c0:c0 + rep * HEAD_DIM].set(jnp.tile(g, rep))
        nmask = nmask.at[c0:c0 + rep * HEAD_DIM].set(1.0)
    gidx = np.arange(MIX_W) // HEAD_DIM
    bd = jnp.asarray((gidx[:, None] == gidx[None, :]).astype(np.float32) / HEAD_DIM, BF)
    tri = jnp.asarray(np.tril(np.ones((tm_max, tm_max), np.float32)), BF)
    return {
        'g_attn': p['norm_attn'][l][None, :], 'w_proj': w_proj.astype(BF), 'gain': gain[None, :],
        'nmask': nmask[None, :], 'b_f': jnp.zeros((1, 128), F32).at[0, :N_HEADS].set(p['b_forget'][l]),
        'bd': bd, 'tri': tri,
        'w_gate': w_in[:, _IN_D:].astype(BF), 'w_br': p['w_branch'][l].astype(BF), 'w_o': p['w_out'][l].astype(BF),
        'g_ffn': p['norm_ffn'][l][None, :], 'w_up': p['w_up'][l].astype(BF), 'conv_w': p['conv_w'][l],
        'conv_b': p['conv_b'][l][None, :], 'w_down': p['w_down'][l].astype(BF),
        'gk_b': qk[1, 1][None, :], 'cmp_w': p['cmp_w'][l], 'cmp_pe': p['cmp_pe'][l],
    }


def _toeplitz_bias(tab):
    r = jnp.arange(TQ)[:, None]
    c = jnp.arange(TKB)[None, :]
    return jnp.stack([_bias_table(tab, dd * TKB + r - c) for dd in range(3)])


def _prompt_tables(rel_bias, t):
    tab_a, tab_b = rel_bias[:, :N_HEADS], rel_bias[:, N_HEADS:]
    nq, nch, ns = t // TQ, t // CMP_STRIDE, t // SEL_BLOCK
    qpos = jnp.arange(t).reshape(nq, TQ)
    cend = jnp.arange(nch) * CMP_STRIDE + (CMP_LEN - 1)
    cbias = jnp.transpose(_bias_table(tab_b, qpos[:, :, None] - cend[None, None, :]), (1, 0, 2, 3))
    n = np.arange(nch)
    gmat = ((n[:, None] // (SEL_BLOCK // CMP_STRIDE) == np.arange(ns)[None, :]) & (n[:, None] < nch - 1))
    emat = np.arange(t)[None, :] // SEL_BLOCK == np.arange(ns)[:, None]
    c = np.arange(TKB)
    return {
        'dsa_tiles': _toeplitz_bias(tab_a), 'nsa_tiles': _toeplitz_bias(tab_b), 'cbias_p': cbias,
        'gmat_p': jnp.asarray(gmat.astype(np.float32), BF), 'emat_p': jnp.asarray(emat.astype(np.float32), BF),
        'uaug': jnp.asarray(np.concatenate([(c[:, None] > c[None, :]), np.ones((TKB, TKB), bool)], axis=1)
                            .astype(np.float32), BF),
    }


def _layer_prompt(x, lw, tabs):
    b, t, d = x.shape
    assert t % TQ == 0
    tm = 256 if t % 256 == 0 else TQ
    x2 = x.reshape(b * t, d)
    pr = _proj(x2, lw, tm=tm, tiles_per_seq=t // tm, with_cumsum=True)
    o_a = _dsa_prompt(pr, tabs['dsa_tiles'], b, t)
    ab = _compress_dense(pr['bcmp'].reshape(b * t // CMP_STRIDE, CMP_STRIDE * 2 * HEAD_DIM), lw, t // CMP_STRIDE)
    o_b = _nsa_prompt(pr, ab, lw, tabs, b, t)
    ckt = jnp.transpose(pr['call'].reshape(b, t, 128)[:, :, :SUBLANES], (0, 2, 1)).reshape(b * SUBLANES, t)
    o_c = _fox_prompt(pr, ckt, b, t)
    o_d = _stick_prompt(pr, tabs['uaug'], b, t)
    xm = _merge(x2, o_a, o_b, o_c, o_d, lw, tm=tm)
    dummy = jnp.zeros((SUBLANES, lw['w_up'].shape[1]), F32)
    y2, conv = _ffn(xm, lw, dummy, dummy, tm=tm, carry_mode=True, tiles_per_seq=t // tm)
    keep = min(WINDOW, t)
    new = {
        'a_kv': pr['akv'].reshape(b, t, 2, HEAD_DIM),
        'a_kidx': pr['amisc'][:, :IDX_DIM].reshape(b, t, IDX_DIM),
        'b_cmp_kv': pr['bcmp'].reshape(b, t, 2, HEAD_DIM),
        'b_slc_kv': pr['bslc'].reshape(b, t, 2, HEAD_DIM),
        'b_win_kv': pr['bwin'].reshape(b, t, 2, HEAD_DIM)[:, t - keep:],
        'c_kv': pr['ckv'].reshape(b, t, 2, N_HEADS, HEAD_DIM),
        'c_logf': pr['logf'][:, :N_HEADS].reshape(b, t, N_HEADS),
        'd_kv': pr['dkv'].reshape(b, t, 2, N_HEADS, HEAD_DIM),
        'ffn_conv': conv.reshape(b, SUBLANES, -1)[:, SUBLANES - (CONV_W - 1):],
    }
    return y2.reshape(b, t, d), new, (o_a, o_b, o_c, o_d)


def _decode_tables(rel_bias, npast, pg):
    tab_a, tab_b = rel_bias[:, :N_HEADS], rel_bias[:, N_HEADS:]
    wstep = pg * PAGE
    tq = jnp.arange(TD)
    rows = lambda tab, dist: _bias_table(tab, dist).reshape(N_HEADS * TD, -1)
    far = lambda tab: jnp.repeat(tab[N_BUCKETS - 1], TD)[:, None]
    d_last = (npast + tq)[:, None] - (npast - wstep + jnp.arange(wstep))[None, :]
    d_new = tq[:, None] - jnp.arange(PAGE)[None, :]
    nch, nsb = npast // CMP_STRIDE, npast // SEL_BLOCK
    d_cmp = (npast + tq)[:, None] - (jnp.arange(nch) * CMP_STRIDE + (CMP_LEN - 1))[None, :]
    d_win = WINDOW + tq[:, None] - jnp.arange(WINDOW)[None, :]
    n = np.arange(nch)
    gmat = (n[:, None] // (SEL_BLOCK // CMP_STRIDE) == np.arange(nsb)[None, :]) & (n[:, None] < nch - 1)
    c = np.arange(PAGE)
    return {
        'a_last': rows(tab_a, d_last), 'a_far': far(tab_a), 'a_new': rows(tab_a, d_new),
        'b_last': rows(tab_b, d_last), 'b_far': far(tab_b), 'b_new': rows(tab_b, d_new),
        'cbias_d': rows(tab_b, d_cmp), 'wb_past': rows(tab_b, d_win),
        'gmat_d': jnp.asarray(gmat.astype(np.float32), BF),
        'umat': jnp.asarray((c[:, None] > c[None, :]).astype(np.float32), BF),
        'tinc': jnp.asarray((c[:, None] <= c[None, :]).astype(np.float32), BF),
    }


def _layer_decode(x, lw, tabs, l, caches, ffn_state, page_table, nstep, pg):
    s, td, d = x.shape
    assert td == TD
    m = s * td
    x2 = x.reshape(m, d)
    pr = _proj(x2, lw, tm=m, tiles_per_seq=1, with_cumsum=False)
    prs = {k: v.reshape(s, td, v.shape[-1]) for k, v in pr.items()}
    o_a = _dsa_decode(prs, caches['a_kidx'], caches['a_kv'], page_table, l, tabs, nstep, pg)
    ab = _compress_paged(caches['b_cmp'], page_table, l, lw, nstep, pg)
    o_b = _nsa_decode(prs, ab, caches['b_win'][l], caches['b_slc'], page_table, l, lw, tabs, nstep, pg)
    lft_new = jnp.pad(jnp.transpose(prs['logf'][:, :, :SUBLANES], (0, 2, 1)), ((0, 0), (0, 0), (0, PAGE - td)))
    o_c = _fox_decode(prs, lft_new, caches['c_kv'], caches['c_lft'], page_table, l, tabs, nstep, pg)
    o_d = _stick_decode(prs, caches['d_kv'], page_table, l, tabs, nstep, pg)
    flat = lambda o: o.reshape(m, MIX_W)
    xm = _merge(x2, flat(o_a), flat(o_b), flat(o_c), flat(o_d), lw, tm=m)
    st0 = jnp.repeat(ffn_state[:, 0], td, axis=0)
    st1 = jnp.repeat(ffn_state[:, 1], td, axis=0)
    y2, conv = _ffn(xm, lw, st0, st1, tm=m, carry_mode=False, tiles_per_seq=1)
    win_new = pr['bwin'].reshape(s, td, 2, HEAD_DIM)
    win_all = jnp.concatenate([caches['b_win'][l].reshape(s, -1, 2, HEAD_DIM), win_new], axis=1)
    keep = min(WINDOW, win_all.shape[1])
    new = {
        'a_kv': pr['akv'].reshape(s, td, 2, HEAD_DIM),
        'a_kidx': pr['amisc'][:, :IDX_DIM].reshape(s, td, IDX_DIM),
        'b_cmp_kv': pr['bcmp'].reshape(s, td, 2, HEAD_DIM),
        'b_slc_kv': pr['bslc'].reshape(s, td, 2, HEAD_DIM),
        'b_win_kv': win_all[:, win_all.shape[1] - keep:],
        'c_kv': pr['ckv'].reshape(s, td, 2, N_HEADS, HEAD_DIM),
        'c_logf': pr['logf'][:, :N_HEADS].reshape(s, td, N_HEADS),
        'd_kv': pr['dkv'].reshape(s, td, 2, N_HEADS, HEAD_DIM),
        'ffn_conv': conv.reshape(s, td, -1)[:, td - (CONV_W - 1):],
    }
    return y2.reshape(s, td, d), new, (o_a, o_b, o_c, o_d)


_STATE_KEYS = ('a_kv', 'a_kidx', 'b_cmp_kv', 'b_slc_kv', 'b_win_kv', 'c_kv', 'c_logf', 'd_kv', 'ffn_conv')


def kernel(x_prompt, x_sample, cache_a_kv, cache_a_kidx, cache_b_cmp_kv, cache_b_slc_kv, state_b_win_kv,
           cache_c_kv, cache_c_logf, cache_d_kv, state_ffn_conv, page_table, rel_bias, norm_attn, w_in,
           b_forget, qk_gain, cmp_w, cmp_pe, w_branch, w_out, norm_ffn, w_up, conv_w, conv_b, w_down):
    params = dict(norm_attn=norm_attn, w_in=w_in, b_forget=b_forget, qk_gain=qk_gain, cmp_w=cmp_w, cmp_pe=cmp_pe,
                  w_branch=w_branch, w_out=w_out, norm_ffn=norm_ffn, w_up=w_up, conv_w=conv_w, conv_b=conv_b,
                  w_down=w_down)
    depth, n_pool = w_in.shape[0], cache_a_kv.shape[1]
    n_seq, n_pages = page_table.shape
    assert cache_a_kv.shape[2] == PAGE
    pg = PAGES_PER_STEP if n_pages % PAGES_PER_STEP == 0 else 1
    nstep = n_pages // pg
    npast = n_pages * PAGE
    caches = {
        'a_kv': cache_a_kv.reshape(depth, n_pool, PAGE, 2 * HEAD_DIM),
        'a_kidx': cache_a_kidx,
        'b_cmp': cache_b_cmp_kv.reshape(depth, n_pool, PAGE // CMP_STRIDE, CMP_STRIDE * 2 * HEAD_DIM),
        'b_slc': cache_b_slc_kv.reshape(depth, n_pool, PAGE, 2 * HEAD_DIM),
        'b_win': state_b_win_kv.reshape(depth, n_seq, state_b_win_kv.shape[2], 2 * HEAD_DIM),
        'c_kv': cache_c_kv.reshape(depth, n_pool, PAGE, 2 * MIX_W),
        'c_lft': jnp.pad(jnp.transpose(cache_c_logf.astype(F32), (0, 1, 3, 2)),
                         ((0, 0), (0, 0), (0, SUBLANES - N_HEADS), (0, 0))),
        'd_kv': cache_d_kv.reshape(depth, n_pool, PAGE, 2 * MIX_W),
    }
    tabs_p = _prompt_tables(rel_bias, x_prompt.shape[1])
    tabs_d = _decode_tables(rel_bias, npast, pg)
    y_p, y_s = x_prompt, x_sample
    new_p = {k: [] for k in _STATE_KEYS}
    new_s = {k: [] for k in _STATE_KEYS}
    for l in range(depth):
        lw = _prep_layer(l, params, 256)
        y_p, st, _ = _layer_prompt(y_p, lw, tabs_p)
        for k in _STATE_KEYS:
            new_p[k].append(st[k])
        y_s, st, _ = _layer_decode(y_s, lw, tabs_d, l, caches, state_ffn_conv[l], page_table, nstep, pg)
        for k in _STATE_KEYS:
            new_s[k].append(st[k])
    sp = {k: jnp.stack(v) for k, v in new_p.items()}
    ss = {k: jnp.stack(v) for k, v in new_s.items()}
    return (y_p, y_s,
            sp['a_kv'], ss['a_kv'], sp['a_kidx'], ss['a_kidx'],
            sp['b_cmp_kv'], ss['b_cmp_kv'], sp['b_slc_kv'], ss['b_slc_kv'],
            sp['b_win_kv'], ss['b_win_kv'], sp['c_kv'], ss['c_kv'],
            sp['c_logf'], ss['c_logf'], sp['d_kv'], ss['d_kv'],
            sp['ffn_conv'], ss['ffn_conv'])
```

```python
import functools
import math

import jax
import jax.numpy as jnp
import numpy as np
from jax import lax
from jax.experimental import pallas as pl
from jax.experimental.pallas import tpu as pltpu

HEAD_DIM = 64
N_HEADS = 4
MIX_W = N_HEADS * HEAD_DIM
IDX_HEADS = 8
IDX_DIM = 64
DSA_TOPK = 256
CMP_LEN = 32
CMP_STRIDE = 16
SEL_BLOCK = 64
SEL_SHIFT = 6
HEAD_SHIFT = 6
N_SEL = 16
WINDOW = 512
N_BUCKETS = 32
MAX_DIST = 128
CONV_W = 3
EPS = 1e-6
NEG = -1e30
FORCE = 1e4
QK_SCALE = HEAD_DIM ** -0.5

LANES = 128
SUBLANES = 8
VMEM_LIMIT = 56 * 1024 * 1024
INT_MIN = -2 ** 31

DENSE_TM = 512
TQ = 128
TKB = 128
PAGES_PER_STEP = 32

BF = jnp.bfloat16
F32 = jnp.float32

C_AQ, C_AKV, C_AQIDX, C_AMISC = 0, 256, 384, 896
C_BQ, C_BCMP, C_BSLC, C_BWIN, C_BGATE = 1024, 1280, 1408, 1536, 1664
C_CQ, C_CK, C_CV, C_CF = 1792, 2048, 2304, 2560
C_DQ, C_DKV = 2688, 2944
N_PROJ = 3456


def _cparams(sem):
    return pltpu.CompilerParams(dimension_semantics=sem, vmem_limit_bytes=VMEM_LIMIT)


def _dot(a, b):
    return jnp.dot(a, b, preferred_element_type=F32)


def _dot_nt(a, b):
    return lax.dot_general(a, b, (((1,), (1,)), ((), ())), preferred_element_type=F32)


def _split3(x):
    hi = x.astype(BF)
    r1 = x - hi.astype(F32)
    mid = r1.astype(BF)
    lo = (r1 - mid.astype(F32)).astype(BF)
    return hi, mid, lo


def _dot3(x, m):
    hi, mid, lo = _split3(x)
    return _dot(hi, m) + _dot(mid, m) + _dot(lo, m)


def _dot3_l(m, x):
    hi, mid, lo = _split3(x)
    return _dot(m, hi) + _dot(m, mid) + _dot(m, lo)


def _sortable(x):
    x = jnp.where(x == 0.0, 0.0, x)
    b = lax.bitcast_convert_type(x, jnp.int32)
    return b ^ ((b >> 31) & 0x7FFFFFFF)


def _log_sigmoid(v):
    return jnp.minimum(v, 0.0) - jnp.log1p(jnp.exp(-jnp.abs(v)))


def _rms_rows(x, g):
    ms = jnp.mean(x * x, axis=-1, keepdims=True)
    return (x * lax.rsqrt(ms + EPS)) * g


def _proj_kernel(x_ref, g_ref, w_ref, gain_ref, nmask_ref, bf_ref, bd_ref, tri_ref,
                 qa_o, akv_o, akv_b, qidx_o, amisc_o, kidx_b, qb_o, bcmp_o, bslc_o, bslc_b, bwin_o, bwin_b,
                 gates_o, qc_o, ckv_o, ckv_b, logf_o, call_o, qd_o, dkv_o, dkv_b, carry_scr,
                 *, tiles_per_seq, with_cumsum):
    h = _rms_rows(x_ref[...], g_ref[...]).astype(BF)

    def cols(c0, width):
        return _dot(h, w_ref[:, c0:c0 + width])

    def normed(c0, width):
        slab = cols(c0, width)
        ms = _dot((slab * slab).astype(BF), bd_ref[0:width, 0:width])
        scale = lax.rsqrt(ms + EPS) * gain_ref[:, c0:c0 + width]
        return slab * jnp.where(nmask_ref[:, c0:c0 + width] > 0.0, scale, 1.0)

    qa_o[...] = (normed(C_AQ, MIX_W) * QK_SCALE).astype(BF)
    akv = normed(C_AKV, 128)
    akv_o[...] = akv
    akv_b[...] = akv.astype(BF)
    qidx_o[...] = cols(C_AQIDX, IDX_HEADS * IDX_DIM).astype(BF)
    amisc = cols(C_AMISC, 128)
    amisc_o[...] = amisc
    kidx_b[...] = amisc[:, 0:IDX_DIM].astype(BF)
    qb_o[...] = (normed(C_BQ, MIX_W) * QK_SCALE).astype(BF)
    bcmp_o[...] = cols(C_BCMP, 128)
    bslc = normed(C_BSLC, 128)
    bslc_o[...] = bslc
    bslc_b[...] = bslc.astype(BF)
    bwin = normed(C_BWIN, 128)
    bwin_o[...] = bwin
    bwin_b[...] = bwin.astype(BF)
    gates_o[...] = jax.nn.sigmoid(cols(C_BGATE, 128))
    qc_o[...] = (normed(C_CQ, MIX_W) * QK_SCALE).astype(BF)
    ck = normed(C_CK, MIX_W)
    cv = cols(C_CV, MIX_W)
    ckv_o[:, 0:MIX_W] = ck
    ckv_o[:, MIX_W:2 * MIX_W] = cv
    ckv_b[:, 0:MIX_W] = ck.astype(BF)
    ckv_b[:, MIX_W:2 * MIX_W] = cv.astype(BF)
    logf = _log_sigmoid(cols(C_CF, 128) + bf_ref[...])
    logf_o[...] = logf
    if with_cumsum:
        t = pl.program_id(0)

        @pl.when(t % tiles_per_seq == 0)
        def _():
            carry_scr[...] = jnp.zeros_like(carry_scr)

        c = _dot3_l(tri_ref[...], logf) + carry_scr[0:1, :]
        call_o[...] = c
        carry_scr[...] = jnp.broadcast_to(c[-1:, :], carry_scr.shape)
    else:
        call_o[...] = logf
    qd_o[...] = (cols(C_DQ, MIX_W) * QK_SCALE).astype(BF)
    dkv = cols(C_DKV, 2 * MIX_W)
    dkv_o[...] = dkv
    dkv_b[...] = dkv.astype(BF)


def _proj(x2d, lw, *, tm, tiles_per_seq, with_cumsum):
    m, d = x2d.shape
    assert m % tm == 0
    row = lambda i: (i, 0)
    const = lambda i: (0, 0)

    def o(width, dtype):
        return jax.ShapeDtypeStruct((m, width), dtype), pl.BlockSpec((tm, width), row)

    outs = [o(MIX_W, BF), o(128, F32), o(128, BF), o(IDX_HEADS * IDX_DIM, BF), o(128, F32), o(IDX_DIM, BF),
            o(MIX_W, BF), o(128, F32), o(128, F32), o(128, BF), o(128, F32), o(128, BF),
            o(128, F32), o(MIX_W, BF), o(2 * MIX_W, F32), o(2 * MIX_W, BF), o(128, F32), o(128, F32),
            o(MIX_W, BF), o(2 * MIX_W, F32), o(2 * MIX_W, BF)]
    names = ('qa', 'akv', 'akv_b', 'qidx', 'amisc', 'kidx_b', 'qb', 'bcmp', 'bslc', 'bslc_b', 'bwin', 'bwin_b',
             'gates', 'qc', 'ckv', 'ckv_b', 'logf', 'call', 'qd', 'dkv', 'dkv_b')
    res = pl.pallas_call(
        functools.partial(_proj_kernel, tiles_per_seq=tiles_per_seq, with_cumsum=with_cumsum),
        grid=(m // tm,),
        in_specs=[pl.BlockSpec((tm, d), row), pl.BlockSpec((1, d), const), pl.BlockSpec((d, N_PROJ), const),
                  pl.BlockSpec((1, N_PROJ), const), pl.BlockSpec((1, N_PROJ), const), pl.BlockSpec((1, 128), const),
                  pl.BlockSpec((MIX_W, MIX_W), const), pl.BlockSpec((tm, tm), const)],
        out_specs=[s for _, s in outs],
        out_shape=[s for s, _ in outs],
        scratch_shapes=[pltpu.VMEM((SUBLANES, 128), F32)],
        compiler_params=_cparams(("arbitrary",)),
        name="proj",
    )(x2d, lw['g_attn'], lw['w_proj'], lw['gain'], lw['nmask'], lw['b_f'], lw['bd'], lw['tri'][:tm, :tm])
    return dict(zip(names, res))


def _merge_kernel(x_ref, g_ref, wg_ref, oa_ref, ob_ref, oc_ref, od_ref, wbr_ref, wo_ref, y_ref):
    x = x_ref[...]
    d = x.shape[1]
    h = _rms_rows(x, g_ref[...]).astype(BF)
    m = None
    for i, o_ref in enumerate((oa_ref, ob_ref, oc_ref, od_ref)):
        gate = jax.nn.sigmoid(_dot(h, wg_ref[:, i * d:(i + 1) * d]))
        term = gate * _dot(o_ref[...], wbr_ref[i])
        m = term if m is None else m + term
    y_ref[...] = x + _dot(m.astype(BF), wo_ref[...])


def _merge(x2d, o_a, o_b, o_c, o_d, lw, *, tm):
    m, d = x2d.shape
    row = lambda i: (i, 0)
    const = lambda i: (0, 0)
    return pl.pallas_call(
        _merge_kernel,
        grid=(m // tm,),
        in_specs=[pl.BlockSpec((tm, d), row), pl.BlockSpec((1, d), const), pl.BlockSpec((d, N_HEADS * d), const),
                  pl.BlockSpec((tm, MIX_W), row), pl.BlockSpec((tm, MIX_W), row), pl.BlockSpec((tm, MIX_W), row),
                  pl.BlockSpec((tm, MIX_W), row), pl.BlockSpec((4, MIX_W, d), lambda i: (0, 0, 0)),
                  pl.BlockSpec((d, d), const)],
        out_specs=pl.BlockSpec((tm, d), row),
        out_shape=jax.ShapeDtypeStruct((m, d), F32),
        compiler_params=_cparams(("arbitrary",)),
        name="merge",
    )(x2d, lw['g_attn'], lw['w_gate'], o_a, o_b, o_c, o_d, lw['w_br'], lw['w_o'])


def _ffn_kernel(x_ref, g_ref, wup_ref, cw_ref, cb_ref, wdn_ref, st0_ref, st1_ref, y_ref, conv_o, prev_scr,
                *, carry_mode, tiles_per_seq, d_ff, cw):
    x = x_ref[...]
    tm = x.shape[0]
    h = _rms_rows(x, g_ref[...]).astype(BF)
    row = lax.broadcasted_iota(jnp.int32, (tm, cw), 0)
    if carry_mode:
        t = pl.program_id(0)

        @pl.when(t % tiles_per_seq == 0)
        def _():
            prev_scr[...] = jnp.zeros_like(prev_scr)
    else:
        rowmod = row % SUBLANES

    def conv_cols(c0):
        up = _dot(h, wup_ref[:, c0:c0 + cw])
        r1 = pltpu.roll(up, 1, axis=0)
        r2 = pltpu.roll(up, 2, axis=0)
        if carry_mode:
            p6 = prev_scr[6:7, c0:c0 + cw]
            p7 = prev_scr[7:8, c0:c0 + cw]
            u1 = jnp.where(row == 0, p7, r1)
            u2 = jnp.where(row == 0, p6, jnp.where(row == 1, p7, r2))
            prev_scr[:, c0:c0 + cw] = up[tm - SUBLANES:tm, :]
            conv_o[:, c0:c0 + cw] = up[tm - SUBLANES:tm, :]
        else:
            s0 = st0_ref[:, c0:c0 + cw]
            s1 = st1_ref[:, c0:c0 + cw]
            u1 = jnp.where(rowmod == 0, s1, r1)
            u2 = jnp.where(rowmod == 0, s0, jnp.where(rowmod == 1, s1, r2))
            conv_o[:, c0:c0 + cw] = up
        conv = (u2 * cw_ref[0:1, c0:c0 + cw] + u1 * cw_ref[1:2, c0:c0 + cw]) + up * cw_ref[2:3, c0:c0 + cw]
        return cb_ref[:, c0:c0 + cw] + conv

    acc = None
    for c in range(d_ff // cw):
        val = conv_cols(c * cw)
        gate = conv_cols(d_ff + c * cw)
        act = (gate * jax.nn.sigmoid(gate)) * val
        part = _dot(act.astype(BF), wdn_ref[c * cw:(c + 1) * cw, :])
        acc = part if acc is None else acc + part
    y_ref[...] = x + acc


def _ffn(x2d, lw, st0, st1, *, tm, carry_mode, tiles_per_seq):
    m, d = x2d.shape
    d_ff = lw['w_down'].shape[0]
    cw = 256
    assert d_ff % cw == 0 and m % tm == 0
    row = lambda i: (i, 0)
    const = lambda i: (0, 0)
    if carry_mode:
        n_seq = m // (tm * tiles_per_seq)
        conv_shape = jax.ShapeDtypeStruct((n_seq * SUBLANES, 2 * d_ff), F32)
        conv_spec = pl.BlockSpec((SUBLANES, 2 * d_ff), lambda i: (i // tiles_per_seq, 0))
        st_spec = pl.BlockSpec((SUBLANES, 2 * d_ff), const)
    else:
        conv_shape = jax.ShapeDtypeStruct((m, 2 * d_ff), F32)
        conv_spec = pl.BlockSpec((tm, 2 * d_ff), row)
        st_spec = pl.BlockSpec((tm, 2 * d_ff), row)
    return pl.pallas_call(
        functools.partial(_ffn_kernel, carry_mode=carry_mode, tiles_per_seq=tiles_per_seq, d_ff=d_ff, cw=cw),
        grid=(m // tm,),
        in_specs=[pl.BlockSpec((tm, d), row), pl.BlockSpec((1, d), const), pl.BlockSpec((d, 2 * d_ff), const),
                  pl.BlockSpec((CONV_W, 2 * d_ff), const), pl.BlockSpec((1, 2 * d_ff), const),
                  pl.BlockSpec((d_ff, d), const), st_spec, st_spec],
        out_specs=[pl.BlockSpec((tm, d), row), conv_spec],
        out_shape=[jax.ShapeDtypeStruct((m, d), F32), conv_shape],
        scratch_shapes=[pltpu.VMEM((SUBLANES, 2 * d_ff), F32)],
        compiler_params=_cparams(("arbitrary",)),
        name="ffn",
    )(x2d, lw['g_ffn'], lw['w_up'], lw['conv_w'], lw['conv_b'], lw['w_down'], st0, st1)


def _osm_update(s, mask, m, l, acc, v):
    if mask is not None:
        s = jnp.where(mask, s, NEG)
    m_new = jnp.maximum(m, jnp.max(s, axis=1, keepdims=True))
    p = jnp.exp(s - m_new)
    if mask is not None:
        p = jnp.where(mask, p, 0.0)
    alpha = jnp.exp(m - m_new)
    l = alpha * l + jnp.sum(p, axis=1, keepdims=True)
    acc = alpha * acc + _dot(p.astype(BF), v)
    return m_new, l, acc


def _osm_ref_init(m_ref, l_ref, acc_ref):
    m_ref[...] = jnp.full(m_ref.shape, NEG, F32)
    l_ref[...] = jnp.zeros(l_ref.shape, F32)
    acc_ref[...] = jnp.zeros(acc_ref.shape, F32)


def _osm_ref_step(s, mask, v, m_ref, l_ref, acc_ref):
    reps = s.shape[1] // LANES
    if mask is not None:
        s = jnp.where(mask, s, NEG)
    m_prev = m_ref[...]
    m_new = jnp.maximum(m_prev, jnp.max(s, axis=1, keepdims=True))
    p = jnp.exp(s - jnp.tile(m_new, (1, reps)))
    if mask is not None:
        p = jnp.where(mask, p, 0.0)
    alpha = jnp.exp(m_prev - m_new)
    l_ref[...] = alpha * l_ref[...] + jnp.sum(p, axis=1, keepdims=True)
    d = acc_ref.shape[1]
    a = alpha[:, 0:d] if d <= LANES else jnp.tile(alpha, (1, d // LANES))
    acc_ref[...] = a * acc_ref[...] + _dot(p.astype(BF), v)
    m_ref[...] = m_new


def _key_chunk(t):
    return 4 * TKB if t % (4 * TKB) == 0 else TKB


def _tile_bias(bias_ref, i, first_tile, ntiles):
    parts = [bias_ref[jnp.clip(i - (first_tile + u), 0, 2)].reshape(N_HEADS * TQ, TKB) for u in range(ntiles)]
    return parts[0] if ntiles == 1 else jnp.concatenate(parts, axis=1)


def _pad_lanes(q):
    return jnp.concatenate([q, jnp.zeros_like(q)], axis=1)


def _heads_to_rows(q):
    return jnp.concatenate([q[:, h * HEAD_DIM:(h + 1) * HEAD_DIM] for h in range(N_HEADS)], axis=0)


def _rows_to_heads(o, r):
    return jnp.concatenate([o[h * r:(h + 1) * r, :] for h in range(N_HEADS)], axis=1)


def _tile_rows(a, n):
    return jnp.concatenate([a] * n, axis=0)


def _unsortable(key):
    return lax.bitcast_convert_type(key ^ ((key >> 31) & 0x7FFFFFFF), F32)


MAX_SEARCH_STEPS = 72


def _kth_search(count_ge, k, vmin, vmax, n_valid):
    lo0 = _sortable(vmin)
    hi0 = _sortable(vmax) + 1

    def done_of(lo, hi, c_lo):
        return (c_lo <= k) | (hi == lo + 1)

    def cond(st):
        it, lo, hi, c_lo, c_hi = st
        pending = jnp.max(jnp.where(done_of(lo, hi, c_lo), 0.0, 1.0)) > 0.0
        return (it < MAX_SEARCH_STEPS) & pending

    def body(st):
        it, lo, hi, c_lo, c_hi = st
        done = done_of(lo, hi, c_lo)
        frac = (c_lo - k).astype(F32) + 0.5
        frac = frac / jnp.maximum(c_lo - c_hi, 1).astype(F32)
        lo_v, hi_v = _unsortable(lo), _unsortable(hi - 1)
        interp = _sortable(lo_v + frac * (hi_v - lo_v))
        mid = (lo >> 1) + (hi >> 1) + (lo & hi & 1)
        cand = jnp.where(it % 2 == 0, interp, mid)
        cand = jnp.where(done, lo, jnp.clip(cand, lo + 1, hi - 1))
        c = count_ge(cand)
        up = (c >= k) & jnp.logical_not(done)
        down = (c < k) & jnp.logical_not(done)
        return (it + 1, jnp.where(up, cand, lo), jnp.where(down, cand, hi),
                jnp.where(up, c, c_lo), jnp.where(down, c, c_hi))

    st = (jnp.int32(0), lo0, hi0, n_valid, jnp.zeros_like(n_valid))
    _, lo, _, c_lo, c_hi = lax.while_loop(cond, body, st)
    return lo, c_lo, c_hi


def _dsa_prompt_kernel(qidx_ref, amisc_ref, qa_ref, kidx_ref, akv_ref, bias_ref, o_ref, key_scr, m_scr, l_scr, acc_scr,
                       *, topk, kt):
    i = pl.program_id(1)
    chunk = kt // TKB
    nchunk = i // chunk + 1
    amisc = amisc_ref[...]
    qidx = qidx_ref[...]
    qh = [qidx[:, h * IDX_DIM:(h + 1) * IDX_DIM] for h in range(IDX_HEADS)]
    wb = [jnp.broadcast_to(amisc[:, IDX_DIM + h:IDX_DIM + h + 1], (TQ, LANES)) for h in range(IDX_HEADS)]
    qpos = i * TQ + lax.broadcasted_iota(jnp.int32, (TQ, kt), 0)
    kcol = lax.broadcasted_iota(jnp.int32, (TQ, kt), 1)

    def score_chunk(c, carry):
        vmax, vmin = carry
        off = pl.multiple_of(c * kt, kt)
        kb = kidx_ref[pl.ds(off, kt), :]
        acc = None
        for h in range(IDX_HEADS):
            term = jnp.maximum(_dot_nt(qh[h], kb), 0.0) * jnp.tile(wb[h], (1, chunk))
            acc = term if acc is None else acc + term
        visible = off + kcol <= qpos
        key_scr[:, pl.ds(off, kt)] = jnp.where(visible, _sortable(acc), INT_MIN)
        hi_part = jnp.where(visible, acc, -jnp.inf)
        lo_part = jnp.where(visible, acc, jnp.inf)
        for u in range(chunk):
            vmax = jnp.maximum(vmax, hi_part[:, u * TKB:(u + 1) * TKB])
            vmin = jnp.minimum(vmin, lo_part[:, u * TKB:(u + 1) * TKB])
        return vmax, vmin

    vmax, vmin = lax.fori_loop(0, nchunk, score_chunk,
                               (jnp.full((TQ, TKB), -jnp.inf, F32), jnp.full((TQ, TKB), jnp.inf, F32)))
    vmax = jnp.max(vmax, axis=1, keepdims=True)
    vmin = jnp.min(vmin, axis=1, keepdims=True)
    n_valid = i * TQ + lax.broadcasted_iota(jnp.int32, (TQ, 1), 0) + 1

    cw = kt

    def count(pred):
        def body(c, acc):
            kc = key_scr[:, pl.ds(pl.multiple_of(c * cw, cw), cw)]
            hit = jnp.where(pred(kc, c * cw), 1.0, 0.0)
            part = hit[:, 0:TKB]
            for u in range(1, chunk):
                part = part + hit[:, u * TKB:(u + 1) * TKB]
            return acc + part
        acc = lax.fori_loop(0, nchunk, body, jnp.zeros((TQ, TKB), F32))
        return jnp.sum(acc, axis=1, keepdims=True).astype(jnp.int32)

    thr, c_lo, c_hi = _kth_search(lambda cand: count(lambda kc, o: kc >= cand), topk, vmin, vmax, n_valid)

    need = topk - c_hi
    tie = c_lo > topk
    any_tie = jnp.max(jnp.where(tie, 1.0, 0.0)) > 0.0
    lane_c = lax.broadcasted_iota(jnp.int32, (TQ, cw), 1)

    @pl.when(any_tie)
    def _():
        nbits = max(1, int(math.ceil(math.log2(key_scr.shape[1]))))

        def bit_body(b, lo):
            cand = lo + jnp.left_shift(jnp.int32(1), nbits - 1 - b)
            cnt = count(lambda kc, o: (kc == thr) & ((lane_c + o) < cand))
            return jnp.where(cnt < need, cand, lo)

        jmax = lax.fori_loop(0, nbits, bit_body, jnp.zeros((TQ, 1), jnp.int32))

        def demote(c, carry):
            sl = pl.ds(pl.multiple_of(c * cw, cw), cw)
            kc = key_scr[:, sl]
            drop = tie & (kc == thr) & ((lane_c + c * cw) > jmax)
            key_scr[:, sl] = jnp.where(drop, thr - 1, kc)
            return carry

        lax.fori_loop(0, nchunk, demote, 0)

    thr_eff = jnp.maximum(thr, INT_MIN + 1)
    q4 = _pad_lanes(_heads_to_rows(qa_ref[...]))
    _osm_ref_init(m_scr, l_scr, acc_scr)

    def att_chunk(c, carry):
        off = pl.multiple_of(c * kt, kt)
        kv = akv_ref[pl.ds(off, kt), :]
        sel = key_scr[:, pl.ds(off, kt)] >= thr_eff
        s = _dot_nt(q4, kv) + _tile_bias(bias_ref, i, c * chunk, chunk)
        _osm_ref_step(s, _tile_rows(sel, N_HEADS), kv, m_scr, l_scr, acc_scr)
        return carry

    lax.fori_loop(0, nchunk, att_chunk, 0)
    o4 = acc_scr[...] / l_scr[...]
    o_ref[...] = _rows_to_heads(o4[:, HEAD_DIM:2 * HEAD_DIM], TQ).astype(BF)


def _dsa_prompt(pr, bias, b, t):
    nq = t // TQ
    topk = min(DSA_TOPK, t // 4)
    rows = N_HEADS * TQ
    q_spec = lambda w: pl.BlockSpec((TQ, w), lambda bi, qi: (bi * nq + qi, 0))
    kv_spec = lambda w: pl.BlockSpec((t, w), lambda bi, qi: (bi, 0))
    return pl.pallas_call(
        functools.partial(_dsa_prompt_kernel, topk=topk, kt=_key_chunk(t)),
        grid=(b, nq),
        in_specs=[q_spec(IDX_HEADS * IDX_DIM), q_spec(128), q_spec(MIX_W), kv_spec(IDX_DIM), kv_spec(128),
                  pl.BlockSpec((3, N_HEADS, TQ, TKB), lambda bi, qi: (0, 0, 0, 0))],
        out_specs=q_spec(MIX_W),
        out_shape=jax.ShapeDtypeStruct((b * t, MIX_W), BF),
        scratch_shapes=[pltpu.VMEM((TQ, t), jnp.int32), pltpu.VMEM((rows, LANES), F32),
                        pltpu.VMEM((rows, LANES), F32), pltpu.VMEM((rows, LANES), F32)],
        compiler_params=_cparams(("arbitrary", "arbitrary")),
        name="dsa_prompt",
    )(pr['qidx'], pr['amisc'], pr['qa'], pr['kidx_b'], pr['akv_b'], bias)


def _compress_kernel(*refs, n_in):
    x_refs, (pe_ref, w_ref, o_ref) = refs[:n_in], refs[n_in:]
    x = x_refs[0][...] if n_in == 1 else jnp.concatenate([r[...] for r in x_refs], axis=0)
    o_ref[:, 0:128] = _dot((x + pe_ref[0:1, :]).astype(BF), w_ref[0])
    o_ref[:, 128:256] = _dot((x + pe_ref[1:2, :]).astype(BF), w_ref[1])


def _compress_weights(lw):
    w4 = lw['cmp_w'].reshape(2, CMP_LEN, HEAD_DIM, HEAD_DIM)
    pe = lw['cmp_pe']

    def half(rs):
        wk, wv = w4[0, rs], w4[1, rs]
        z = jnp.zeros_like(wk)
        rows = jnp.stack([jnp.concatenate([wk, z], axis=-1), jnp.concatenate([z, wv], axis=-1)], axis=1)
        return rows.reshape(CMP_STRIDE * 2 * HEAD_DIM, 2 * HEAD_DIM)

    top, bot = slice(0, CMP_STRIDE), slice(CMP_STRIDE, CMP_LEN)
    w = jnp.stack([half(top), half(bot)]).astype(BF)
    pef = jnp.stack([jnp.transpose(pe[:, top], (1, 0, 2)).reshape(-1), jnp.transpose(pe[:, bot], (1, 0, 2)).reshape(-1)])
    return pef, w


def _compress_dense(chunks, lw, rows_per_step):
    n, width = chunks.shape
    pef, w = _compress_weights(lw)
    return pl.pallas_call(
        functools.partial(_compress_kernel, n_in=1),
        grid=(n // rows_per_step,),
        in_specs=[pl.BlockSpec((rows_per_step, width), lambda i: (i, 0)), pl.BlockSpec((2, width), lambda i: (0, 0)),
                  pl.BlockSpec((2, width, 128), lambda i: (0, 0, 0))],
        out_specs=pl.BlockSpec((rows_per_step, 256), lambda i: (i, 0)),
        out_shape=jax.ShapeDtypeStruct((n, 256), F32),
        compiler_params=_cparams(("arbitrary",)),
        name="compress_prompt",
    )(chunks, pef, w)


def _combine_compressed(ab, gk):
    n = ab.shape[0]
    kv = ab[:, 0:128] + pltpu.roll(ab[:, 128:256], n - 1, axis=0)
    ck_raw = kv[:, 0:HEAD_DIM]
    ms = jnp.mean(ck_raw * ck_raw, axis=-1, keepdims=True)
    ck = (ck_raw * lax.rsqrt(ms + EPS)) * gk
    return ck.astype(BF), kv[:, HEAD_DIM:2 * HEAD_DIM].astype(BF)


def _masked_softmax_rows(s, valid):
    s = jnp.where(valid, s, NEG)
    m = jnp.max(s, axis=1, keepdims=True)
    p = jnp.where(valid, jnp.exp(s - m), 0.0)
    l = jnp.sum(p, axis=1, keepdims=True)
    return p * (1.0 / jnp.where(l > 0.0, l, 1.0))


def _gate_cols(g, c):
    return jnp.concatenate([g[:, h * 3 + c:h * 3 + c + 1] for h in range(N_HEADS)], axis=0)


def _nsa_prompt_kernel(qb_ref, gates_ref, ab_ref, gk_ref, cbias_ref, gmat_ref, emat_ref, tie_ref, slc_ref, win_ref, bias_ref,
                       o_ref, ck_scr, cv_scr, selm_scr, m_scr, l_scr, acc_scr, *, n_sel, ns, nch, kt, wt, nq):
    i = pl.program_id(1)

    @pl.when(i == 0)
    def _():
        ck, cv = _combine_compressed(ab_ref[...], gk_ref[...])
        ck_scr[...] = ck
        cv_scr[...] = cv

    q4 = _heads_to_rows(qb_ref[...])
    qpos = i * TQ + lax.broadcasted_iota(jnp.int32, (TQ, 1), 0)

    cend = lax.broadcasted_iota(jnp.int32, (TQ, nch), 1) * CMP_STRIDE + (CMP_LEN - 1)
    cvalid = cend <= qpos
    s = _dot_nt(q4, ck_scr[...]) + cbias_ref[...].reshape(N_HEADS * TQ, nch)
    pc = _masked_softmax_rows(s, _tile_rows(cvalid, N_HEADS))
    o_c = _dot(pc.astype(BF), cv_scr[...])
    pcs = pc[0:TQ] + pc[TQ:2 * TQ] + pc[2 * TQ:3 * TQ] + pc[3 * TQ:4 * TQ]
    imp = _dot3(pcs, gmat_ref[...])

    j_io = lax.broadcasted_iota(jnp.int32, (TQ, ns), 1)
    cur = qpos >> SEL_SHIFT
    gap = cur - j_io
    score = jnp.where(gap >= 0, jnp.where((j_io == 0) | (gap <= 1), FORCE, imp), NEG)
    rank = jnp.zeros((TQ, ns), F32)
    for jj in range(ns):
        col = score[:, jj:jj + 1]
        rank = rank + (jnp.where(col > score, 1.0, 0.0) + jnp.where(col == score, tie_ref[jj:jj + 1, :], 0.0))
    sel = jnp.where(gap >= 0, rank, float(ns)) < n_sel
    selm_scr[...] = _dot(jnp.where(sel, 1.0, 0.0).astype(BF), emat_ref[...])

    q4p = _pad_lanes(q4)
    chunk = kt // TKB
    qrow = i * TQ + lax.broadcasted_iota(jnp.int32, (TQ, kt), 0)
    kcol = lax.broadcasted_iota(jnp.int32, (TQ, kt), 1)
    _osm_ref_init(m_scr, l_scr, acc_scr)

    def sel_chunk(c, carry):
        off = pl.multiple_of(c * kt, kt)
        kv = slc_ref[pl.ds(off, kt), :]
        mask = (selm_scr[:, pl.ds(off, kt)] > 0.5) & (off + kcol <= qrow)
        s = _dot_nt(q4p, kv) + _tile_bias(bias_ref, i, c * chunk, chunk)
        _osm_ref_step(s, _tile_rows(mask, N_HEADS), kv, m_scr, l_scr, acc_scr)
        return carry

    lax.fori_loop(0, i // chunk + 1, sel_chunk, 0)
    o_s = (acc_scr[...] / l_scr[...])[:, HEAD_DIM:2 * HEAD_DIM]

    w0 = jnp.clip(i - (wt - 1), 0, nq - wt)
    woff = pl.multiple_of(w0 * TKB, TKB)
    kv = win_ref[pl.ds(woff, wt * TKB), :]
    wd = (i * TQ + lax.broadcasted_iota(jnp.int32, (TQ, wt * TKB), 0)) - (
        woff + lax.broadcasted_iota(jnp.int32, (TQ, wt * TKB), 1))
    s = _dot_nt(q4p, kv) + _tile_bias(bias_ref, i, w0, wt)
    pw = _masked_softmax_rows(s, _tile_rows((wd >= 0) & (wd <= WINDOW), N_HEADS))
    o_w = _dot(pw.astype(BF), kv)[:, HEAD_DIM:2 * HEAD_DIM]

    g = gates_ref[...]
    o = _gate_cols(g, 0) * o_c + _gate_cols(g, 1) * o_s + _gate_cols(g, 2) * o_w
    o_ref[...] = _rows_to_heads(o, TQ).astype(BF)


def _nsa_prompt(pr, ab, lw, tabs, b, t):
    nq = t // TQ
    nch = t // CMP_STRIDE
    ns = t // SEL_BLOCK
    n_sel = min(N_SEL, ns)
    q_spec = lambda w: pl.BlockSpec((TQ, w), lambda bi, qi: (bi * nq + qi, 0))
    kv_spec = lambda w: pl.BlockSpec((t, w), lambda bi, qi: (bi, 0))
    const2 = lambda bi, qi: (0, 0)
    return pl.pallas_call(
        functools.partial(_nsa_prompt_kernel, n_sel=n_sel, ns=ns, nch=nch, kt=_key_chunk(t),
                          wt=min(WINDOW // TKB + 1, nq), nq=nq),
        grid=(b, nq),
        in_specs=[q_spec(MIX_W), q_spec(128), pl.BlockSpec((nch, 256), lambda bi, qi: (bi, 0)),
                  pl.BlockSpec((1, HEAD_DIM), const2),
                  pl.BlockSpec((None, N_HEADS, TQ, nch), lambda bi, qi: (qi, 0, 0, 0)),
                  pl.BlockSpec((nch, ns), const2), pl.BlockSpec((ns, t), const2), pl.BlockSpec((ns, ns), const2),
                  kv_spec(128), kv_spec(128),
                  pl.BlockSpec((3, N_HEADS, TQ, TKB), lambda bi, qi: (0, 0, 0, 0))],
        out_specs=q_spec(MIX_W),
        out_shape=jax.ShapeDtypeStruct((b * t, MIX_W), BF),
        scratch_shapes=[pltpu.VMEM((nch, HEAD_DIM), BF), pltpu.VMEM((nch, HEAD_DIM), BF), pltpu.VMEM((TQ, t), F32),
                        pltpu.VMEM((N_HEADS * TQ, LANES), F32), pltpu.VMEM((N_HEADS * TQ, LANES), F32),
                        pltpu.VMEM((N_HEADS * TQ, LANES), F32)],
        compiler_params=_cparams(("arbitrary", "arbitrary")),
        name="nsa_prompt",
    )(pr['qb'], pr['gates'], ab, lw['gk_b'], tabs['cbias_p'], tabs['gmat_p'], tabs['emat_p'], tabs['tie_p'],
      pr['bslc_b'], pr['bwin_b'], tabs['nsa_tiles'])


def _fox_prompt_kernel(q_ref, cq_ref, ckv_ref, ckt_ref, o_ref, m_scr, l_scr, acc_scr, *, kt):
    i = pl.program_id(1)
    chunk = kt // TKB
    nfull = i // chunk
    q4 = _head_block_diag(q_ref[...])
    cq = cq_ref[...]
    cq4 = jnp.concatenate([jnp.broadcast_to(cq[:, h:h + 1], (TQ, LANES)) for h in range(N_HEADS)], axis=0)
    qpos = i * TQ + lax.broadcasted_iota(jnp.int32, (TQ, kt), 0)
    kcol = lax.broadcasted_iota(jnp.int32, (TQ, kt), 1)
    _osm_ref_init(m_scr, l_scr, acc_scr)

    def do_chunk(c, masked):
        off = pl.multiple_of(c * kt, kt)
        kv = ckv_ref[pl.ds(off, kt), :]
        ck4 = jnp.concatenate([jnp.broadcast_to(ckt_ref[h:h + 1, pl.ds(off, kt)], (TQ, kt)) for h in range(N_HEADS)],
                              axis=0)
        s = (_dot_nt(q4, kv[:, 0:MIX_W]) + jnp.tile(cq4, (1, chunk))) - ck4
        mask = _tile_rows(off + kcol <= qpos, N_HEADS) if masked else None
        _osm_ref_step(s, mask, kv[:, MIX_W:2 * MIX_W], m_scr, l_scr, acc_scr)

    def body(c, carry):
        do_chunk(c, False)
        return carry

    lax.fori_loop(0, nfull, body, 0)
    do_chunk(nfull, True)
    o4 = acc_scr[...] / jnp.tile(l_scr[...], (1, MIX_W // LANES))
    o_ref[...] = _head_diag_pick(o4, TQ).astype(BF)


def _fox_prompt(pr, ckt, b, t):
    nq = t // TQ
    rows = N_HEADS * TQ
    q_spec = lambda w: pl.BlockSpec((TQ, w), lambda bi, qi: (bi * nq + qi, 0))
    return pl.pallas_call(
        functools.partial(_fox_prompt_kernel, kt=_key_chunk(t)),
        grid=(b, nq),
        in_specs=[q_spec(MIX_W), q_spec(128), pl.BlockSpec((t, 2 * MIX_W), lambda bi, qi: (bi, 0)),
                  pl.BlockSpec((SUBLANES, t), lambda bi, qi: (bi, 0))],
        out_specs=q_spec(MIX_W),
        out_shape=jax.ShapeDtypeStruct((b * t, MIX_W), BF),
        scratch_shapes=[pltpu.VMEM((rows, LANES), F32), pltpu.VMEM((rows, LANES), F32), pltpu.VMEM((rows, MIX_W), F32)],
        compiler_params=_cparams(("arbitrary", "arbitrary")),
        name="fox_prompt",
    )(pr['qc'], pr['call'], pr['ckv_b'], ckt)


def _stick_terms(z):
    e = jnp.log(1.0 + jnp.exp(-jnp.abs(z)))
    return -(jnp.maximum(z, 0.0) + e), jnp.minimum(z, 0.0) - e


def _suffix_sums(x, uaug, ntiles):
    r = x.shape[0]
    stack = x if ntiles == 1 else jnp.concatenate([x[:, u * TKB:(u + 1) * TKB] for u in range(ntiles)], axis=0)
    hi = stack.astype(BF)
    lo = (stack - hi.astype(F32)).astype(BF)
    rs = _dot(hi, uaug) + _dot(lo, uaug)
    return [(rs[u * r:(u + 1) * r, 0:TKB], rs[u * r:(u + 1) * r, TKB:2 * TKB]) for u in range(ntiles)]


def _stick_prompt_kernel(q_ref, dkv_ref, u_ref, o_ref, acc_scr, run_scr, *, kt):
    i = pl.program_id(1)
    chunk = kt // TKB
    last = i // chunk
    q4 = _head_block_diag(q_ref[...])
    uaug = u_ref[...]
    qpos = i * TQ + lax.broadcasted_iota(jnp.int32, (TQ, kt), 0)
    kcol = lax.broadcasted_iota(jnp.int32, (TQ, kt), 1)
    acc_scr[...] = jnp.zeros(acc_scr.shape, F32)
    run_scr[...] = jnp.zeros(run_scr.shape, F32)

    def do_chunk(c, masked):
        off = pl.multiple_of(c * kt, kt)
        kv = dkv_ref[pl.ds(off, kt), :]
        nsp, lsig = _stick_terms(_dot_nt(q4, kv[:, 0:MIX_W]))
        if masked:
            ok = _tile_rows(off + kcol < qpos, N_HEADS)
            nsp = jnp.where(ok, nsp, 0.0)
        sums = _suffix_sums(nsp, uaug, chunk)
        run = run_scr[...]
        pieces = [None] * chunk
        for u in reversed(range(chunk)):
            pieces[u] = sums[u][0] + run
            run = run + sums[u][1]
        run_scr[...] = run
        a = jnp.exp(lsig + (pieces[0] if chunk == 1 else jnp.concatenate(pieces, axis=1)))
        if masked:
            a = jnp.where(ok, a, 0.0)
        acc_scr[...] = acc_scr[...] + _dot(a.astype(BF), kv[:, MIX_W:2 * MIX_W])

    def body(jj, carry):
        @pl.when(jj == 0)
        def _():
            do_chunk(last, True)

        @pl.when(jj > 0)
        def _():
            do_chunk(last - jj, False)

        return carry

    lax.fori_loop(0, last + 1, body, 0)
    o_ref[...] = _head_diag_pick(acc_scr[...], TQ).astype(BF)


def _stick_prompt(pr, uaug, b, t):
    nq = t // TQ
    rows = N_HEADS * TQ
    q_spec = lambda w: pl.BlockSpec((TQ, w), lambda bi, qi: (bi * nq + qi, 0))
    return pl.pallas_call(
        functools.partial(_stick_prompt_kernel, kt=_key_chunk(t)),
        grid=(b, nq),
        in_specs=[q_spec(MIX_W), pl.BlockSpec((t, 2 * MIX_W), lambda bi, qi: (bi, 0)),
                  pl.BlockSpec((TKB, 2 * TKB), lambda bi, qi: (0, 0))],
        out_specs=q_spec(MIX_W),
        out_shape=jax.ShapeDtypeStruct((b * t, MIX_W), BF),
        scratch_shapes=[pltpu.VMEM((rows, MIX_W), F32), pltpu.VMEM((rows, LANES), F32)],
        compiler_params=_cparams(("arbitrary", "arbitrary")),
        name="stick_prompt",
    )(pr['qd'], pr['dkv_b'], uaug)


PAGE = 128
TD = SUBLANES


def _page_specs(page_shape, l, pg, page_of):
    tail = (0,) * len(page_shape)

    def spec(i):
        def index_map(*args):
            pt = args[-1]
            return (l, pt[args[0], page_of(*args[1:-1], i)]) + tail
        return pl.BlockSpec((None, None) + page_shape, index_map)

    return [spec(i) for i in range(pg)]


def _pad_rows(a, n):
    return jnp.concatenate([a, jnp.zeros((n - a.shape[0], a.shape[1]), a.dtype)], axis=0)


def _topk_select_ref(key_ref, k):
    rows, width = key_ref.shape

    def count(pred):
        return jnp.sum(jnp.where(pred(key_ref[...]), 1.0, 0.0), axis=1, keepdims=True).astype(jnp.int32)

    keys = key_ref[...]
    valid = keys > INT_MIN
    vals = _unsortable(keys)
    vmax = jnp.max(jnp.where(valid, vals, -jnp.inf), axis=1, keepdims=True)
    vmin = jnp.min(jnp.where(valid, vals, jnp.inf), axis=1, keepdims=True)
    thr, c_lo, c_hi = _kth_search(lambda cand: count(lambda kk: kk >= cand), k, vmin, vmax,
                                  count(lambda kk: kk > INT_MIN))
    need = k - c_hi
    tie = c_lo > k
    any_tie = jnp.max(jnp.where(tie, 1.0, 0.0)) > 0.0

    @pl.when(any_tie)
    def _():
        idx = lax.broadcasted_iota(jnp.int32, (rows, width), 1)
        nbits = max(1, int(math.ceil(math.log2(width))))

        def bit_body(b, lo):
            cand = lo + jnp.left_shift(jnp.int32(1), nbits - 1 - b)
            cnt = count(lambda kk: (kk == thr) & (idx < cand))
            return jnp.where(cnt < need, cand, lo)

        jmax = lax.fori_loop(0, nbits, bit_body, jnp.zeros((rows, 1), jnp.int32))
        kk = key_ref[...]
        key_ref[...] = jnp.where(tie & (kk == thr) & (idx > jmax), thr - 1, kk)

    return jnp.maximum(thr, INT_MIN + 1)


def _osm_scratch_update(s, mask, v, m_scr, l_scr, acc_scr):
    m, l, acc = _osm_update(s, mask, m_scr[...], l_scr[...], acc_scr[...], v)
    m_scr[...] = m
    l_scr[...] = l
    acc_scr[...] = acc


def _osm_scratch_init(m_scr, l_scr, acc_scr):
    m_scr[...] = jnp.full(m_scr.shape, NEG, F32)
    l_scr[...] = jnp.zeros(l_scr.shape, F32)
    acc_scr[...] = jnp.zeros(acc_scr.shape, F32)


def _head_block_diag(q):
    lane = lax.broadcasted_iota(jnp.int32, q.shape, 1)
    return jnp.concatenate([jnp.where(lane >> HEAD_SHIFT == h, q, jnp.zeros_like(q)) for h in range(N_HEADS)], axis=0)


def _head_diag_pick(o, r):
    lane = lax.broadcasted_iota(jnp.int32, (r, o.shape[1]), 1)
    out = None
    for h in range(N_HEADS):
        part = jnp.where(lane >> HEAD_SHIFT == h, o[h * r:(h + 1) * r, :], 0.0)
        out = part if out is None else out + part
    return out


def _causal_new_mask(strict):
    lane = lax.broadcasted_iota(jnp.int32, (TD, PAGE), 1)
    rowi = lax.broadcasted_iota(jnp.int32, (TD, PAGE), 0)
    return lane < rowi if strict else lane <= rowi


def _compress_paged(cache, page_table, l, lw, nstep, pg):
    s = page_table.shape[0]
    pef, w = _compress_weights(lw)
    width = pef.shape[1]
    rows = PAGE // CMP_STRIDE

    def body(pt_ref, *refs):
        _compress_kernel(*refs, n_in=pg)

    return pl.pallas_call(
        body,
        grid_spec=pltpu.PrefetchScalarGridSpec(
            num_scalar_prefetch=1, grid=(s, nstep),
            in_specs=_page_specs((rows, width), l, pg, lambda p, i: p * pg + i)
            + [pl.BlockSpec((2, width), lambda si, p, pt: (0, 0)), pl.BlockSpec((2, width, 128), lambda si, p, pt: (0, 0, 0))],
            out_specs=pl.BlockSpec((None, pg * rows, 256), lambda si, p, pt: (si, p, 0))),
        out_shape=jax.ShapeDtypeStruct((s, nstep * pg * rows, 256), F32),
        compiler_params=_cparams(("arbitrary", "arbitrary")),
        name="compress_decode",
    )(page_table, *([cache] * pg), pef, w)


def _dsa_decode_kernel(pt_ref, qidx_ref, amisc_ref, qa_ref, kidxn_ref, akvn_ref, blast_ref, bfar_ref, bnew_ref, *rest,
                       topk, nstep, pg):
    kid_refs, kv_refs = rest[:pg], rest[pg:2 * pg]
    o_ref, key_scr, thr_scr, m_scr, l_scr, acc_scr = rest[2 * pg:]
    ph, p = pl.program_id(1), pl.program_id(2)
    wstep = pg * PAGE
    npast = nstep * wstep
    last = p == nstep - 1

    @pl.when(ph == 0)
    def _():
        qidx = qidx_ref[...]
        w = amisc_ref[:, IDX_DIM:IDX_DIM + IDX_HEADS]
        q64 = jnp.concatenate([qidx[:, h * IDX_DIM:(h + 1) * IDX_DIM] for h in range(IDX_HEADS)], axis=0)

        def scores(keys):
            sc = _dot_nt(q64, keys)
            acc = None
            for h in range(IDX_HEADS):
                term = jnp.maximum(sc[h * TD:(h + 1) * TD, :], 0.0) * w[:, h:h + 1]
                acc = term if acc is None else acc + term
            return acc

        kid = jnp.concatenate([r[...].astype(BF) for r in kid_refs], axis=0)
        key_scr[:, pl.ds(pl.multiple_of(p * wstep, wstep), wstep)] = _sortable(scores(kid))

        @pl.when(last)
        def _():
            acc = scores(_pad_rows(kidxn_ref[...], PAGE))
            key_scr[:, npast:npast + PAGE] = jnp.where(_causal_new_mask(False), _sortable(acc), INT_MIN)
            thr_scr[...] = jnp.broadcast_to(_topk_select_ref(key_scr, topk), thr_scr.shape)

    @pl.when(ph == 1)
    def _():
        @pl.when(p == 0)
        def _():
            _osm_scratch_init(m_scr, l_scr, acc_scr)

        thr = thr_scr[:, 0:1]
        q4 = _pad_lanes(_heads_to_rows(qa_ref[...]))
        kv = jnp.concatenate([r[...].astype(BF) for r in kv_refs], axis=0)
        bias = jnp.where(last, blast_ref[...], bfar_ref[...])
        s = _dot_nt(q4, kv) + bias
        sel = key_scr[:, pl.ds(pl.multiple_of(p * wstep, wstep), wstep)] >= thr
        _osm_scratch_update(s, _tile_rows(sel, N_HEADS), kv, m_scr, l_scr, acc_scr)

        @pl.when(last)
        def _():
            kvn = _pad_rows(akvn_ref[...], PAGE)
            s = _dot_nt(q4, kvn) + bnew_ref[...]
            sel = key_scr[:, npast:npast + PAGE] >= thr
            m, l, acc = _osm_update(s, _tile_rows(sel, N_HEADS), m_scr[...], l_scr[...], acc_scr[...], kvn)
            o_ref[...] = _rows_to_heads((acc / l)[:, HEAD_DIM:2 * HEAD_DIM], TD).astype(BF)


def _seq_spec(width, ngrid):
    if ngrid == 2:
        return pl.BlockSpec((None, TD, width), lambda si, p, pt: (si, 0, 0))
    return pl.BlockSpec((None, TD, width), lambda si, ph, p, pt: (si, 0, 0))


def _const_spec(shape, ngrid):
    zeros = (0,) * len(shape)
    if ngrid == 2:
        return pl.BlockSpec(shape, lambda si, p, pt: zeros)
    return pl.BlockSpec(shape, lambda si, ph, p, pt: zeros)


def _dsa_decode(prs, cache_kidx, cache_akv, page_table, l, tabs, nstep, pg):
    s = page_table.shape[0]
    npast = nstep * pg * PAGE
    topk = min(DSA_TOPK, (npast + TD) // 4)
    wstep = pg * PAGE
    kid_specs = _page_specs((PAGE, IDX_DIM), l, pg, lambda ph, p, i: jnp.where(ph == 0, p, nstep - 1) * pg + i)
    kv_specs = _page_specs((PAGE, 128), l, pg, lambda ph, p, i: jnp.where(ph == 0, 0, p) * pg + i)
    return pl.pallas_call(
        functools.partial(_dsa_decode_kernel, topk=topk, nstep=nstep, pg=pg),
        grid_spec=pltpu.PrefetchScalarGridSpec(
            num_scalar_prefetch=1, grid=(s, 2, nstep),
            in_specs=[_seq_spec(IDX_HEADS * IDX_DIM, 3), _seq_spec(128, 3), _seq_spec(MIX_W, 3), _seq_spec(IDX_DIM, 3),
                      _seq_spec(128, 3), _const_spec((N_HEADS * TD, wstep), 3), _const_spec((N_HEADS * TD, 1), 3),
                      _const_spec((N_HEADS * TD, PAGE), 3)] + kid_specs + kv_specs,
            out_specs=_seq_spec(MIX_W, 3),
            scratch_shapes=[pltpu.VMEM((TD, npast + PAGE), jnp.int32), pltpu.VMEM((TD, 128), jnp.int32),
                            pltpu.VMEM((N_HEADS * TD, 1), F32), pltpu.VMEM((N_HEADS * TD, 1), F32),
                            pltpu.VMEM((N_HEADS * TD, 2 * HEAD_DIM), F32)]),
        out_shape=jax.ShapeDtypeStruct((s, TD, MIX_W), BF),
        compiler_params=_cparams(("arbitrary", "arbitrary", "arbitrary")),
        name="dsa_decode",
    )(page_table, prs['qidx'], prs['amisc'], prs['qa'], prs['kidx_b'], prs['akv_b'],
      tabs['a_last'], tabs['a_far'], tabs['a_new'], *([cache_kidx] * pg), *([cache_akv] * pg))


def _nsa_decode_kernel(pt_ref, qb_ref, gates_ref, ab_ref, gk_ref, cbias_ref, gmat_ref, slcn_ref, winp_ref, winn_ref,
                       blast_ref, bfar_ref, bnew_ref, wbp_ref, *rest, n_sel, nstep, pg):
    slc_refs = rest[:pg]
    o_ref, selm_scr, bkey_scr, oc_scr, m_scr, l_scr, acc_scr = rest[pg:]
    p = pl.program_id(1)
    wstep = pg * PAGE
    npast = nstep * wstep
    nch = npast // CMP_STRIDE
    nsb = npast // SEL_BLOCK
    last = p == nstep - 1
    q4 = _heads_to_rows(qb_ref[...])

    @pl.when(p == 0)
    def _():
        ck, cv = _combine_compressed(ab_ref[...], gk_ref[...])
        qpos = npast + lax.broadcasted_iota(jnp.int32, (TD, 1), 0)
        cend = lax.broadcasted_iota(jnp.int32, (TD, nch), 1) * CMP_STRIDE + (CMP_LEN - 1)
        s = _dot_nt(q4, ck) + cbias_ref[...]
        pc = _masked_softmax_rows(s, _tile_rows(cend <= qpos, N_HEADS))
        oc_scr[...] = _dot(pc.astype(BF), cv)
        pcs = pc[0:TD] + pc[TD:2 * TD] + pc[2 * TD:3 * TD] + pc[3 * TD:4 * TD]
        imp = _dot3(pcs, gmat_ref[...])
        j_io = lax.broadcasted_iota(jnp.int32, (TD, nsb), 1)
        forced = (j_io == 0) | (j_io == nsb - 1)
        bkey_scr[...] = _sortable(jnp.where(forced, FORCE, imp))
        thr = _topk_select_ref(bkey_scr, n_sel - 1)
        selb = jnp.where(bkey_scr[...] >= thr, 1.0, 0.0).astype(BF)
        blk = lax.broadcasted_iota(jnp.int32, (nsb, wstep), 0)
        col = lax.broadcasted_iota(jnp.int32, (nsb, wstep), 1)
        for c in range(nstep):
            expand = jnp.where(blk == (c * wstep + col) >> SEL_SHIFT, 1.0, 0.0).astype(BF)
            selm_scr[:, c * wstep:(c + 1) * wstep] = _dot(selb, expand)
        _osm_scratch_init(m_scr, l_scr, acc_scr)

    q4p = _pad_lanes(q4)
    kv = jnp.concatenate([r[...].astype(BF) for r in slc_refs], axis=0)
    bias = jnp.where(last, blast_ref[...], bfar_ref[...])
    s = _dot_nt(q4p, kv) + bias
    mask = selm_scr[:, pl.ds(pl.multiple_of(p * wstep, wstep), wstep)] > 0.5
    _osm_scratch_update(s, _tile_rows(mask, N_HEADS), kv, m_scr, l_scr, acc_scr)

    @pl.when(last)
    def _():
        hi = lambda a: a[:, HEAD_DIM:2 * HEAD_DIM]
        new_mask = _tile_rows(_causal_new_mask(False), N_HEADS)
        kvn = _pad_rows(slcn_ref[...], PAGE)
        s = _dot_nt(q4p, kvn) + bnew_ref[...]
        _, l_s, acc_s = _osm_update(s, new_mask, m_scr[...], l_scr[...], acc_scr[...], kvn)
        wp = winp_ref[...].astype(BF)
        nw = wp.shape[0]
        wd = nw + lax.broadcasted_iota(jnp.int32, (TD, nw), 0) - lax.broadcasted_iota(jnp.int32, (TD, nw), 1)
        s = _dot_nt(q4p, wp) + wbp_ref[...]
        carry = _osm_update(s, _tile_rows(wd <= WINDOW, N_HEADS), jnp.full((N_HEADS * TD, 1), NEG, F32),
                            jnp.zeros((N_HEADS * TD, 1), F32), jnp.zeros((N_HEADS * TD, 2 * HEAD_DIM), F32), wp)
        kvw = _pad_rows(winn_ref[...], PAGE)
        s = _dot_nt(q4p, kvw) + bnew_ref[...]
        _, l_w, acc_w = _osm_update(s, new_mask, *carry, kvw)
        g = gates_ref[...]
        o = (_gate_cols(g, 0) * oc_scr[...] + _gate_cols(g, 1) * hi(acc_s / l_s) + _gate_cols(g, 2) * hi(acc_w / l_w))
        o_ref[...] = _rows_to_heads(o, TD).astype(BF)


def _nsa_decode(prs, ab, win_past, cache_slc, page_table, l, lw, tabs, nstep, pg):
    s = page_table.shape[0]
    wstep = pg * PAGE
    npast = nstep * wstep
    nch, nsb = npast // CMP_STRIDE, npast // SEL_BLOCK
    n_sel = min(N_SEL, nsb + 1)
    assert n_sel >= 2 and win_past.shape[1] == WINDOW
    return pl.pallas_call(
        functools.partial(_nsa_decode_kernel, n_sel=n_sel, nstep=nstep, pg=pg),
        grid_spec=pltpu.PrefetchScalarGridSpec(
            num_scalar_prefetch=1, grid=(s, nstep),
            in_specs=[_seq_spec(MIX_W, 2), _seq_spec(128, 2),
                      pl.BlockSpec((None, nch, 256), lambda si, p, pt: (si, 0, 0)), _const_spec((1, HEAD_DIM), 2),
                      _const_spec((N_HEADS * TD, nch), 2), _const_spec((nch, nsb), 2), _seq_spec(128, 2),
                      pl.BlockSpec((None, WINDOW, 128), lambda si, p, pt: (si, 0, 0)), _seq_spec(128, 2),
                      _const_spec((N_HEADS * TD, wstep), 2), _const_spec((N_HEADS * TD, 1), 2),
                      _const_spec((N_HEADS * TD, PAGE), 2), _const_spec((N_HEADS * TD, WINDOW), 2)]
            + _page_specs((PAGE, 128), l, pg, lambda p, i: p * pg + i),
            out_specs=_seq_spec(MIX_W, 2),
            scratch_shapes=[pltpu.VMEM((TD, npast), F32), pltpu.VMEM((TD, nsb), jnp.int32),
                            pltpu.VMEM((N_HEADS * TD, HEAD_DIM), F32), pltpu.VMEM((N_HEADS * TD, 1), F32),
                            pltpu.VMEM((N_HEADS * TD, 1), F32), pltpu.VMEM((N_HEADS * TD, 2 * HEAD_DIM), F32)]),
        out_shape=jax.ShapeDtypeStruct((s, TD, MIX_W), BF),
        compiler_params=_cparams(("arbitrary", "arbitrary")),
        name="nsa_decode",
    )(page_table, prs['qb'], prs['gates'], ab, lw['gk_b'], tabs['cbias_d'], tabs['gmat_d'], prs['bslc_b'], win_past,
      prs['bwin_b'], tabs['b_last'], tabs['b_far'], tabs['b_new'], tabs['wb_past'], *([cache_slc] * pg))


def _fox_decode_kernel(pt_ref, q_ref, lfn_ref, lftn_ref, ckvn_ref, u_ref, tinc_ref, *rest, nstep, pg):
    kv_refs, lf_refs = rest[:pg], rest[pg:2 * pg]
    o_ref, qbd_scr, cq_scr, m_scr, l_scr, acc_scr, carry_scr = rest[2 * pg:]
    p = pl.program_id(1)

    @pl.when(p == 0)
    def _():
        qbd = _head_block_diag(q_ref[...])
        qbd_scr[...] = qbd
        lf = lfn_ref[...]
        rows = [lf[0:1, :]]
        for r in range(1, TD):
            rows.append(rows[-1] + lf[r:r + 1, :])
        npf = jnp.concatenate(rows, axis=0)
        cq4 = jnp.concatenate([npf[:, h:h + 1] for h in range(N_HEADS)], axis=0)
        cq_scr[...] = cq4
        npt = _dot3(lftn_ref[...], tinc_ref[...])
        ck4 = jnp.concatenate([jnp.broadcast_to(npt[h:h + 1, :], (TD, PAGE)) for h in range(N_HEADS)], axis=0)
        kvn = _pad_rows(ckvn_ref[...], PAGE)
        s = (_dot_nt(qbd, kvn[:, 0:MIX_W]) + cq4) - ck4
        _osm_scratch_init(m_scr, l_scr, acc_scr)
        _osm_scratch_update(s, _tile_rows(_causal_new_mask(False), N_HEADS), kvn[:, MIX_W:2 * MIX_W], m_scr, l_scr, acc_scr)
        carry_scr[...] = jnp.zeros_like(carry_scr)

    x = jnp.concatenate([r[...] for r in lf_refs], axis=0)
    rs = _dot3(x, u_ref[...])
    tot = jnp.sum(x, axis=1, keepdims=True)
    off = carry_scr[:, 0:1]
    pieces = [None] * pg
    for i in reversed(range(pg)):
        s_i = rs[i * SUBLANES:(i + 1) * SUBLANES, :] + off
        pieces[i] = jnp.concatenate([jnp.broadcast_to(s_i[h:h + 1, :], (TD, PAGE)) for h in range(N_HEADS)], axis=0)
        off = off + tot[i * SUBLANES:(i + 1) * SUBLANES, :]
    carry_scr[...] = jnp.broadcast_to(off, carry_scr.shape)
    k = jnp.concatenate([r[:, 0:MIX_W].astype(BF) for r in kv_refs], axis=0)
    v = jnp.concatenate([r[:, MIX_W:2 * MIX_W].astype(BF) for r in kv_refs], axis=0)
    s = (_dot_nt(qbd_scr[...], k) + cq_scr[...]) + jnp.concatenate(pieces, axis=1)
    _osm_scratch_update(s, None, v, m_scr, l_scr, acc_scr)

    @pl.when(p == nstep - 1)
    def _():
        o_ref[...] = _head_diag_pick(acc_scr[...] / l_scr[...], TD).astype(BF)


def _fox_decode(prs, lft_new, cache_ckv, cache_lft, page_table, l, tabs, nstep, pg):
    s = page_table.shape[0]
    rev = lambda p, i: (nstep - 1 - p) * pg + i
    return pl.pallas_call(
        functools.partial(_fox_decode_kernel, nstep=nstep, pg=pg),
        grid_spec=pltpu.PrefetchScalarGridSpec(
            num_scalar_prefetch=1, grid=(s, nstep),
            in_specs=[_seq_spec(MIX_W, 2), _seq_spec(128, 2), _seq_spec(128, 2), _seq_spec(2 * MIX_W, 2),
                      _const_spec((PAGE, PAGE), 2), _const_spec((PAGE, PAGE), 2)]
            + _page_specs((PAGE, 2 * MIX_W), l, pg, rev) + _page_specs((SUBLANES, PAGE), l, pg, rev),
            out_specs=_seq_spec(MIX_W, 2),
            scratch_shapes=[pltpu.VMEM((N_HEADS * TD, MIX_W), BF), pltpu.VMEM((N_HEADS * TD, 1), F32),
                            pltpu.VMEM((N_HEADS * TD, 1), F32), pltpu.VMEM((N_HEADS * TD, 1), F32),
                            pltpu.VMEM((N_HEADS * TD, MIX_W), F32), pltpu.VMEM((SUBLANES, 128), F32)]),
        out_shape=jax.ShapeDtypeStruct((s, TD, MIX_W), BF),
        compiler_params=_cparams(("arbitrary", "arbitrary")),
        name="fox_decode",
    )(page_table, prs['qc'], prs['logf'], lft_new, prs['ckv_b'], tabs['umat'], tabs['tinc'],
      *([cache_ckv] * pg), *([cache_lft] * pg))


def _stick_decode_kernel(pt_ref, q_ref, dkvn_ref, u_ref, *rest, nstep, pg):
    kv_refs = rest[:pg]
    o_ref, qbd_scr, acc_scr, run_scr = rest[pg:]
    p = pl.program_id(1)
    umat = u_ref[...]
    rows = N_HEADS * TD

    @pl.when(p == 0)
    def _():
        qbd = _head_block_diag(q_ref[...])
        qbd_scr[...] = qbd
        kvn = _pad_rows(dkvn_ref[...], PAGE)
        strict = _tile_rows(_causal_new_mask(True), N_HEADS)
        nsp, lsig = _stick_terms(_dot_nt(qbd, kvn[:, 0:MIX_W]))
        nsp = jnp.where(strict, nsp, 0.0)
        a = jnp.where(strict, jnp.exp(lsig + _dot3(nsp, umat)), 0.0)
        acc_scr[...] = _dot(a.astype(BF), kvn[:, MIX_W:2 * MIX_W])
        run_scr[...] = jnp.sum(nsp, axis=1, keepdims=True)

    k = jnp.concatenate([r[:, 0:MIX_W].astype(BF) for r in kv_refs], axis=0)
    v = jnp.concatenate([r[:, MIX_W:2 * MIX_W].astype(BF) for r in kv_refs], axis=0)
    nsp, lsig = _stick_terms(_dot_nt(qbd_scr[...], k))
    stack = jnp.concatenate([nsp[:, i * PAGE:(i + 1) * PAGE] for i in range(pg)], axis=0)
    rs = _dot3(stack, umat)
    tot = jnp.sum(stack, axis=1, keepdims=True)
    off = run_scr[...]
    pieces = [None] * pg
    for i in reversed(range(pg)):
        pieces[i] = rs[i * rows:(i + 1) * rows, :] + off
        off = off + tot[i * rows:(i + 1) * rows, :]
    run_scr[...] = off
    a = jnp.exp(lsig + jnp.concatenate(pieces, axis=1))
    acc_scr[...] = acc_scr[...] + _dot(a.astype(BF), v)

    @pl.when(p == nstep - 1)
    def _():
        o_ref[...] = _head_diag_pick(acc_scr[...], TD).astype(BF)


def _stick_decode(prs, cache_dkv, page_table, l, tabs, nstep, pg):
    s = page_table.shape[0]
    return pl.pallas_call(
        functools.partial(_stick_decode_kernel, nstep=nstep, pg=pg),
        grid_spec=pltpu.PrefetchScalarGridSpec(
            num_scalar_prefetch=1, grid=(s, nstep),
            in_specs=[_seq_spec(MIX_W, 2), _seq_spec(2 * MIX_W, 2), _const_spec((PAGE, PAGE), 2)]
            + _page_specs((PAGE, 2 * MIX_W), l, pg, lambda p, i: (nstep - 1 - p) * pg + i),
            out_specs=_seq_spec(MIX_W, 2),
            scratch_shapes=[pltpu.VMEM((N_HEADS * TD, MIX_W), BF), pltpu.VMEM((N_HEADS * TD, MIX_W), F32),
                            pltpu.VMEM((N_HEADS * TD, 1), F32)]),
        out_shape=jax.ShapeDtypeStruct((s, TD, MIX_W), BF),
        compiler_params=_cparams(("arbitrary", "arbitrary")),
        name="stick_decode",
    )(page_table, prs['qd'], prs['dkv_b'], tabs['umat'], *([cache_dkv] * pg))


_IN_A, _IN_B, _IN_C, _IN_D = 968, 1620, 2392, 3160


def _bucket(dist):
    n = jnp.maximum(dist, 0)
    exact = N_BUCKETS // 2
    nf = jnp.maximum(n, 1).astype(F32)
    large = exact + (jnp.log(nf / exact) / math.log(MAX_DIST / exact) * (N_BUCKETS - exact)).astype(jnp.int32)
    return jnp.where(n < exact, n, jnp.minimum(large, N_BUCKETS - 1))


def _bias_table(tab, dist):
    b = _bucket(dist)
    out = jnp.zeros((tab.shape[1],) + dist.shape, F32)
    for j in range(N_BUCKETS):
        out = jnp.where(b[None] == j, tab[j].reshape((-1,) + (1,) * dist.ndim), out)
    return out


def _prep_layer(l, p, tm_max):
    w_in = p['w_in'][l]
    d = w_in.shape[0]
    z = lambda n: jnp.zeros((d, n), w_in.dtype)
    w_proj = jnp.concatenate([w_in[:, 0:_IN_A], z(1024 - _IN_A), w_in[:, _IN_A:_IN_B], z(768 - (_IN_B - _IN_A)),
                              w_in[:, _IN_B:_IN_C], z(896 - (_IN_C - _IN_B)), w_in[:, _IN_C:_IN_D]], axis=1)
    assert w_proj.shape[1] == N_PROJ
    qk = p['qk_gain'][l]
    gain = jnp.ones((N_PROJ,), F32)
    nmask = jnp.zeros((N_PROJ,), F32)
    for c0, g, rep in ((C_AQ, qk[0, 0], 4), (C_AKV, qk[0, 1], 1), (C_BQ, qk[1, 0], 4), (C_BSLC, qk[1, 1], 1),
                       (C_BWIN, qk[1, 1], 1), (C_CQ, qk[2, 0], 4), (C_CK, qk[2, 1], 4)):
        gain = gain.at[c0:c0 + rep * HEAD_DIM].set(jnp.tile(g, rep))
        nmask = nmask.at[c0:c0 + rep * HEAD_DIM].set(1.0)
    gidx = np.arange(MIX_W) // HEAD_DIM
    bd = jnp.asarray((gidx[:, None] == gidx[None, :]).astype(np.float32) / HEAD_DIM, BF)
    tri = jnp.asarray(np.tril(np.ones((tm_max, tm_max), np.float32)), BF)
    return {
        'g_attn': p['norm_attn'][l][None, :], 'w_proj': w_proj.astype(BF), 'gain': gain[None, :],
        'nmask': nmask[None, :], 'b_f': jnp.zeros((1, 128), F32).at[0, :N_HEADS].set(p['b_forget'][l]),
        'bd': bd, 'tri': tri,
        'w_gate': w_in[:, _IN_D:].astype(BF), 'w_br': p['w_branch'][l].astype(BF), 'w_o': p['w_out'][l].astype(BF),
        'g_ffn': p['norm_ffn'][l][None, :], 'w_up': p['w_up'][l].astype(BF), 'conv_w': p['conv_w'][l],
        'conv_b': p['conv_b'][l][None, :], 'w_down': p['w_down'][l].astype(BF),
        'gk_b': qk[1, 1][None, :], 'cmp_w': p['cmp_w'][l], 'cmp_pe': p['cmp_pe'][l],
    }


def _toeplitz_bias(tab):
    r = jnp.arange(TQ)[:, None]
    c = jnp.arange(TKB)[None, :]
    return jnp.stack([_bias_table(tab, dd * TKB + r - c) for dd in range(3)])


def _prompt_tables(rel_bias, t):
    tab_a, tab_b = rel_bias[:, :N_HEADS], rel_bias[:, N_HEADS:]
    nq, nch, ns = t // TQ, t // CMP_STRIDE, t // SEL_BLOCK
    qpos = jnp.arange(t).reshape(nq, TQ)
    cend = jnp.arange(nch) * CMP_STRIDE + (CMP_LEN - 1)
    cbias = jnp.transpose(_bias_table(tab_b, qpos[:, :, None] - cend[None, None, :]), (1, 0, 2, 3))
    n = np.arange(nch)
    gmat = ((n[:, None] // (SEL_BLOCK // CMP_STRIDE) == np.arange(ns)[None, :]) & (n[:, None] < nch - 1))
    emat = np.arange(t)[None, :] // SEL_BLOCK == np.arange(ns)[:, None]
    c = np.arange(TKB)
    return {
        'dsa_tiles': _toeplitz_bias(tab_a), 'nsa_tiles': _toeplitz_bias(tab_b), 'cbias_p': cbias,
        'gmat_p': jnp.asarray(gmat.astype(np.float32), BF), 'emat_p': jnp.asarray(emat.astype(np.float32), BF),
        'tie_p': jnp.asarray((np.arange(ns)[None, :] > np.arange(ns)[:, None]).astype(np.float32)),
        'uaug': jnp.asarray(np.concatenate([(c[:, None] > c[None, :]), np.ones((TKB, TKB), bool)], axis=1)
                            .astype(np.float32), BF),
    }


def _layer_prompt(x, lw, tabs):
    b, t, d = x.shape
    assert t % TQ == 0
    tm = next(c for c in (DENSE_TM, 256, TQ) if t % c == 0)
    x2 = x.reshape(b * t, d)
    pr = _proj(x2, lw, tm=tm, tiles_per_seq=t // tm, with_cumsum=True)
    o_a = _dsa_prompt(pr, tabs['dsa_tiles'], b, t)
    ab = _compress_dense(pr['bcmp'].reshape(b * t // CMP_STRIDE, CMP_STRIDE * 2 * HEAD_DIM), lw, t // CMP_STRIDE)
    o_b = _nsa_prompt(pr, ab, lw, tabs, b, t)
    ckt = jnp.transpose(pr['call'].reshape(b, t, 128)[:, :, :SUBLANES], (0, 2, 1)).reshape(b * SUBLANES, t)
    o_c = _fox_prompt(pr, ckt, b, t)
    o_d = _stick_prompt(pr, tabs['uaug'], b, t)
    xm = _merge(x2, o_a, o_b, o_c, o_d, lw, tm=tm)
    dummy = jnp.zeros((SUBLANES, lw['w_up'].shape[1]), F32)
    y2, conv = _ffn(xm, lw, dummy, dummy, tm=tm, carry_mode=True, tiles_per_seq=t // tm)
    keep = min(WINDOW, t)
    new = {
        'a_kv': pr['akv'].reshape(b, t, 2, HEAD_DIM),
        'a_kidx': pr['amisc'][:, :IDX_DIM].reshape(b, t, IDX_DIM),
        'b_cmp_kv': pr['bcmp'].reshape(b, t, 2, HEAD_DIM),
        'b_slc_kv': pr['bslc'].reshape(b, t, 2, HEAD_DIM),
        'b_win_kv': pr['bwin'].reshape(b, t, 2, HEAD_DIM)[:, t - keep:],
        'c_kv': pr['ckv'].reshape(b, t, 2, N_HEADS, HEAD_DIM),
        'c_logf': pr['logf'][:, :N_HEADS].reshape(b, t, N_HEADS),
        'd_kv': pr['dkv'].reshape(b, t, 2, N_HEADS, HEAD_DIM),
        'ffn_conv': conv.reshape(b, SUBLANES, -1)[:, SUBLANES - (CONV_W - 1):],
    }
    return y2.reshape(b, t, d), new, (o_a, o_b, o_c, o_d)


def _decode_tables(rel_bias, npast, pg):
    tab_a, tab_b = rel_bias[:, :N_HEADS], rel_bias[:, N_HEADS:]
    wstep = pg * PAGE
    tq = jnp.arange(TD)
    rows = lambda tab, dist: _bias_table(tab, dist).reshape(N_HEADS * TD, -1)
    far = lambda tab: jnp.repeat(tab[N_BUCKETS - 1], TD)[:, None]
    d_last = (npast + tq)[:, None] - (npast - wstep + jnp.arange(wstep))[None, :]
    d_new = tq[:, None] - jnp.arange(PAGE)[None, :]
    nch, nsb = npast // CMP_STRIDE, npast // SEL_BLOCK
    d_cmp = (npast + tq)[:, None] - (jnp.arange(nch) * CMP_STRIDE + (CMP_LEN - 1))[None, :]
    d_win = WINDOW + tq[:, None] - jnp.arange(WINDOW)[None, :]
    n = np.arange(nch)
    gmat = (n[:, None] // (SEL_BLOCK // CMP_STRIDE) == np.arange(nsb)[None, :]) & (n[:, None] < nch - 1)
    c = np.arange(PAGE)
    return {
        'a_last': rows(tab_a, d_last), 'a_far': far(tab_a), 'a_new': rows(tab_a, d_new),
        'b_last': rows(tab_b, d_last), 'b_far': far(tab_b), 'b_new': rows(tab_b, d_new),
        'cbias_d': rows(tab_b, d_cmp), 'wb_past': rows(tab_b, d_win),
        'gmat_d': jnp.asarray(gmat.astype(np.float32), BF),
        'umat': jnp.asarray((c[:, None] > c[None, :]).astype(np.float32), BF),
        'tinc': jnp.asarray((c[:, None] <= c[None, :]).astype(np.float32), BF),
    }


def _layer_decode(x, lw, tabs, l, caches, ffn_state, page_table, nstep, pg):
    s, td, d = x.shape
    assert td == TD
    m = s * td
    x2 = x.reshape(m, d)
    pr = _proj(x2, lw, tm=m, tiles_per_seq=1, with_cumsum=False)
    prs = {k: v.reshape(s, td, v.shape[-1]) for k, v in pr.items()}
    o_a = _dsa_decode(prs, caches['a_kidx'], caches['a_kv'], page_table, l, tabs, nstep, pg)
    ab = _compress_paged(caches['b_cmp'], page_table, l, lw, nstep, pg)
    o_b = _nsa_decode(prs, ab, caches['b_win'][l], caches['b_slc'], page_table, l, lw, tabs, nstep, pg)
    lft_new = jnp.pad(jnp.transpose(prs['logf'][:, :, :SUBLANES], (0, 2, 1)), ((0, 0), (0, 0), (0, PAGE - td)))
    o_c = _fox_decode(prs, lft_new, caches['c_kv'], caches['c_lft'], page_table, l, tabs, nstep, pg)
    o_d = _stick_decode(prs, caches['d_kv'], page_table, l, tabs, nstep, pg)
    flat = lambda o: o.reshape(m, MIX_W)
    xm = _merge(x2, flat(o_a), flat(o_b), flat(o_c), flat(o_d), lw, tm=m)
    st0 = jnp.repeat(ffn_state[:, 0], td, axis=0)
    st1 = jnp.repeat(ffn_state[:, 1], td, axis=0)
    y2, conv = _ffn(xm, lw, st0, st1, tm=m, carry_mode=False, tiles_per_seq=1)
    win_new = pr['bwin'].reshape(s, td, 2, HEAD_DIM)
    win_all = jnp.concatenate([caches['b_win'][l].reshape(s, -1, 2, HEAD_DIM), win_new], axis=1)
    keep = min(WINDOW, win_all.shape[1])
    new = {
        'a_kv': pr['akv'].reshape(s, td, 2, HEAD_DIM),
        'a_kidx': pr['amisc'][:, :IDX_DIM].reshape(s, td, IDX_DIM),
        'b_cmp_kv': pr['bcmp'].reshape(s, td, 2, HEAD_DIM),
        'b_slc_kv': pr['bslc'].reshape(s, td, 2, HEAD_DIM),
        'b_win_kv': win_all[:, win_all.shape[1] - keep:],
        'c_kv': pr['ckv'].reshape(s, td, 2, N_HEADS, HEAD_DIM),
        'c_logf': pr['logf'][:, :N_HEADS].reshape(s, td, N_HEADS),
        'd_kv': pr['dkv'].reshape(s, td, 2, N_HEADS, HEAD_DIM),
        'ffn_conv': conv.reshape(s, td, -1)[:, td - (CONV_W - 1):],
    }
    return y2.reshape(s, td, d), new, (o_a, o_b, o_c, o_d)


_STATE_KEYS = ('a_kv', 'a_kidx', 'b_cmp_kv', 'b_slc_kv', 'b_win_kv', 'c_kv', 'c_logf', 'd_kv', 'ffn_conv')


def kernel(x_prompt, x_sample, cache_a_kv, cache_a_kidx, cache_b_cmp_kv, cache_b_slc_kv, state_b_win_kv,
           cache_c_kv, cache_c_logf, cache_d_kv, state_ffn_conv, page_table, rel_bias, norm_attn, w_in,
           b_forget, qk_gain, cmp_w, cmp_pe, w_branch, w_out, norm_ffn, w_up, conv_w, conv_b, w_down):
    params = dict(norm_attn=norm_attn, w_in=w_in, b_forget=b_forget, qk_gain=qk_gain, cmp_w=cmp_w, cmp_pe=cmp_pe,
                  w_branch=w_branch, w_out=w_out, norm_ffn=norm_ffn, w_up=w_up, conv_w=conv_w, conv_b=conv_b,
                  w_down=w_down)
    depth, n_pool = w_in.shape[0], cache_a_kv.shape[1]
    n_seq, n_pages = page_table.shape
    assert cache_a_kv.shape[2] == PAGE
    pg = PAGES_PER_STEP if n_pages % PAGES_PER_STEP == 0 else 1
    nstep = n_pages // pg
    npast = n_pages * PAGE
    caches = {
        'a_kv': cache_a_kv.reshape(depth, n_pool, PAGE, 2 * HEAD_DIM),
        'a_kidx': cache_a_kidx,
        'b_cmp': cache_b_cmp_kv.reshape(depth, n_pool, PAGE // CMP_STRIDE, CMP_STRIDE * 2 * HEAD_DIM),
        'b_slc': cache_b_slc_kv.reshape(depth, n_pool, PAGE, 2 * HEAD_DIM),
        'b_win': state_b_win_kv.reshape(depth, n_seq, state_b_win_kv.shape[2], 2 * HEAD_DIM),
        'c_kv': cache_c_kv.reshape(depth, n_pool, PAGE, 2 * MIX_W),
        'c_lft': jnp.pad(jnp.transpose(cache_c_logf.astype(F32), (0, 1, 3, 2)),
                         ((0, 0), (0, 0), (0, SUBLANES - N_HEADS), (0, 0))),
        'd_kv': cache_d_kv.reshape(depth, n_pool, PAGE, 2 * MIX_W),
    }
    tabs_p = _prompt_tables(rel_bias, x_prompt.shape[1])
    tabs_d = _decode_tables(rel_bias, npast, pg)
    y_p, y_s = x_prompt, x_sample
    new_p = {k: [] for k in _STATE_KEYS}
    new_s = {k: [] for k in _STATE_KEYS}
    for l in range(depth):
        lw = _prep_layer(l, params, DENSE_TM)
        y_p, st, _ = _layer_prompt(y_p, lw, tabs_p)
        for k in _STATE_KEYS:
            new_p[k].append(st[k])
        y_s, st, _ = _layer_decode(y_s, lw, tabs_d, l, caches, state_ffn_conv[l], page_table, nstep, pg)
        for k in _STATE_KEYS:
            new_s[k].append(st[k])
    sp = {k: jnp.stack(v) for k, v in new_p.items()}
    ss = {k: jnp.stack(v) for k, v in new_s.items()}
    return (y_p, y_s,
            sp['a_kv'], ss['a_kv'], sp['a_kidx'], ss['a_kidx'],
            sp['b_cmp_kv'], ss['b_cmp_kv'], sp['b_slc_kv'], ss['b_slc_kv'],
            sp['b_win_kv'], ss['b_win_kv'], sp['c_kv'], ss['c_kv'],
            sp['c_logf'], ss['c_logf'], sp['d_kv'], ss['d_kv'],
            sp['ffn_conv'], ss['ffn_conv'])
```

```python
import functools
import math

import jax
import jax.numpy as jnp
import numpy as np
from jax import lax
from jax.experimental import pallas as pl
from jax.experimental.pallas import tpu as pltpu

HEAD_DIM = 64
N_HEADS = 4
MIX_W = N_HEADS * HEAD_DIM
IDX_HEADS = 8
IDX_DIM = 64
DSA_TOPK = 256
CMP_LEN = 32
CMP_STRIDE = 16
SEL_BLOCK = 64
SEL_SHIFT = 6
HEAD_SHIFT = 6
N_SEL = 16
WINDOW = 512
N_BUCKETS = 32
MAX_DIST = 128
CONV_W = 3
EPS = 1e-6
NEG = -1e30
FORCE = 1e4
QK_SCALE = HEAD_DIM ** -0.5

LANES = 128
SUBLANES = 8
VMEM_LIMIT = 56 * 1024 * 1024
INT_MIN = -2 ** 31

DENSE_TM = 512
TQ = 128
TKB = 128
PAGES_PER_STEP = 32

BF = jnp.bfloat16
F32 = jnp.float32

C_AQ, C_AKV, C_AQIDX, C_AMISC = 0, 256, 384, 896
C_BQ, C_BCMP, C_BSLC, C_BWIN, C_BGATE = 1024, 1280, 1408, 1536, 1664
C_CQ, C_CK, C_CV, C_CF = 1792, 2048, 2304, 2560
C_DQ, C_DKV = 2688, 2944
N_PROJ = 3456


def _cparams(sem):
    return pltpu.CompilerParams(dimension_semantics=sem, vmem_limit_bytes=VMEM_LIMIT)


def _dot(a, b):
    return jnp.dot(a, b, preferred_element_type=F32)


def _dot_nt(a, b):
    return lax.dot_general(a, b, (((1,), (1,)), ((), ())), preferred_element_type=F32)


def _split3(x):
    hi = x.astype(BF)
    r1 = x - hi.astype(F32)
    mid = r1.astype(BF)
    lo = (r1 - mid.astype(F32)).astype(BF)
    return hi, mid, lo


def _dot3(x, m):
    hi, mid, lo = _split3(x)
    return _dot(hi, m) + _dot(mid, m) + _dot(lo, m)


def _dot3_l(m, x):
    hi, mid, lo = _split3(x)
    return _dot(m, hi) + _dot(m, mid) + _dot(m, lo)


def _sortable(x):
    x = jnp.where(x == 0.0, 0.0, x)
    b = lax.bitcast_convert_type(x, jnp.int32)
    return b ^ ((b >> 31) & 0x7FFFFFFF)


def _log_sigmoid(v):
    return jnp.minimum(v, 0.0) - jnp.log1p(jnp.exp(-jnp.abs(v)))


def _rms_rows(x, g):
    ms = jnp.mean(x * x, axis=-1, keepdims=True)
    return (x * lax.rsqrt(ms + EPS)) * g


def _proj_kernel(x_ref, g_ref, w_ref, gain_ref, nmask_ref, bf_ref, bd_ref, tri_ref,
                 qa_o, akv_o, akv_b, qidx_o, amisc_o, kidx_b, qb_o, bcmp_o, bslc_o, bslc_b, bwin_o, bwin_b,
                 gates_o, qc_o, ckv_o, ckv_b, logf_o, call_o, qd_o, dkv_o, dkv_b, carry_scr,
                 *, tiles_per_seq, with_cumsum):
    h = _rms_rows(x_ref[...], g_ref[...]).astype(BF)

    def cols(c0, width):
        return _dot(h, w_ref[:, c0:c0 + width])

    def normed(c0, width):
        slab = cols(c0, width)
        ms = _dot((slab * slab).astype(BF), bd_ref[0:width, 0:width])
        scale = lax.rsqrt(ms + EPS) * gain_ref[:, c0:c0 + width]
        return slab * jnp.where(nmask_ref[:, c0:c0 + width] > 0.0, scale, 1.0)

    qa_o[...] = (normed(C_AQ, MIX_W) * QK_SCALE).astype(BF)
    akv = normed(C_AKV, 128)
    akv_o[...] = akv
    akv_b[...] = akv.astype(BF)
    qidx_o[...] = cols(C_AQIDX, IDX_HEADS * IDX_DIM).astype(BF)
    amisc = cols(C_AMISC, 128)
    amisc_o[...] = amisc
    kidx_b[...] = amisc[:, 0:IDX_DIM].astype(BF)
    qb_o[...] = (normed(C_BQ, MIX_W) * QK_SCALE).astype(BF)
    bcmp_o[...] = cols(C_BCMP, 128)
    bslc = normed(C_BSLC, 128)
    bslc_o[...] = bslc
    bslc_b[...] = bslc.astype(BF)
    bwin = normed(C_BWIN, 128)
    bwin_o[...] = bwin
    bwin_b[...] = bwin.astype(BF)
    gates_o[...] = jax.nn.sigmoid(cols(C_BGATE, 128))
    qc_o[...] = (normed(C_CQ, MIX_W) * QK_SCALE).astype(BF)
    ck = normed(C_CK, MIX_W)
    cv = cols(C_CV, MIX_W)
    ckv_o[:, 0:MIX_W] = ck
    ckv_o[:, MIX_W:2 * MIX_W] = cv
    ckv_b[:, 0:MIX_W] = ck.astype(BF)
    ckv_b[:, MIX_W:2 * MIX_W] = cv.astype(BF)
    logf = _log_sigmoid(cols(C_CF, 128) + bf_ref[...])
    logf_o[...] = logf
    if with_cumsum:
        t = pl.program_id(0)

        @pl.when(t % tiles_per_seq == 0)
        def _():
            carry_scr[...] = jnp.zeros_like(carry_scr)

        c = _dot3_l(tri_ref[...], logf) + carry_scr[0:1, :]
        call_o[...] = c
        carry_scr[...] = jnp.broadcast_to(c[-1:, :], carry_scr.shape)
    else:
        call_o[...] = logf
    qd_o[...] = (cols(C_DQ, MIX_W) * QK_SCALE).astype(BF)
    dkv = cols(C_DKV, 2 * MIX_W)
    dkv_o[...] = dkv
    dkv_b[...] = dkv.astype(BF)


def _proj(x2d, lw, *, tm, tiles_per_seq, with_cumsum):
    m, d = x2d.shape
    assert m % tm == 0
    row = lambda i: (i, 0)
    const = lambda i: (0, 0)

    def o(width, dtype):
        return jax.ShapeDtypeStruct((m, width), dtype), pl.BlockSpec((tm, width), row)

    outs = [o(MIX_W, BF), o(128, F32), o(128, BF), o(IDX_HEADS * IDX_DIM, BF), o(128, F32), o(IDX_DIM, BF),
            o(MIX_W, BF), o(128, F32), o(128, F32), o(128, BF), o(128, F32), o(128, BF),
            o(128, F32), o(MIX_W, BF), o(2 * MIX_W, F32), o(2 * MIX_W, BF), o(128, F32), o(128, F32),
            o(MIX_W, BF), o(2 * MIX_W, F32), o(2 * MIX_W, BF)]
    names = ('qa', 'akv', 'akv_b', 'qidx', 'amisc', 'kidx_b', 'qb', 'bcmp', 'bslc', 'bslc_b', 'bwin', 'bwin_b',
             'gates', 'qc', 'ckv', 'ckv_b', 'logf', 'call', 'qd', 'dkv', 'dkv_b')
    res = pl.pallas_call(
        functools.partial(_proj_kernel, tiles_per_seq=tiles_per_seq, with_cumsum=with_cumsum),
        grid=(m // tm,),
        in_specs=[pl.BlockSpec((tm, d), row), pl.BlockSpec((1, d), const), pl.BlockSpec((d, N_PROJ), const),
                  pl.BlockSpec((1, N_PROJ), const), pl.BlockSpec((1, N_PROJ), const), pl.BlockSpec((1, 128), const),
                  pl.BlockSpec((MIX_W, MIX_W), const), pl.BlockSpec((tm, tm), const)],
        out_specs=[s for _, s in outs],
        out_shape=[s for s, _ in outs],
        scratch_shapes=[pltpu.VMEM((SUBLANES, 128), F32)],
        compiler_params=_cparams(("arbitrary",)),
        name="proj",
    )(x2d, lw['g_attn'], lw['w_proj'], lw['gain'], lw['nmask'], lw['b_f'], lw['bd'], lw['tri'][:tm, :tm])
    return dict(zip(names, res))


def _merge_kernel(x_ref, g_ref, wg_ref, oa_ref, ob_ref, oc_ref, od_ref, wbr_ref, wo_ref, y_ref):
    x = x_ref[...]
    d = x.shape[1]
    h = _rms_rows(x, g_ref[...]).astype(BF)
    m = None
    for i, o_ref in enumerate((oa_ref, ob_ref, oc_ref, od_ref)):
        gate = jax.nn.sigmoid(_dot(h, wg_ref[:, i * d:(i + 1) * d]))
        term = gate * _dot(o_ref[...], wbr_ref[i])
        m = term if m is None else m + term
    y_ref[...] = x + _dot(m.astype(BF), wo_ref[...])


def _merge(x2d, o_a, o_b, o_c, o_d, lw, *, tm):
    m, d = x2d.shape
    row = lambda i: (i, 0)
    const = lambda i: (0, 0)
    return pl.pallas_call(
        _merge_kernel,
        grid=(m // tm,),
        in_specs=[pl.BlockSpec((tm, d), row), pl.BlockSpec((1, d), const), pl.BlockSpec((d, N_HEADS * d), const),
                  pl.BlockSpec((tm, MIX_W), row), pl.BlockSpec((tm, MIX_W), row), pl.BlockSpec((tm, MIX_W), row),
                  pl.BlockSpec((tm, MIX_W), row), pl.BlockSpec((4, MIX_W, d), lambda i: (0, 0, 0)),
                  pl.BlockSpec((d, d), const)],
        out_specs=pl.BlockSpec((tm, d), row),
        out_shape=jax.ShapeDtypeStruct((m, d), F32),
        compiler_params=_cparams(("arbitrary",)),
        name="merge",
    )(x2d, lw['g_attn'], lw['w_gate'], o_a, o_b, o_c, o_d, lw['w_br'], lw['w_o'])


def _ffn_kernel(x_ref, g_ref, wup_ref, cw_ref, cb_ref, wdn_ref, st0_ref, st1_ref, y_ref, conv_o, prev_scr,
                *, carry_mode, tiles_per_seq, d_ff, cw):
    x = x_ref[...]
    tm = x.shape[0]
    h = _rms_rows(x, g_ref[...]).astype(BF)
    row = lax.broadcasted_iota(jnp.int32, (tm, cw), 0)
    if carry_mode:
        t = pl.program_id(0)

        @pl.when(t % tiles_per_seq == 0)
        def _():
            prev_scr[...] = jnp.zeros_like(prev_scr)
    else:
        rowmod = row % SUBLANES

    def conv_cols(c0):
        up = _dot(h, wup_ref[:, c0:c0 + cw])
        r1 = pltpu.roll(up, 1, axis=0)
        r2 = pltpu.roll(up, 2, axis=0)
        if carry_mode:
            p6 = prev_scr[6:7, c0:c0 + cw]
            p7 = prev_scr[7:8, c0:c0 + cw]
            u1 = jnp.where(row == 0, p7, r1)
            u2 = jnp.where(row == 0, p6, jnp.where(row == 1, p7, r2))
            prev_scr[:, c0:c0 + cw] = up[tm - SUBLANES:tm, :]
            conv_o[:, c0:c0 + cw] = up[tm - SUBLANES:tm, :]
        else:
            s0 = st0_ref[:, c0:c0 + cw]
            s1 = st1_ref[:, c0:c0 + cw]
            u1 = jnp.where(rowmod == 0, s1, r1)
            u2 = jnp.where(rowmod == 0, s0, jnp.where(rowmod == 1, s1, r2))
            conv_o[:, c0:c0 + cw] = up
        conv = (u2 * cw_ref[0:1, c0:c0 + cw] + u1 * cw_ref[1:2, c0:c0 + cw]) + up * cw_ref[2:3, c0:c0 + cw]
        return cb_ref[:, c0:c0 + cw] + conv

    acc = None
    for c in range(d_ff // cw):
        val = conv_cols(c * cw)
        gate = conv_cols(d_ff + c * cw)
        act = (gate * jax.nn.sigmoid(gate)) * val
        part = _dot(act.astype(BF), wdn_ref[c * cw:(c + 1) * cw, :])
        acc = part if acc is None else acc + part
    y_ref[...] = x + acc


def _ffn(x2d, lw, st0, st1, *, tm, carry_mode, tiles_per_seq):
    m, d = x2d.shape
    d_ff = lw['w_down'].shape[0]
    cw = 256
    assert d_ff % cw == 0 and m % tm == 0
    row = lambda i: (i, 0)
    const = lambda i: (0, 0)
    if carry_mode:
        n_seq = m // (tm * tiles_per_seq)
        conv_shape = jax.ShapeDtypeStruct((n_seq * SUBLANES, 2 * d_ff), F32)
        conv_spec = pl.BlockSpec((SUBLANES, 2 * d_ff), lambda i: (i // tiles_per_seq, 0))
        st_spec = pl.BlockSpec((SUBLANES, 2 * d_ff), const)
    else:
        conv_shape = jax.ShapeDtypeStruct((m, 2 * d_ff), F32)
        conv_spec = pl.BlockSpec((tm, 2 * d_ff), row)
        st_spec = pl.BlockSpec((tm, 2 * d_ff), row)
    return pl.pallas_call(
        functools.partial(_ffn_kernel, carry_mode=carry_mode, tiles_per_seq=tiles_per_seq, d_ff=d_ff, cw=cw),
        grid=(m // tm,),
        in_specs=[pl.BlockSpec((tm, d), row), pl.BlockSpec((1, d), const), pl.BlockSpec((d, 2 * d_ff), const),
                  pl.BlockSpec((CONV_W, 2 * d_ff), const), pl.BlockSpec((1, 2 * d_ff), const),
                  pl.BlockSpec((d_ff, d), const), st_spec, st_spec],
        out_specs=[pl.BlockSpec((tm, d), row), conv_spec],
        out_shape=[jax.ShapeDtypeStruct((m, d), F32), conv_shape],
        scratch_shapes=[pltpu.VMEM((SUBLANES, 2 * d_ff), F32)],
        compiler_params=_cparams(("arbitrary",)),
        name="ffn",
    )(x2d, lw['g_ffn'], lw['w_up'], lw['conv_w'], lw['conv_b'], lw['w_down'], st0, st1)


def _osm_update(s, mask, m, l, acc, v):
    if mask is not None:
        s = jnp.where(mask, s, NEG)
    m_new = jnp.maximum(m, jnp.max(s, axis=1, keepdims=True))
    p = jnp.exp(s - m_new)
    if mask is not None:
        p = jnp.where(mask, p, 0.0)
    alpha = jnp.exp(m - m_new)
    l = alpha * l + jnp.sum(p, axis=1, keepdims=True)
    acc = alpha * acc + _dot(p.astype(BF), v)
    return m_new, l, acc


def _osm_ref_init(m_ref, l_ref, acc_ref):
    m_ref[...] = jnp.full(m_ref.shape, NEG, F32)
    l_ref[...] = jnp.zeros(l_ref.shape, F32)
    acc_ref[...] = jnp.zeros(acc_ref.shape, F32)


def _osm_ref_step(s, mask, v, m_ref, l_ref, acc_ref):
    reps = s.shape[1] // LANES
    if mask is not None:
        s = jnp.where(mask, s, NEG)
    m_prev = m_ref[...]
    m_new = jnp.maximum(m_prev, jnp.max(s, axis=1, keepdims=True))
    p = jnp.exp(s - jnp.tile(m_new, (1, reps)))
    if mask is not None:
        p = jnp.where(mask, p, 0.0)
    alpha = jnp.exp(m_prev - m_new)
    l_ref[...] = alpha * l_ref[...] + jnp.sum(p, axis=1, keepdims=True)
    d = acc_ref.shape[1]
    a = alpha[:, 0:d] if d <= LANES else jnp.tile(alpha, (1, d // LANES))
    acc_ref[...] = a * acc_ref[...] + _dot(p.astype(BF), v)
    m_ref[...] = m_new


def _key_chunk(t):
    return 4 * TKB if t % (4 * TKB) == 0 else TKB


def _tile_bias(bias_ref, i, first_tile, ntiles):
    parts = [bias_ref[jnp.clip(i - (first_tile + u), 0, 2)].reshape(N_HEADS * TQ, TKB) for u in range(ntiles)]
    return parts[0] if ntiles == 1 else jnp.concatenate(parts, axis=1)


def _pad_lanes(q):
    return jnp.concatenate([q, jnp.zeros_like(q)], axis=1)


def _heads_to_rows(q):
    return jnp.concatenate([q[:, h * HEAD_DIM:(h + 1) * HEAD_DIM] for h in range(N_HEADS)], axis=0)


def _rows_to_heads(o, r):
    return jnp.concatenate([o[h * r:(h + 1) * r, :] for h in range(N_HEADS)], axis=1)


def _tile_rows(a, n):
    return jnp.concatenate([a] * n, axis=0)


def _unsortable(key):
    return lax.bitcast_convert_type(key ^ ((key >> 31) & 0x7FFFFFFF), F32)


VALUE_BISECTIONS = 14
KEY_BISECTIONS = 34


def _kth_search(count_ge, k, vmin, vmax, n_valid):
    lo0 = _sortable(vmin)
    hi0 = _sortable(vmax) + 1

    def done_of(lo, hi, c_lo):
        return (c_lo <= k) | (hi == lo + 1)

    def step(st, midpoint):
        lo, hi, c_lo, c_hi = st
        done = done_of(lo, hi, c_lo)
        cand = jnp.where(done, lo, jnp.clip(midpoint(lo, hi), lo + 1, hi - 1))
        c = count_ge(cand)
        up = (c >= k) & jnp.logical_not(done)
        down = (c < k) & jnp.logical_not(done)
        return jnp.where(up, cand, lo), jnp.where(down, cand, hi), jnp.where(up, c, c_lo), jnp.where(down, c, c_hi)

    def value_mid(lo, hi):
        return _sortable(0.5 * _unsortable(lo) + 0.5 * _unsortable(hi - 1))

    def key_mid(lo, hi):
        return (lo >> 1) + (hi >> 1) + (lo & hi & 1)

    st = (lo0, hi0, n_valid, jnp.zeros_like(n_valid))
    st = lax.fori_loop(0, VALUE_BISECTIONS, lambda _, s: step(s, value_mid), st)

    def cond(carry):
        it, (lo, hi, c_lo, _) = carry
        pending = jnp.max(jnp.where(done_of(lo, hi, c_lo), 0.0, 1.0)) > 0.0
        return (it < KEY_BISECTIONS) & pending

    _, (lo, _, c_lo, c_hi) = lax.while_loop(cond, lambda c: (c[0] + 1, step(c[1], key_mid)), (jnp.int32(0), st))
    return lo, c_lo, c_hi


def _dsa_prompt_kernel(qidx_ref, amisc_ref, qa_ref, kidx_ref, akv_ref, bias_ref, o_ref, key_scr, m_scr, l_scr, acc_scr,
                       *, topk, kt):
    i = pl.program_id(1)
    chunk = kt // TKB
    nchunk = i // chunk + 1
    amisc = amisc_ref[...]
    qidx = qidx_ref[...]
    qh = [qidx[:, h * IDX_DIM:(h + 1) * IDX_DIM] for h in range(IDX_HEADS)]
    wb = [jnp.broadcast_to(amisc[:, IDX_DIM + h:IDX_DIM + h + 1], (TQ, LANES)) for h in range(IDX_HEADS)]
    qpos = i * TQ + lax.broadcasted_iota(jnp.int32, (TQ, kt), 0)
    kcol = lax.broadcasted_iota(jnp.int32, (TQ, kt), 1)

    def score_chunk(c, carry):
        vmax, vmin = carry
        off = pl.multiple_of(c * kt, kt)
        kb = kidx_ref[pl.ds(off, kt), :]
        acc = None
        for h in range(IDX_HEADS):
            term = jnp.maximum(_dot_nt(qh[h], kb), 0.0) * jnp.tile(wb[h], (1, chunk))
            acc = term if acc is None else acc + term
        visible = off + kcol <= qpos
        key_scr[:, pl.ds(off, kt)] = jnp.where(visible, _sortable(acc), INT_MIN)
        hi_part = jnp.where(visible, acc, -jnp.inf)
        lo_part = jnp.where(visible, acc, jnp.inf)
        for u in range(chunk):
            vmax = jnp.maximum(vmax, hi_part[:, u * TKB:(u + 1) * TKB])
            vmin = jnp.minimum(vmin, lo_part[:, u * TKB:(u + 1) * TKB])
        return vmax, vmin

    vmax, vmin = lax.fori_loop(0, nchunk, score_chunk,
                               (jnp.full((TQ, TKB), -jnp.inf, F32), jnp.full((TQ, TKB), jnp.inf, F32)))
    vmax = jnp.max(vmax, axis=1, keepdims=True)
    vmin = jnp.min(vmin, axis=1, keepdims=True)
    n_valid = i * TQ + lax.broadcasted_iota(jnp.int32, (TQ, 1), 0) + 1

    cw = kt

    def count(pred):
        def body(c, acc):
            kc = key_scr[:, pl.ds(pl.multiple_of(c * cw, cw), cw)]
            hit = jnp.where(pred(kc, c * cw), 1.0, 0.0)
            part = hit[:, 0:TKB]
            for u in range(1, chunk):
                part = part + hit[:, u * TKB:(u + 1) * TKB]
            return acc + part
        acc = lax.fori_loop(0, nchunk, body, jnp.zeros((TQ, TKB), F32))
        return jnp.sum(acc, axis=1, keepdims=True).astype(jnp.int32)

    thr, c_lo, c_hi = _kth_search(lambda cand: count(lambda kc, o: kc >= cand), topk, vmin, vmax, n_valid)

    need = topk - c_hi
    tie = c_lo > topk
    any_tie = jnp.max(jnp.where(tie, 1.0, 0.0)) > 0.0
    lane_c = lax.broadcasted_iota(jnp.int32, (TQ, cw), 1)

    @pl.when(any_tie)
    def _():
        nbits = max(1, int(math.ceil(math.log2(key_scr.shape[1]))))

        def bit_body(b, lo):
            cand = lo + jnp.left_shift(jnp.int32(1), nbits - 1 - b)
            cnt = count(lambda kc, o: (kc == thr) & ((lane_c + o) < cand))
            return jnp.where(cnt < need, cand, lo)

        jmax = lax.fori_loop(0, nbits, bit_body, jnp.zeros((TQ, 1), jnp.int32))

        def demote(c, carry):
            sl = pl.ds(pl.multiple_of(c * cw, cw), cw)
            kc = key_scr[:, sl]
            drop = tie & (kc == thr) & ((lane_c + c * cw) > jmax)
            key_scr[:, sl] = jnp.where(drop, thr - 1, kc)
            return carry

        lax.fori_loop(0, nchunk, demote, 0)

    thr_eff = jnp.maximum(thr, INT_MIN + 1)
    q4 = _pad_lanes(_heads_to_rows(qa_ref[...]))
    _osm_ref_init(m_scr, l_scr, acc_scr)

    def att_chunk(c, carry):
        off = pl.multiple_of(c * kt, kt)
        kv = akv_ref[pl.ds(off, kt), :]
        sel = key_scr[:, pl.ds(off, kt)] >= thr_eff
        s = _dot_nt(q4, kv) + _tile_bias(bias_ref, i, c * chunk, chunk)
        _osm_ref_step(s, _tile_rows(sel, N_HEADS), kv, m_scr, l_scr, acc_scr)
        return carry

    lax.fori_loop(0, nchunk, att_chunk, 0)
    o4 = acc_scr[...] / l_scr[...]
    o_ref[...] = _rows_to_heads(o4[:, HEAD_DIM:2 * HEAD_DIM], TQ).astype(BF)


def _dsa_prompt(pr, bias, b, t):
    nq = t // TQ
    topk = min(DSA_TOPK, t // 4)
    rows = N_HEADS * TQ
    q_spec = lambda w: pl.BlockSpec((TQ, w), lambda bi, qi: (bi * nq + qi, 0))
    kv_spec = lambda w: pl.BlockSpec((t, w), lambda bi, qi: (bi, 0))
    return pl.pallas_call(
        functools.partial(_dsa_prompt_kernel, topk=topk, kt=_key_chunk(t)),
        grid=(b, nq),
        in_specs=[q_spec(IDX_HEADS * IDX_DIM), q_spec(128), q_spec(MIX_W), kv_spec(IDX_DIM), kv_spec(128),
                  pl.BlockSpec((3, N_HEADS, TQ, TKB), lambda bi, qi: (0, 0, 0, 0))],
        out_specs=q_spec(MIX_W),
        out_shape=jax.ShapeDtypeStruct((b * t, MIX_W), BF),
        scratch_shapes=[pltpu.VMEM((TQ, t), jnp.int32), pltpu.VMEM((rows, LANES), F32),
                        pltpu.VMEM((rows, LANES), F32), pltpu.VMEM((rows, LANES), F32)],
        compiler_params=_cparams(("arbitrary", "arbitrary")),
        name="dsa_prompt",
    )(pr['qidx'], pr['amisc'], pr['qa'], pr['kidx_b'], pr['akv_b'], bias)


def _compress_kernel(*refs, n_in):
    x_refs, (pe_ref, w_ref, o_ref) = refs[:n_in], refs[n_in:]
    x = x_refs[0][...] if n_in == 1 else jnp.concatenate([r[...] for r in x_refs], axis=0)
    o_ref[:, 0:128] = _dot((x + pe_ref[0:1, :]).astype(BF), w_ref[0])
    o_ref[:, 128:256] = _dot((x + pe_ref[1:2, :]).astype(BF), w_ref[1])


def _compress_weights(lw):
    w4 = lw['cmp_w'].reshape(2, CMP_LEN, HEAD_DIM, HEAD_DIM)
    pe = lw['cmp_pe']

    def half(rs):
        wk, wv = w4[0, rs], w4[1, rs]
        z = jnp.zeros_like(wk)
        rows = jnp.stack([jnp.concatenate([wk, z], axis=-1), jnp.concatenate([z, wv], axis=-1)], axis=1)
        return rows.reshape(CMP_STRIDE * 2 * HEAD_DIM, 2 * HEAD_DIM)

    top, bot = slice(0, CMP_STRIDE), slice(CMP_STRIDE, CMP_LEN)
    w = jnp.stack([half(top), half(bot)]).astype(BF)
    pef = jnp.stack([jnp.transpose(pe[:, top], (1, 0, 2)).reshape(-1), jnp.transpose(pe[:, bot], (1, 0, 2)).reshape(-1)])
    return pef, w


def _compress_dense(chunks, lw, rows_per_step):
    n, width = chunks.shape
    pef, w = _compress_weights(lw)
    return pl.pallas_call(
        functools.partial(_compress_kernel, n_in=1),
        grid=(n // rows_per_step,),
        in_specs=[pl.BlockSpec((rows_per_step, width), lambda i: (i, 0)), pl.BlockSpec((2, width), lambda i: (0, 0)),
                  pl.BlockSpec((2, width, 128), lambda i: (0, 0, 0))],
        out_specs=pl.BlockSpec((rows_per_step, 256), lambda i: (i, 0)),
        out_shape=jax.ShapeDtypeStruct((n, 256), F32),
        compiler_params=_cparams(("arbitrary",)),
        name="compress_prompt",
    )(chunks, pef, w)


def _combine_compressed(ab, gk):
    n = ab.shape[0]
    kv = ab[:, 0:128] + pltpu.roll(ab[:, 128:256], n - 1, axis=0)
    ck_raw = kv[:, 0:HEAD_DIM]
    ms = jnp.mean(ck_raw * ck_raw, axis=-1, keepdims=True)
    ck = (ck_raw * lax.rsqrt(ms + EPS)) * gk
    return ck.astype(BF), kv[:, HEAD_DIM:2 * HEAD_DIM].astype(BF)


def _masked_softmax_rows(s, valid):
    s = jnp.where(valid, s, NEG)
    m = jnp.max(s, axis=1, keepdims=True)
    p = jnp.where(valid, jnp.exp(s - m), 0.0)
    l = jnp.sum(p, axis=1, keepdims=True)
    return p * (1.0 / jnp.where(l > 0.0, l, 1.0))


def _gate_cols(g, c):
    return jnp.concatenate([g[:, h * 3 + c:h * 3 + c + 1] for h in range(N_HEADS)], axis=0)


def _nsa_prompt_kernel(qb_ref, gates_ref, ab_ref, gk_ref, cbias_ref, gmat_ref, emat_ref, tie_ref, slc_ref, win_ref, bias_ref,
                       o_ref, ck_scr, cv_scr, selm_scr, m_scr, l_scr, acc_scr, *, n_sel, ns, nch, kt, wt, nq):
    i = pl.program_id(1)

    @pl.when(i == 0)
    def _():
        ck, cv = _combine_compressed(ab_ref[...], gk_ref[...])
        ck_scr[...] = ck
        cv_scr[...] = cv

    q4 = _heads_to_rows(qb_ref[...])
    qpos = i * TQ + lax.broadcasted_iota(jnp.int32, (TQ, 1), 0)

    cend = lax.broadcasted_iota(jnp.int32, (TQ, nch), 1) * CMP_STRIDE + (CMP_LEN - 1)
    cvalid = cend <= qpos
    s = _dot_nt(q4, ck_scr[...]) + cbias_ref[...].reshape(N_HEADS * TQ, nch)
    pc = _masked_softmax_rows(s, _tile_rows(cvalid, N_HEADS))
    o_c = _dot(pc.astype(BF), cv_scr[...])
    pcs = pc[0:TQ] + pc[TQ:2 * TQ] + pc[2 * TQ:3 * TQ] + pc[3 * TQ:4 * TQ]
    imp = _dot3(pcs, gmat_ref[...])

    j_io = lax.broadcasted_iota(jnp.int32, (TQ, ns), 1)
    cur = qpos >> SEL_SHIFT
    gap = cur - j_io
    score = jnp.where(gap >= 0, jnp.where((j_io == 0) | (gap <= 1), FORCE, imp), NEG)
    rank = jnp.zeros((TQ, ns), F32)
    for jj in range(ns):
        col = score[:, jj:jj + 1]
        rank = rank + (jnp.where(col > score, 1.0, 0.0) + jnp.where(col == score, tie_ref[jj:jj + 1, :], 0.0))
    sel = jnp.where(gap >= 0, rank, float(ns)) < n_sel
    selm_scr[...] = _dot(jnp.where(sel, 1.0, 0.0).astype(BF), emat_ref[...])

    q4p = _pad_lanes(q4)
    chunk = kt // TKB
    qrow = i * TQ + lax.broadcasted_iota(jnp.int32, (TQ, kt), 0)
    kcol = lax.broadcasted_iota(jnp.int32, (TQ, kt), 1)
    _osm_ref_init(m_scr, l_scr, acc_scr)

    def sel_chunk(c, carry):
        off = pl.multiple_of(c * kt, kt)
        kv = slc_ref[pl.ds(off, kt), :]
        mask = (selm_scr[:, pl.ds(off, kt)] > 0.5) & (off + kcol <= qrow)
        s = _dot_nt(q4p, kv) + _tile_bias(bias_ref, i, c * chunk, chunk)
        _osm_ref_step(s, _tile_rows(mask, N_HEADS), kv, m_scr, l_scr, acc_scr)
        return carry

    lax.fori_loop(0, i // chunk + 1, sel_chunk, 0)
    o_s = (acc_scr[...] / l_scr[...])[:, HEAD_DIM:2 * HEAD_DIM]

    w0 = jnp.clip(i - (wt - 1), 0, nq - wt)
    woff = pl.multiple_of(w0 * TKB, TKB)
    kv = win_ref[pl.ds(woff, wt * TKB), :]
    wd = (i * TQ + lax.broadcasted_iota(jnp.int32, (TQ, wt * TKB), 0)) - (
        woff + lax.broadcasted_iota(jnp.int32, (TQ, wt * TKB), 1))
    s = _dot_nt(q4p, kv) + _tile_bias(bias_ref, i, w0, wt)
    pw = _masked_softmax_rows(s, _tile_rows((wd >= 0) & (wd <= WINDOW), N_HEADS))
    o_w = _dot(pw.astype(BF), kv)[:, HEAD_DIM:2 * HEAD_DIM]

    g = gates_ref[...]
    o = _gate_cols(g, 0) * o_c + _gate_cols(g, 1) * o_s + _gate_cols(g, 2) * o_w
    o_ref[...] = _rows_to_heads(o, TQ).astype(BF)


def _nsa_prompt(pr, ab, lw, tabs, b, t):
    nq = t // TQ
    nch = t // CMP_STRIDE
    ns = t // SEL_BLOCK
    n_sel = min(N_SEL, ns)
    q_spec = lambda w: pl.BlockSpec((TQ, w), lambda bi, qi: (bi * nq + qi, 0))
    kv_spec = lambda w: pl.BlockSpec((t, w), lambda bi, qi: (bi, 0))
    const2 = lambda bi, qi: (0, 0)
    return pl.pallas_call(
        functools.partial(_nsa_prompt_kernel, n_sel=n_sel, ns=ns, nch=nch, kt=_key_chunk(t),
                          wt=min(WINDOW // TKB + 1, nq), nq=nq),
        grid=(b, nq),
        in_specs=[q_spec(MIX_W), q_spec(128), pl.BlockSpec((nch, 256), lambda bi, qi: (bi, 0)),
                  pl.BlockSpec((1, HEAD_DIM), const2),
                  pl.BlockSpec((None, N_HEADS, TQ, nch), lambda bi, qi: (qi, 0, 0, 0)),
                  pl.BlockSpec((nch, ns), const2), pl.BlockSpec((ns, t), const2), pl.BlockSpec((ns, ns), const2),
                  kv_spec(128), kv_spec(128),
                  pl.BlockSpec((3, N_HEADS, TQ, TKB), lambda bi, qi: (0, 0, 0, 0))],
        out_specs=q_spec(MIX_W),
        out_shape=jax.ShapeDtypeStruct((b * t, MIX_W), BF),
        scratch_shapes=[pltpu.VMEM((nch, HEAD_DIM), BF), pltpu.VMEM((nch, HEAD_DIM), BF), pltpu.VMEM((TQ, t), F32),
                        pltpu.VMEM((N_HEADS * TQ, LANES), F32), pltpu.VMEM((N_HEADS * TQ, LANES), F32),
                        pltpu.VMEM((N_HEADS * TQ, LANES), F32)],
        compiler_params=_cparams(("arbitrary", "arbitrary")),
        name="nsa_prompt",
    )(pr['qb'], pr['gates'], ab, lw['gk_b'], tabs['cbias_p'], tabs['gmat_p'], tabs['emat_p'], tabs['tie_p'],
      pr['bslc_b'], pr['bwin_b'], tabs['nsa_tiles'])


def _fox_prompt_kernel(q_ref, cq_ref, ckv_ref, ckt_ref, o_ref, m_scr, l_scr, acc_scr, *, kt):
    i = pl.program_id(1)
    chunk = kt // TKB
    nfull = i // chunk
    q4 = _head_block_diag(q_ref[...])
    cq = cq_ref[...]
    cq4 = jnp.concatenate([jnp.broadcast_to(cq[:, h:h + 1], (TQ, LANES)) for h in range(N_HEADS)], axis=0)
    qpos = i * TQ + lax.broadcasted_iota(jnp.int32, (TQ, kt), 0)
    kcol = lax.broadcasted_iota(jnp.int32, (TQ, kt), 1)
    _osm_ref_init(m_scr, l_scr, acc_scr)

    def do_chunk(c, masked):
        off = pl.multiple_of(c * kt, kt)
        kv = ckv_ref[pl.ds(off, kt), :]
        ck4 = jnp.concatenate([jnp.broadcast_to(ckt_ref[h:h + 1, pl.ds(off, kt)], (TQ, kt)) for h in range(N_HEADS)],
                              axis=0)
        s = (_dot_nt(q4, kv[:, 0:MIX_W]) + jnp.tile(cq4, (1, chunk))) - ck4
        mask = _tile_rows(off + kcol <= qpos, N_HEADS) if masked else None
        _osm_ref_step(s, mask, kv[:, MIX_W:2 * MIX_W], m_scr, l_scr, acc_scr)

    def body(c, carry):
        do_chunk(c, False)
        return carry

    lax.fori_loop(0, nfull, body, 0)
    do_chunk(nfull, True)
    o4 = acc_scr[...] / jnp.tile(l_scr[...], (1, MIX_W // LANES))
    o_ref[...] = _head_diag_pick(o4, TQ).astype(BF)


def _fox_prompt(pr, ckt, b, t):
    nq = t // TQ
    rows = N_HEADS * TQ
    q_spec = lambda w: pl.BlockSpec((TQ, w), lambda bi, qi: (bi * nq + qi, 0))
    return pl.pallas_call(
        functools.partial(_fox_prompt_kernel, kt=_key_chunk(t)),
        grid=(b, nq),
        in_specs=[q_spec(MIX_W), q_spec(128), pl.BlockSpec((t, 2 * MIX_W), lambda bi, qi: (bi, 0)),
                  pl.BlockSpec((SUBLANES, t), lambda bi, qi: (bi, 0))],
        out_specs=q_spec(MIX_W),
        out_shape=jax.ShapeDtypeStruct((b * t, MIX_W), BF),
        scratch_shapes=[pltpu.VMEM((rows, LANES), F32), pltpu.VMEM((rows, LANES), F32), pltpu.VMEM((rows, MIX_W), F32)],
        compiler_params=_cparams(("arbitrary", "arbitrary")),
        name="fox_prompt",
    )(pr['qc'], pr['call'], pr['ckv_b'], ckt)


def _stick_terms(z):
    e = jnp.log(1.0 + jnp.exp(-jnp.abs(z)))
    nsp = -(jnp.maximum(z, 0.0) + e)
    return nsp, z + nsp


def _suffix_sums(x, uaug, ntiles):
    r = x.shape[0]
    stack = x if ntiles == 1 else jnp.concatenate([x[:, u * TKB:(u + 1) * TKB] for u in range(ntiles)], axis=0)
    hi = stack.astype(BF)
    lo = (stack - hi.astype(F32)).astype(BF)
    rs = _dot(hi, uaug) + _dot(lo, uaug)
    return [(rs[u * r:(u + 1) * r, 0:TKB], rs[u * r:(u + 1) * r, TKB:2 * TKB]) for u in range(ntiles)]


def _stick_prompt_kernel(q_ref, dkv_ref, u_ref, o_ref, acc_scr, run_scr, *, kt):
    i = pl.program_id(1)
    chunk = kt // TKB
    last = i // chunk
    q4 = _head_block_diag(q_ref[...])
    uaug = u_ref[...]
    qpos = i * TQ + lax.broadcasted_iota(jnp.int32, (TQ, kt), 0)
    kcol = lax.broadcasted_iota(jnp.int32, (TQ, kt), 1)
    acc_scr[...] = jnp.zeros(acc_scr.shape, F32)
    run_scr[...] = jnp.zeros(run_scr.shape, F32)

    def do_chunk(c, masked):
        off = pl.multiple_of(c * kt, kt)
        kv = dkv_ref[pl.ds(off, kt), :]
        nsp, lsig = _stick_terms(_dot_nt(q4, kv[:, 0:MIX_W]))
        if masked:
            ok = _tile_rows(off + kcol < qpos, N_HEADS)
            nsp = jnp.where(ok, nsp, 0.0)
        sums = _suffix_sums(nsp, uaug, chunk)
        run = run_scr[...]
        pieces = [None] * chunk
        for u in reversed(range(chunk)):
            pieces[u] = sums[u][0] + run
            run = run + sums[u][1]
        run_scr[...] = run
        a = jnp.exp(lsig + (pieces[0] if chunk == 1 else jnp.concatenate(pieces, axis=1)))
        if masked:
            a = jnp.where(ok, a, 0.0)
        acc_scr[...] = acc_scr[...] + _dot(a.astype(BF), kv[:, MIX_W:2 * MIX_W])

    def body(jj, carry):
        @pl.when(jj == 0)
        def _():
            do_chunk(last, True)

        @pl.when(jj > 0)
        def _():
            do_chunk(last - jj, False)

        return carry

    lax.fori_loop(0, last + 1, body, 0)
    o_ref[...] = _head_diag_pick(acc_scr[...], TQ).astype(BF)


def _stick_prompt(pr, uaug, b, t):
    nq = t // TQ
    rows = N_HEADS * TQ
    q_spec = lambda w: pl.BlockSpec((TQ, w), lambda bi, qi: (bi * nq + qi, 0))
    return pl.pallas_call(
        functools.partial(_stick_prompt_kernel, kt=_key_chunk(t)),
        grid=(b, nq),
        in_specs=[q_spec(MIX_W), pl.BlockSpec((t, 2 * MIX_W), lambda bi, qi: (bi, 0)),
                  pl.BlockSpec((TKB, 2 * TKB), lambda bi, qi: (0, 0))],
        out_specs=q_spec(MIX_W),
        out_shape=jax.ShapeDtypeStruct((b * t, MIX_W), BF),
        scratch_shapes=[pltpu.VMEM((rows, MIX_W), F32), pltpu.VMEM((rows, LANES), F32)],
        compiler_params=_cparams(("arbitrary", "arbitrary")),
        name="stick_prompt",
    )(pr['qd'], pr['dkv_b'], uaug)


PAGE = 128
TD = SUBLANES


def _page_specs(page_shape, l, pg, page_of):
    tail = (0,) * len(page_shape)

    def spec(i):
        def index_map(*args):
            pt = args[-1]
            return (l, pt[args[0], page_of(*args[1:-1], i)]) + tail
        return pl.BlockSpec((None, None) + page_shape, index_map)

    return [spec(i) for i in range(pg)]


def _pad_rows(a, n):
    return jnp.concatenate([a, jnp.zeros((n - a.shape[0], a.shape[1]), a.dtype)], axis=0)


def _topk_select_ref(key_ref, k):
    rows, width = key_ref.shape

    def count(pred):
        return jnp.sum(jnp.where(pred(key_ref[...]), 1.0, 0.0), axis=1, keepdims=True).astype(jnp.int32)

    keys = key_ref[...]
    valid = keys > INT_MIN
    vals = _unsortable(keys)
    vmax = jnp.max(jnp.where(valid, vals, -jnp.inf), axis=1, keepdims=True)
    vmin = jnp.min(jnp.where(valid, vals, jnp.inf), axis=1, keepdims=True)
    thr, c_lo, c_hi = _kth_search(lambda cand: count(lambda kk: kk >= cand), k, vmin, vmax,
                                  count(lambda kk: kk > INT_MIN))
    need = k - c_hi
    tie = c_lo > k
    any_tie = jnp.max(jnp.where(tie, 1.0, 0.0)) > 0.0

    @pl.when(any_tie)
    def _():
        idx = lax.broadcasted_iota(jnp.int32, (rows, width), 1)
        nbits = max(1, int(math.ceil(math.log2(width))))

        def bit_body(b, lo):
            cand = lo + jnp.left_shift(jnp.int32(1), nbits - 1 - b)
            cnt = count(lambda kk: (kk == thr) & (idx < cand))
            return jnp.where(cnt < need, cand, lo)

        jmax = lax.fori_loop(0, nbits, bit_body, jnp.zeros((rows, 1), jnp.int32))
        kk = key_ref[...]
        key_ref[...] = jnp.where(tie & (kk == thr) & (idx > jmax), thr - 1, kk)

    return jnp.maximum(thr, INT_MIN + 1)


def _osm_scratch_update(s, mask, v, m_scr, l_scr, acc_scr):
    m, l, acc = _osm_update(s, mask, m_scr[...], l_scr[...], acc_scr[...], v)
    m_scr[...] = m
    l_scr[...] = l
    acc_scr[...] = acc


def _osm_scratch_init(m_scr, l_scr, acc_scr):
    m_scr[...] = jnp.full(m_scr.shape, NEG, F32)
    l_scr[...] = jnp.zeros(l_scr.shape, F32)
    acc_scr[...] = jnp.zeros(acc_scr.shape, F32)


def _head_block_diag(q):
    lane = lax.broadcasted_iota(jnp.int32, q.shape, 1)
    return jnp.concatenate([jnp.where(lane >> HEAD_SHIFT == h, q, jnp.zeros_like(q)) for h in range(N_HEADS)], axis=0)


def _head_diag_pick(o, r):
    lane = lax.broadcasted_iota(jnp.int32, (r, o.shape[1]), 1)
    out = None
    for h in range(N_HEADS):
        part = jnp.where(lane >> HEAD_SHIFT == h, o[h * r:(h + 1) * r, :], 0.0)
        out = part if out is None else out + part
    return out


def _causal_new_mask(strict):
    lane = lax.broadcasted_iota(jnp.int32, (TD, PAGE), 1)
    rowi = lax.broadcasted_iota(jnp.int32, (TD, PAGE), 0)
    return lane < rowi if strict else lane <= rowi


def _compress_paged(cache, page_table, l, lw, nstep, pg):
    s = page_table.shape[0]
    pef, w = _compress_weights(lw)
    width = pef.shape[1]
    rows = PAGE // CMP_STRIDE

    def body(pt_ref, *refs):
        _compress_kernel(*refs, n_in=pg)

    return pl.pallas_call(
        body,
        grid_spec=pltpu.PrefetchScalarGridSpec(
            num_scalar_prefetch=1, grid=(s, nstep),
            in_specs=_page_specs((rows, width), l, pg, lambda p, i: p * pg + i)
            + [pl.BlockSpec((2, width), lambda si, p, pt: (0, 0)), pl.BlockSpec((2, width, 128), lambda si, p, pt: (0, 0, 0))],
            out_specs=pl.BlockSpec((None, pg * rows, 256), lambda si, p, pt: (si, p, 0))),
        out_shape=jax.ShapeDtypeStruct((s, nstep * pg * rows, 256), F32),
        compiler_params=_cparams(("arbitrary", "arbitrary")),
        name="compress_decode",
    )(page_table, *([cache] * pg), pef, w)


def _dsa_decode_kernel(pt_ref, qidx_ref, amisc_ref, qa_ref, kidxn_ref, akvn_ref, blast_ref, bfar_ref, bnew_ref, *rest,
                       topk, nstep, pg):
    kid_refs, kv_refs = rest[:pg], rest[pg:2 * pg]
    o_ref, key_scr, thr_scr, m_scr, l_scr, acc_scr = rest[2 * pg:]
    ph, p = pl.program_id(1), pl.program_id(2)
    wstep = pg * PAGE
    npast = nstep * wstep
    last = p == nstep - 1

    @pl.when(ph == 0)
    def _():
        qidx = qidx_ref[...]
        w = amisc_ref[:, IDX_DIM:IDX_DIM + IDX_HEADS]
        q64 = jnp.concatenate([qidx[:, h * IDX_DIM:(h + 1) * IDX_DIM] for h in range(IDX_HEADS)], axis=0)

        def scores(keys):
            sc = _dot_nt(q64, keys)
            acc = None
            for h in range(IDX_HEADS):
                term = jnp.maximum(sc[h * TD:(h + 1) * TD, :], 0.0) * w[:, h:h + 1]
                acc = term if acc is None else acc + term
            return acc

        kid = jnp.concatenate([r[...].astype(BF) for r in kid_refs], axis=0)
        key_scr[:, pl.ds(pl.multiple_of(p * wstep, wstep), wstep)] = _sortable(scores(kid))

        @pl.when(last)
        def _():
            acc = scores(_pad_rows(kidxn_ref[...], PAGE))
            key_scr[:, npast:npast + PAGE] = jnp.where(_causal_new_mask(False), _sortable(acc), INT_MIN)
            thr_scr[...] = jnp.broadcast_to(_topk_select_ref(key_scr, topk), thr_scr.shape)

    @pl.when(ph == 1)
    def _():
        @pl.when(p == 0)
        def _():
            _osm_scratch_init(m_scr, l_scr, acc_scr)

        thr = thr_scr[:, 0:1]
        q4 = _pad_lanes(_heads_to_rows(qa_ref[...]))
        kv = jnp.concatenate([r[...].astype(BF) for r in kv_refs], axis=0)
        bias = jnp.where(last, blast_ref[...], bfar_ref[...])
        s = _dot_nt(q4, kv) + bias
        sel = key_scr[:, pl.ds(pl.multiple_of(p * wstep, wstep), wstep)] >= thr
        _osm_scratch_update(s, _tile_rows(sel, N_HEADS), kv, m_scr, l_scr, acc_scr)

        @pl.when(last)
        def _():
            kvn = _pad_rows(akvn_ref[...], PAGE)
            s = _dot_nt(q4, kvn) + bnew_ref[...]
            sel = key_scr[:, npast:npast + PAGE] >= thr
            m, l, acc = _osm_update(s, _tile_rows(sel, N_HEADS), m_scr[...], l_scr[...], acc_scr[...], kvn)
            o_ref[...] = _rows_to_heads((acc / l)[:, HEAD_DIM:2 * HEAD_DIM], TD).astype(BF)


def _seq_spec(width, ngrid):
    if ngrid == 2:
        return pl.BlockSpec((None, TD, width), lambda si, p, pt: (si, 0, 0))
    return pl.BlockSpec((None, TD, width), lambda si, ph, p, pt: (si, 0, 0))


def _const_spec(shape, ngrid):
    zeros = (0,) * len(shape)
    if ngrid == 2:
        return pl.BlockSpec(shape, lambda si, p, pt: zeros)
    return pl.BlockSpec(shape, lambda si, ph, p, pt: zeros)


def _dsa_decode(prs, cache_kidx, cache_akv, page_table, l, tabs, nstep, pg):
    s = page_table.shape[0]
    npast = nstep * pg * PAGE
    topk = min(DSA_TOPK, (npast + TD) // 4)
    wstep = pg * PAGE
    kid_specs = _page_specs((PAGE, IDX_DIM), l, pg, lambda ph, p, i: jnp.where(ph == 0, p, nstep - 1) * pg + i)
    kv_specs = _page_specs((PAGE, 128), l, pg, lambda ph, p, i: jnp.where(ph == 0, 0, p) * pg + i)
    return pl.pallas_call(
        functools.partial(_dsa_decode_kernel, topk=topk, nstep=nstep, pg=pg),
        grid_spec=pltpu.PrefetchScalarGridSpec(
            num_scalar_prefetch=1, grid=(s, 2, nstep),
            in_specs=[_seq_spec(IDX_HEADS * IDX_DIM, 3), _seq_spec(128, 3), _seq_spec(MIX_W, 3), _seq_spec(IDX_DIM, 3),
                      _seq_spec(128, 3), _const_spec((N_HEADS * TD, wstep), 3), _const_spec((N_HEADS * TD, 1), 3),
                      _const_spec((N_HEADS * TD, PAGE), 3)] + kid_specs + kv_specs,
            out_specs=_seq_spec(MIX_W, 3),
            scratch_shapes=[pltpu.VMEM((TD, npast + PAGE), jnp.int32), pltpu.VMEM((TD, 128), jnp.int32),
                            pltpu.VMEM((N_HEADS * TD, 1), F32), pltpu.VMEM((N_HEADS * TD, 1), F32),
                            pltpu.VMEM((N_HEADS * TD, 2 * HEAD_DIM), F32)]),
        out_shape=jax.ShapeDtypeStruct((s, TD, MIX_W), BF),
        compiler_params=_cparams(("arbitrary", "arbitrary", "arbitrary")),
        name="dsa_decode",
    )(page_table, prs['qidx'], prs['amisc'], prs['qa'], prs['kidx_b'], prs['akv_b'],
      tabs['a_last'], tabs['a_far'], tabs['a_new'], *([cache_kidx] * pg), *([cache_akv] * pg))


def _nsa_decode_kernel(pt_ref, qb_ref, gates_ref, ab_ref, gk_ref, cbias_ref, gmat_ref, slcn_ref, winp_ref, winn_ref,
                       blast_ref, bfar_ref, bnew_ref, wbp_ref, *rest, n_sel, nstep, pg):
    slc_refs = rest[:pg]
    o_ref, selm_scr, bkey_scr, oc_scr, m_scr, l_scr, acc_scr = rest[pg:]
    p = pl.program_id(1)
    wstep = pg * PAGE
    npast = nstep * wstep
    nch = npast // CMP_STRIDE
    nsb = npast // SEL_BLOCK
    last = p == nstep - 1
    q4 = _heads_to_rows(qb_ref[...])

    @pl.when(p == 0)
    def _():
        ck, cv = _combine_compressed(ab_ref[...], gk_ref[...])
        qpos = npast + lax.broadcasted_iota(jnp.int32, (TD, 1), 0)
        cend = lax.broadcasted_iota(jnp.int32, (TD, nch), 1) * CMP_STRIDE + (CMP_LEN - 1)
        s = _dot_nt(q4, ck) + cbias_ref[...]
        pc = _masked_softmax_rows(s, _tile_rows(cend <= qpos, N_HEADS))
        oc_scr[...] = _dot(pc.astype(BF), cv)
        pcs = pc[0:TD] + pc[TD:2 * TD] + pc[2 * TD:3 * TD] + pc[3 * TD:4 * TD]
        imp = _dot3(pcs, gmat_ref[...])
        j_io = lax.broadcasted_iota(jnp.int32, (TD, nsb), 1)
        forced = (j_io == 0) | (j_io == nsb - 1)
        bkey_scr[...] = _sortable(jnp.where(forced, FORCE, imp))
        thr = _topk_select_ref(bkey_scr, n_sel - 1)
        selb = jnp.where(bkey_scr[...] >= thr, 1.0, 0.0).astype(BF)
        blk = lax.broadcasted_iota(jnp.int32, (nsb, wstep), 0)
        col = lax.broadcasted_iota(jnp.int32, (nsb, wstep), 1)
        for c in range(nstep):
            expand = jnp.where(blk == (c * wstep + col) >> SEL_SHIFT, 1.0, 0.0).astype(BF)
            selm_scr[:, c * wstep:(c + 1) * wstep] = _dot(selb, expand)
        _osm_scratch_init(m_scr, l_scr, acc_scr)

    q4p = _pad_lanes(q4)
    kv = jnp.concatenate([r[...].astype(BF) for r in slc_refs], axis=0)
    bias = jnp.where(last, blast_ref[...], bfar_ref[...])
    s = _dot_nt(q4p, kv) + bias
    mask = selm_scr[:, pl.ds(pl.multiple_of(p * wstep, wstep), wstep)] > 0.5
    _osm_scratch_update(s, _tile_rows(mask, N_HEADS), kv, m_scr, l_scr, acc_scr)

    @pl.when(last)
    def _():
        hi = lambda a: a[:, HEAD_DIM:2 * HEAD_DIM]
        new_mask = _tile_rows(_causal_new_mask(False), N_HEADS)
        kvn = _pad_rows(slcn_ref[...], PAGE)
        s = _dot_nt(q4p, kvn) + bnew_ref[...]
        _, l_s, acc_s = _osm_update(s, new_mask, m_scr[...], l_scr[...], acc_scr[...], kvn)
        wp = winp_ref[...].astype(BF)
        nw = wp.shape[0]
        wd = nw + lax.broadcasted_iota(jnp.int32, (TD, nw), 0) - lax.broadcasted_iota(jnp.int32, (TD, nw), 1)
        s = _dot_nt(q4p, wp) + wbp_ref[...]
        carry = _osm_update(s, _tile_rows(wd <= WINDOW, N_HEADS), jnp.full((N_HEADS * TD, 1), NEG, F32),
                            jnp.zeros((N_HEADS * TD, 1), F32), jnp.zeros((N_HEADS * TD, 2 * HEAD_DIM), F32), wp)
        kvw = _pad_rows(winn_ref[...], PAGE)
        s = _dot_nt(q4p, kvw) + bnew_ref[...]
        _, l_w, acc_w = _osm_update(s, new_mask, *carry, kvw)
        g = gates_ref[...]
        o = (_gate_cols(g, 0) * oc_scr[...] + _gate_cols(g, 1) * hi(acc_s / l_s) + _gate_cols(g, 2) * hi(acc_w / l_w))
        o_ref[...] = _rows_to_heads(o, TD).astype(BF)


def _nsa_decode(prs, ab, win_past, cache_slc, page_table, l, lw, tabs, nstep, pg):
    s = page_table.shape[0]
    wstep = pg * PAGE
    npast = nstep * wstep
    nch, nsb = npast // CMP_STRIDE, npast // SEL_BLOCK
    n_sel = min(N_SEL, nsb + 1)
    assert n_sel >= 2 and win_past.shape[1] == WINDOW
    return pl.pallas_call(
        functools.partial(_nsa_decode_kernel, n_sel=n_sel, nstep=nstep, pg=pg),
        grid_spec=pltpu.PrefetchScalarGridSpec(
            num_scalar_prefetch=1, grid=(s, nstep),
            in_specs=[_seq_spec(MIX_W, 2), _seq_spec(128, 2),
                      pl.BlockSpec((None, nch, 256), lambda si, p, pt: (si, 0, 0)), _const_spec((1, HEAD_DIM), 2),
                      _const_spec((N_HEADS * TD, nch), 2), _const_spec((nch, nsb), 2), _seq_spec(128, 2),
                      pl.BlockSpec((None, WINDOW, 128), lambda si, p, pt: (si, 0, 0)), _seq_spec(128, 2),
                      _const_spec((N_HEADS * TD, wstep), 2), _const_spec((N_HEADS * TD, 1), 2),
                      _const_spec((N_HEADS * TD, PAGE), 2), _const_spec((N_HEADS * TD, WINDOW), 2)]
            + _page_specs((PAGE, 128), l, pg, lambda p, i: p * pg + i),
            out_specs=_seq_spec(MIX_W, 2),
            scratch_shapes=[pltpu.VMEM((TD, npast), F32), pltpu.VMEM((TD, nsb), jnp.int32),
                            pltpu.VMEM((N_HEADS * TD, HEAD_DIM), F32), pltpu.VMEM((N_HEADS * TD, 1), F32),
                            pltpu.VMEM((N_HEADS * TD, 1), F32), pltpu.VMEM((N_HEADS * TD, 2 * HEAD_DIM), F32)]),
        out_shape=jax.ShapeDtypeStruct((s, TD, MIX_W), BF),
        compiler_params=_cparams(("arbitrary", "arbitrary")),
        name="nsa_decode",
    )(page_table, prs['qb'], prs['gates'], ab, lw['gk_b'], tabs['cbias_d'], tabs['gmat_d'], prs['bslc_b'], win_past,
      prs['bwin_b'], tabs['b_last'], tabs['b_far'], tabs['b_new'], tabs['wb_past'], *([cache_slc] * pg))


def _fox_decode_kernel(pt_ref, q_ref, lfn_ref, lftn_ref, ckvn_ref, u_ref, tinc_ref, *rest, nstep, pg):
    kv_refs, lf_refs = rest[:pg], rest[pg:2 * pg]
    o_ref, qbd_scr, cq_scr, m_scr, l_scr, acc_scr, carry_scr = rest[2 * pg:]
    p = pl.program_id(1)

    @pl.when(p == 0)
    def _():
        qbd = _head_block_diag(q_ref[...])
        qbd_scr[...] = qbd
        lf = lfn_ref[...]
        rows = [lf[0:1, :]]
        for r in range(1, TD):
            rows.append(rows[-1] + lf[r:r + 1, :])
        npf = jnp.concatenate(rows, axis=0)
        cq4 = jnp.concatenate([npf[:, h:h + 1] for h in range(N_HEADS)], axis=0)
        cq_scr[...] = cq4
        npt = _dot3(lftn_ref[...], tinc_ref[...])
        ck4 = jnp.concatenate([jnp.broadcast_to(npt[h:h + 1, :], (TD, PAGE)) for h in range(N_HEADS)], axis=0)
        kvn = _pad_rows(ckvn_ref[...], PAGE)
        s = (_dot_nt(qbd, kvn[:, 0:MIX_W]) + cq4) - ck4
        _osm_scratch_init(m_scr, l_scr, acc_scr)
        _osm_scratch_update(s, _tile_rows(_causal_new_mask(False), N_HEADS), kvn[:, MIX_W:2 * MIX_W], m_scr, l_scr, acc_scr)
        carry_scr[...] = jnp.zeros_like(carry_scr)

    x = jnp.concatenate([r[...] for r in lf_refs], axis=0)
    rs = _dot3(x, u_ref[...])
    tot = jnp.sum(x, axis=1, keepdims=True)
    off = carry_scr[:, 0:1]
    pieces = [None] * pg
    for i in reversed(range(pg)):
        s_i = rs[i * SUBLANES:(i + 1) * SUBLANES, :] + off
        pieces[i] = jnp.concatenate([jnp.broadcast_to(s_i[h:h + 1, :], (TD, PAGE)) for h in range(N_HEADS)], axis=0)
        off = off + tot[i * SUBLANES:(i + 1) * SUBLANES, :]
    carry_scr[...] = jnp.broadcast_to(off, carry_scr.shape)
    k = jnp.concatenate([r[:, 0:MIX_W].astype(BF) for r in kv_refs], axis=0)
    v = jnp.concatenate([r[:, MIX_W:2 * MIX_W].astype(BF) for r in kv_refs], axis=0)
    s = (_dot_nt(qbd_scr[...], k) + cq_scr[...]) + jnp.concatenate(pieces, axis=1)
    _osm_scratch_update(s, None, v, m_scr, l_scr, acc_scr)

    @pl.when(p == nstep - 1)
    def _():
        o_ref[...] = _head_diag_pick(acc_scr[...] / l_scr[...], TD).astype(BF)


def _fox_decode(prs, lft_new, cache_ckv, cache_lft, page_table, l, tabs, nstep, pg):
    s = page_table.shape[0]
    rev = lambda p, i: (nstep - 1 - p) * pg + i
    return pl.pallas_call(
        functools.partial(_fox_decode_kernel, nstep=nstep, pg=pg),
        grid_spec=pltpu.PrefetchScalarGridSpec(
            num_scalar_prefetch=1, grid=(s, nstep),
            in_specs=[_seq_spec(MIX_W, 2), _seq_spec(128, 2), _seq_spec(128, 2), _seq_spec(2 * MIX_W, 2),
                      _const_spec((PAGE, PAGE), 2), _const_spec((PAGE, PAGE), 2)]
            + _page_specs((PAGE, 2 * MIX_W), l, pg, rev) + _page_specs((SUBLANES, PAGE), l, pg, rev),
            out_specs=_seq_spec(MIX_W, 2),
            scratch_shapes=[pltpu.VMEM((N_HEADS * TD, MIX_W), BF), pltpu.VMEM((N_HEADS * TD, 1), F32),
                            pltpu.VMEM((N_HEADS * TD, 1), F32), pltpu.VMEM((N_HEADS * TD, 1), F32),
                            pltpu.VMEM((N_HEADS * TD, MIX_W), F32), pltpu.VMEM((SUBLANES, 128), F32)]),
        out_shape=jax.ShapeDtypeStruct((s, TD, MIX_W), BF),
        compiler_params=_cparams(("arbitrary", "arbitrary")),
        name="fox_decode",
    )(page_table, prs['qc'], prs['logf'], lft_new, prs['ckv_b'], tabs['umat'], tabs['tinc'],
      *([cache_ckv] * pg), *([cache_lft] * pg))


def _stick_decode_kernel(pt_ref, q_ref, dkvn_ref, u_ref, *rest, nstep, pg):
    kv_refs = rest[:pg]
    o_ref, qbd_scr, acc_scr, run_scr = rest[pg:]
    p = pl.program_id(1)
    umat = u_ref[...]
    rows = N_HEADS * TD

    @pl.when(p == 0)
    def _():
        qbd = _head_block_diag(q_ref[...])
        qbd_scr[...] = qbd
        kvn = _pad_rows(dkvn_ref[...], PAGE)
        strict = _tile_rows(_causal_new_mask(True), N_HEADS)
        nsp, lsig = _stick_terms(_dot_nt(qbd, kvn[:, 0:MIX_W]))
        nsp = jnp.where(strict, nsp, 0.0)
        a = jnp.where(strict, jnp.exp(lsig + _dot3(nsp, umat)), 0.0)
        acc_scr[...] = _dot(a.astype(BF), kvn[:, MIX_W:2 * MIX_W])
        run_scr[...] = jnp.sum(nsp, axis=1, keepdims=True)

    k = jnp.concatenate([r[:, 0:MIX_W].astype(BF) for r in kv_refs], axis=0)
    v = jnp.concatenate([r[:, MIX_W:2 * MIX_W].astype(BF) for r in kv_refs], axis=0)
    nsp, lsig = _stick_terms(_dot_nt(qbd_scr[...], k))
    stack = jnp.concatenate([nsp[:, i * PAGE:(i + 1) * PAGE] for i in range(pg)], axis=0)
    rs = _dot3(stack, umat)
    tot = jnp.sum(stack, axis=1, keepdims=True)
    off = run_scr[...]
    pieces = [None] * pg
    for i in reversed(range(pg)):
        pieces[i] = rs[i * rows:(i + 1) * rows, :] + off
        off = off + tot[i * rows:(i + 1) * rows, :]
    run_scr[...] = off
    a = jnp.exp(lsig + jnp.concatenate(pieces, axis=1))
    acc_scr[...] = acc_scr[...] + _dot(a.astype(BF), v)

    @pl.when(p == nstep - 1)
    def _():
        o_ref[...] = _head_diag_pick(acc_scr[...], TD).astype(BF)


def _stick_decode(prs, cache_dkv, page_table, l, tabs, nstep, pg):
    s = page_table.shape[0]
    return pl.pallas_call(
        functools.partial(_stick_decode_kernel, nstep=nstep, pg=pg),
        grid_spec=pltpu.PrefetchScalarGridSpec(
            num_scalar_prefetch=1, grid=(s, nstep),
            in_specs=[_seq_spec(MIX_W, 2), _seq_spec(2 * MIX_W, 2), _const_spec((PAGE, PAGE), 2)]
            + _page_specs((PAGE, 2 * MIX_W), l, pg, lambda p, i: (nstep - 1 - p) * pg + i),
            out_specs=_seq_spec(MIX_W, 2),
            scratch_shapes=[pltpu.VMEM((N_HEADS * TD, MIX_W), BF), pltpu.VMEM((N_HEADS * TD, MIX_W), F32),
                            pltpu.VMEM((N_HEADS * TD, 1), F32)]),
        out_shape=jax.ShapeDtypeStruct((s, TD, MIX_W), BF),
        compiler_params=_cparams(("arbitrary", "arbitrary")),
        name="stick_decode",
    )(page_table, prs['qd'], prs['dkv_b'], tabs['umat'], *([cache_dkv] * pg))


_IN_A, _IN_B, _IN_C, _IN_D = 968, 1620, 2392, 3160


def _bucket(dist):
    n = jnp.maximum(dist, 0)
    exact = N_BUCKETS // 2
    nf = jnp.maximum(n, 1).astype(F32)
    large = exact + (jnp.log(nf / exact) / math.log(MAX_DIST / exact) * (N_BUCKETS - exact)).astype(jnp.int32)
    return jnp.where(n < exact, n, jnp.minimum(large, N_BUCKETS - 1))


def _bias_table(tab, dist):
    b = _bucket(dist)
    out = jnp.zeros((tab.shape[1],) + dist.shape, F32)
    for j in range(N_BUCKETS):
        out = jnp.where(b[None] == j, tab[j].reshape((-1,) + (1,) * dist.ndim), out)
    return out


def _prep_layer(l, p, tm_max):
    w_in = p['w_in'][l]
    d = w_in.shape[0]
    z = lambda n: jnp.zeros((d, n), w_in.dtype)
    w_proj = jnp.concatenate([w_in[:, 0:_IN_A], z(1024 - _IN_A), w_in[:, _IN_A:_IN_B], z(768 - (_IN_B - _IN_A)),
                              w_in[:, _IN_B:_IN_C], z(896 - (_IN_C - _IN_B)), w_in[:, _IN_C:_IN_D]], axis=1)
    assert w_proj.shape[1] == N_PROJ
    qk = p['qk_gain'][l]
    gain = jnp.ones((N_PROJ,), F32)
    nmask = jnp.zeros((N_PROJ,), F32)
    for c0, g, rep in ((C_AQ, qk[0, 0], 4), (C_AKV, qk[0, 1], 1), (C_BQ, qk[1, 0], 4), (C_BSLC, qk[1, 1], 1),
                       (C_BWIN, qk[1, 1], 1), (C_CQ, qk[2, 0], 4), (C_CK, qk[2, 1], 4)):
        gain = gain.at[c0:c0 + rep * HEAD_DIM].set(jnp.tile(g, rep))
        nmask = nmask.at[c0:c0 + rep * HEAD_DIM].set(1.0)
    gidx = np.arange(MIX_W) // HEAD_DIM
    bd = jnp.asarray((gidx[:, None] == gidx[None, :]).astype(np.float32) / HEAD_DIM, BF)
    tri = jnp.asarray(np.tril(np.ones((tm_max, tm_max), np.float32)), BF)
    return {
        'g_attn': p['norm_attn'][l][None, :], 'w_proj': w_proj.astype(BF), 'gain': gain[None, :],
        'nmask': nmask[None, :], 'b_f': jnp.zeros((1, 128), F32).at[0, :N_HEADS].set(p['b_forget'][l]),
        'bd': bd, 'tri': tri,
        'w_gate': w_in[:, _IN_D:].astype(BF), 'w_br': p['w_branch'][l].astype(BF), 'w_o': p['w_out'][l].astype(BF),
        'g_ffn': p['norm_ffn'][l][None, :], 'w_up': p['w_up'][l].astype(BF), 'conv_w': p['conv_w'][l],
        'conv_b': p['conv_b'][l][None, :], 'w_down': p['w_down'][l].astype(BF),
        'gk_b': qk[1, 1][None, :], 'cmp_w': p['cmp_w'][l], 'cmp_pe': p['cmp_pe'][l],
    }


def _toeplitz_bias(tab):
    r = jnp.arange(TQ)[:, None]
    c = jnp.arange(TKB)[None, :]
    return jnp.stack([_bias_table(tab, dd * TKB + r - c) for dd in range(3)])


def _prompt_tables(rel_bias, t):
    tab_a, tab_b = rel_bias[:, :N_HEADS], rel_bias[:, N_HEADS:]
    nq, nch, ns = t // TQ, t // CMP_STRIDE, t // SEL_BLOCK
    qpos = jnp.arange(t).reshape(nq, TQ)
    cend = jnp.arange(nch) * CMP_STRIDE + (CMP_LEN - 1)
    cbias = jnp.transpose(_bias_table(tab_b, qpos[:, :, None] - cend[None, None, :]), (1, 0, 2, 3))
    n = np.arange(nch)
    gmat = ((n[:, None] // (SEL_BLOCK // CMP_STRIDE) == np.arange(ns)[None, :]) & (n[:, None] < nch - 1))
    emat = np.arange(t)[None, :] // SEL_BLOCK == np.arange(ns)[:, None]
    c = np.arange(TKB)
    return {
        'dsa_tiles': _toeplitz_bias(tab_a), 'nsa_tiles': _toeplitz_bias(tab_b), 'cbias_p': cbias,
        'gmat_p': jnp.asarray(gmat.astype(np.float32), BF), 'emat_p': jnp.asarray(emat.astype(np.float32), BF),
        'tie_p': jnp.asarray((np.arange(ns)[None, :] > np.arange(ns)[:, None]).astype(np.float32)),
        'uaug': jnp.asarray(np.concatenate([(c[:, None] > c[None, :]), np.ones((TKB, TKB), bool)], axis=1)
                            .astype(np.float32), BF),
    }


def _layer_prompt(x, lw, tabs):
    b, t, d = x.shape
    assert t % TQ == 0
    tm = next(c for c in (DENSE_TM, 256, TQ) if t % c == 0)
    x2 = x.reshape(b * t, d)
    pr = _proj(x2, lw, tm=tm, tiles_per_seq=t // tm, with_cumsum=True)
    o_a = _dsa_prompt(pr, tabs['dsa_tiles'], b, t)
    ab = _compress_dense(pr['bcmp'].reshape(b * t // CMP_STRIDE, CMP_STRIDE * 2 * HEAD_DIM), lw, t // CMP_STRIDE)
    o_b = _nsa_prompt(pr, ab, lw, tabs, b, t)
    ckt = jnp.transpose(pr['call'].reshape(b, t, 128)[:, :, :SUBLANES], (0, 2, 1)).reshape(b * SUBLANES, t)
    o_c = _fox_prompt(pr, ckt, b, t)
    o_d = _stick_prompt(pr, tabs['uaug'], b, t)
    xm = _merge(x2, o_a, o_b, o_c, o_d, lw, tm=tm)
    dummy = jnp.zeros((SUBLANES, lw['w_up'].shape[1]), F32)
    y2, conv = _ffn(xm, lw, dummy, dummy, tm=tm, carry_mode=True, tiles_per_seq=t // tm)
    keep = min(WINDOW, t)
    new = {
        'a_kv': pr['akv'].reshape(b, t, 2, HEAD_DIM),
        'a_kidx': pr['amisc'][:, :IDX_DIM].reshape(b, t, IDX_DIM),
        'b_cmp_kv': pr['bcmp'].reshape(b, t, 2, HEAD_DIM),
        'b_slc_kv': pr['bslc'].reshape(b, t, 2, HEAD_DIM),
        'b_win_kv': pr['bwin'].reshape(b, t, 2, HEAD_DIM)[:, t - keep:],
        'c_kv': pr['ckv'].reshape(b, t, 2, N_HEADS, HEAD_DIM),
        'c_logf': pr['logf'][:, :N_HEADS].reshape(b, t, N_HEADS),
        'd_kv': pr['dkv'].reshape(b, t, 2, N_HEADS, HEAD_DIM),
        'ffn_conv': conv.reshape(b, SUBLANES, -1)[:, SUBLANES - (CONV_W - 1):],
    }
    return y2.reshape(b, t, d), new, (o_a, o_b, o_c, o_d)


def _decode_tables(rel_bias, npast, pg):
    tab_a, tab_b = rel_bias[:, :N_HEADS], rel_bias[:, N_HEADS:]
    wstep = pg * PAGE
    tq = jnp.arange(TD)
    rows = lambda tab, dist: _bias_table(tab, dist).reshape(N_HEADS * TD, -1)
    far = lambda tab: jnp.repeat(tab[N_BUCKETS - 1], TD)[:, None]
    d_last = (npast + tq)[:, None] - (npast - wstep + jnp.arange(wstep))[None, :]
    d_new = tq[:, None] - jnp.arange(PAGE)[None, :]
    nch, nsb = npast // CMP_STRIDE, npast // SEL_BLOCK
    d_cmp = (npast + tq)[:, None] - (jnp.arange(nch) * CMP_STRIDE + (CMP_LEN - 1))[None, :]
    d_win = WINDOW + tq[:, None] - jnp.arange(WINDOW)[None, :]
    n = np.arange(nch)
    gmat = (n[:, None] // (SEL_BLOCK // CMP_STRIDE) == np.arange(nsb)[None, :]) & (n[:, None] < nch - 1)
    c = np.arange(PAGE)
    return {
        'a_last': rows(tab_a, d_last), 'a_far': far(tab_a), 'a_new': rows(tab_a, d_new),
        'b_last': rows(tab_b, d_last), 'b_far': far(tab_b), 'b_new': rows(tab_b, d_new),
        'cbias_d': rows(tab_b, d_cmp), 'wb_past': rows(tab_b, d_win),
        'gmat_d': jnp.asarray(gmat.astype(np.float32), BF),
        'umat': jnp.asarray((c[:, None] > c[None, :]).astype(np.float32), BF),
        'tinc': jnp.asarray((c[:, None] <= c[None, :]).astype(np.float32), BF),
    }


def _layer_decode(x, lw, tabs, l, caches, ffn_state, page_table, nstep, pg):
    s, td, d = x.shape
    assert td == TD
    m = s * td
    x2 = x.reshape(m, d)
    pr = _proj(x2, lw, tm=m, tiles_per_seq=1, with_cumsum=False)
    prs = {k: v.reshape(s, td, v.shape[-1]) for k, v in pr.items()}
    o_a = _dsa_decode(prs, caches['a_kidx'], caches['a_kv'], page_table, l, tabs, nstep, pg)
    ab = _compress_paged(caches['b_cmp'], page_table, l, lw, nstep, pg)
    o_b = _nsa_decode(prs, ab, caches['b_win'][l], caches['b_slc'], page_table, l, lw, tabs, nstep, pg)
    lft_new = jnp.pad(jnp.transpose(prs['logf'][:, :, :SUBLANES], (0, 2, 1)), ((0, 0), (0, 0), (0, PAGE - td)))
    o_c = _fox_decode(prs, lft_new, caches['c_kv'], caches['c_lft'], page_table, l, tabs, nstep, pg)
    o_d = _stick_decode(prs, caches['d_kv'], page_table, l, tabs, nstep, pg)
    flat = lambda o: o.reshape(m, MIX_W)
    xm = _merge(x2, flat(o_a), flat(o_b), flat(o_c), flat(o_d), lw, tm=m)
    st0 = jnp.repeat(ffn_state[:, 0], td, axis=0)
    st1 = jnp.repeat(ffn_state[:, 1], td, axis=0)
    y2, conv = _ffn(xm, lw, st0, st1, tm=m, carry_mode=False, tiles_per_seq=1)
    win_new = pr['bwin'].reshape(s, td, 2, HEAD_DIM)
    win_all = jnp.concatenate([caches['b_win'][l].reshape(s, -1, 2, HEAD_DIM), win_new], axis=1)
    keep = min(WINDOW, win_all.shape[1])
    new = {
        'a_kv': pr['akv'].reshape(s, td, 2, HEAD_DIM),
        'a_kidx': pr['amisc'][:, :IDX_DIM].reshape(s, td, IDX_DIM),
        'b_cmp_kv': pr['bcmp'].reshape(s, td, 2, HEAD_DIM),
        'b_slc_kv': pr['bslc'].reshape(s, td, 2, HEAD_DIM),
        'b_win_kv': win_all[:, win_all.shape[1] - keep:],
        'c_kv': pr['ckv'].reshape(s, td, 2, N_HEADS, HEAD_DIM),
        'c_logf': pr['logf'][:, :N_HEADS].reshape(s, td, N_HEADS),
        'd_kv': pr['dkv'].reshape(s, td, 2, N_HEADS, HEAD_DIM),
        'ffn_conv': conv.reshape(s, td, -1)[:, td - (CONV_W - 1):],
    }
    return y2.reshape(s, td, d), new, (o_a, o_b, o_c, o_d)


_STATE_KEYS = ('a_kv', 'a_kidx', 'b_cmp_kv', 'b_slc_kv', 'b_win_kv', 'c_kv', 'c_logf', 'd_kv', 'ffn_conv')


def kernel(x_prompt, x_sample, cache_a_kv, cache_a_kidx, cache_b_cmp_kv, cache_b_slc_kv, state_b_win_kv,
           cache_c_kv, cache_c_logf, cache_d_kv, state_ffn_conv, page_table, rel_bias, norm_attn, w_in,
           b_forget, qk_gain, cmp_w, cmp_pe, w_branch, w_out, norm_ffn, w_up, conv_w, conv_b, w_down):
    params = dict(norm_attn=norm_attn, w_in=w_in, b_forget=b_forget, qk_gain=qk_gain, cmp_w=cmp_w, cmp_pe=cmp_pe,
                  w_branch=w_branch, w_out=w_out, norm_ffn=norm_ffn, w_up=w_up, conv_w=conv_w, conv_b=conv_b,
                  w_down=w_down)
    depth, n_pool = w_in.shape[0], cache_a_kv.shape[1]
    n_seq, n_pages = page_table.shape
    assert cache_a_kv.shape[2] == PAGE
    pg = PAGES_PER_STEP if n_pages % PAGES_PER_STEP == 0 else 1
    nstep = n_pages // pg
    npast = n_pages * PAGE
    caches = {
        'a_kv': cache_a_kv.reshape(depth, n_pool, PAGE, 2 * HEAD_DIM).astype(BF),
        'a_kidx': cache_a_kidx,
        'b_cmp': cache_b_cmp_kv.reshape(depth, n_pool, PAGE // CMP_STRIDE, CMP_STRIDE * 2 * HEAD_DIM),
        'b_slc': cache_b_slc_kv.reshape(depth, n_pool, PAGE, 2 * HEAD_DIM).astype(BF),
        'b_win': state_b_win_kv.reshape(depth, n_seq, state_b_win_kv.shape[2], 2 * HEAD_DIM),
        'c_kv': cache_c_kv.reshape(depth, n_pool, PAGE, 2 * MIX_W).astype(BF),
        'c_lft': jnp.pad(jnp.transpose(cache_c_logf.astype(F32), (0, 1, 3, 2)),
                         ((0, 0), (0, 0), (0, SUBLANES - N_HEADS), (0, 0))),
        'd_kv': cache_d_kv.reshape(depth, n_pool, PAGE, 2 * MIX_W).astype(BF),
    }
    tabs_p = _prompt_tables(rel_bias, x_prompt.shape[1])
    tabs_d = _decode_tables(rel_bias, npast, pg)
    y_p, y_s = x_prompt, x_sample
    new_p = {k: [] for k in _STATE_KEYS}
    new_s = {k: [] for k in _STATE_KEYS}
    for l in range(depth):
        lw = _prep_layer(l, params, DENSE_TM)
        y_p, st, _ = _layer_prompt(y_p, lw, tabs_p)
        for k in _STATE_KEYS:
            new_p[k].append(st[k])
        y_s, st, _ = _layer_decode(y_s, lw, tabs_d, l, caches, state_ffn_conv[l], page_table, nstep, pg)
        for k in _STATE_KEYS:
            new_s[k].append(st[k])
    sp = {k: jnp.stack(v) for k, v in new_p.items()}
    ss = {k: jnp.stack(v) for k, v in new_s.items()}
    return (y_p, y_s,
            sp['a_kv'], ss['a_kv'], sp['a_kidx'], ss['a_kidx'],
            sp['b_cmp_kv'], ss['b_cmp_kv'], sp['b_slc_kv'], ss['b_slc_kv'],
            sp['b_win_kv'], ss['b_win_kv'], sp['c_kv'], ss['c_kv'],
            sp['c_logf'], ss['c_logf'], sp['d_kv'], ss['d_kv'],
            sp['ffn_conv'], ss['ffn_conv'])
```

```python
import functools
import math

import jax
import jax.numpy as jnp
import numpy as np
from jax import lax
from jax.experimental import pallas as pl
from jax.experimental.pallas import tpu as pltpu

HEAD_DIM = 64
N_HEADS = 4
MIX_W = N_HEADS * HEAD_DIM
IDX_HEADS = 8
IDX_DIM = 64
DSA_TOPK = 256
CMP_LEN = 32
CMP_STRIDE = 16
SEL_BLOCK = 64
SEL_SHIFT = 6
HEAD_SHIFT = 6
N_SEL = 16
WINDOW = 512
N_BUCKETS = 32
MAX_DIST = 128
CONV_W = 3
EPS = 1e-6
NEG = -1e30
FORCE = 1e4
QK_SCALE = HEAD_DIM ** -0.5

LANES = 128
SUBLANES = 8
VMEM_LIMIT = 56 * 1024 * 1024
INT_MIN = -2 ** 31

DENSE_TM = 512
TQ = 128
TKB = 128
PAGES_PER_STEP = 32

BF = jnp.bfloat16
F32 = jnp.float32

C_AQ, C_AKV, C_AQIDX, C_AMISC = 0, 256, 384, 896
C_BQ, C_BCMP, C_BSLC, C_BWIN, C_BGATE = 1024, 1280, 1408, 1536, 1664
C_CQ, C_CK, C_CV, C_CF = 1792, 2048, 2304, 2560
C_DQ, C_DKV = 2688, 2944
N_PROJ = 3456


def _cparams(sem):
    return pltpu.CompilerParams(dimension_semantics=sem, vmem_limit_bytes=VMEM_LIMIT)


def _dot(a, b):
    return jnp.dot(a, b, preferred_element_type=F32)


def _dot_nt(a, b):
    return lax.dot_general(a, b, (((1,), (1,)), ((), ())), preferred_element_type=F32)


def _split3(x):
    hi = x.astype(BF)
    r1 = x - hi.astype(F32)
    mid = r1.astype(BF)
    lo = (r1 - mid.astype(F32)).astype(BF)
    return hi, mid, lo


def _dot3(x, m):
    hi, mid, lo = _split3(x)
    return _dot(hi, m) + _dot(mid, m) + _dot(lo, m)


def _dot3_l(m, x):
    hi, mid, lo = _split3(x)
    return _dot(m, hi) + _dot(m, mid) + _dot(m, lo)


def _sortable(x):
    x = jnp.where(x == 0.0, 0.0, x)
    b = lax.bitcast_convert_type(x, jnp.int32)
    return b ^ ((b >> 31) & 0x7FFFFFFF)


def _log_sigmoid(v):
    return jnp.minimum(v, 0.0) - jnp.log1p(jnp.exp(-jnp.abs(v)))


def _rms_rows(x, g):
    ms = jnp.mean(x * x, axis=-1, keepdims=True)
    return (x * lax.rsqrt(ms + EPS)) * g


def _proj_kernel(x_ref, g_ref, w_ref, gain_ref, nmask_ref, bf_ref, bd_ref, tri_ref,
                 qa_o, akv_o, akv_b, qidx_o, amisc_o, kidx_b, qb_o, bcmp_o, bslc_o, bslc_b, bwin_o, bwin_b,
                 gates_o, qc_o, ckv_o, ckv_b, logf_o, call_o, qd_o, dkv_o, dkv_b, carry_scr,
                 *, tiles_per_seq, with_cumsum):
    h = _rms_rows(x_ref[...], g_ref[...]).astype(BF)

    def cols(c0, width):
        return _dot(h, w_ref[:, c0:c0 + width])

    def normed(c0, width):
        slab = cols(c0, width)
        ms = _dot((slab * slab).astype(BF), bd_ref[0:width, 0:width])
        scale = lax.rsqrt(ms + EPS) * gain_ref[:, c0:c0 + width]
        return slab * jnp.where(nmask_ref[:, c0:c0 + width] > 0.0, scale, 1.0)

    qa_o[...] = (normed(C_AQ, MIX_W) * QK_SCALE).astype(BF)
    akv = normed(C_AKV, 128)
    akv_o[...] = akv
    akv_b[...] = akv.astype(BF)
    qidx_o[...] = cols(C_AQIDX, IDX_HEADS * IDX_DIM).astype(BF)
    amisc = cols(C_AMISC, 128)
    amisc_o[...] = amisc
    kidx_b[...] = amisc[:, 0:IDX_DIM].astype(BF)
    qb_o[...] = (normed(C_BQ, MIX_W) * QK_SCALE).astype(BF)
    bcmp_o[...] = cols(C_BCMP, 128)
    bslc = normed(C_BSLC, 128)
    bslc_o[...] = bslc
    bslc_b[...] = bslc.astype(BF)
    bwin = normed(C_BWIN, 128)
    bwin_o[...] = bwin
    bwin_b[...] = bwin.astype(BF)
    gates_o[...] = jax.nn.sigmoid(cols(C_BGATE, 128))
    qc_o[...] = (normed(C_CQ, MIX_W) * QK_SCALE).astype(BF)
    ck = normed(C_CK, MIX_W)
    cv = cols(C_CV, MIX_W)
    ckv_o[:, 0:MIX_W] = ck
    ckv_o[:, MIX_W:2 * MIX_W] = cv
    ckv_b[:, 0:MIX_W] = ck.astype(BF)
    ckv_b[:, MIX_W:2 * MIX_W] = cv.astype(BF)
    logf = _log_sigmoid(cols(C_CF, 128) + bf_ref[...])
    logf_o[...] = logf
    if with_cumsum:
        t = pl.program_id(0)

        @pl.when(t % tiles_per_seq == 0)
        def _():
            carry_scr[...] = jnp.zeros_like(carry_scr)

        c = _dot3_l(tri_ref[...], logf) + carry_scr[0:1, :]
        call_o[...] = c
        carry_scr[...] = jnp.broadcast_to(c[-1:, :], carry_scr.shape)
    else:
        call_o[...] = logf
    qd_o[...] = (cols(C_DQ, MIX_W) * QK_SCALE).astype(BF)
    dkv = cols(C_DKV, 2 * MIX_W)
    dkv_o[...] = dkv
    dkv_b[...] = dkv.astype(BF)


def _proj(x2d, lw, *, tm, tiles_per_seq, with_cumsum):
    m, d = x2d.shape
    assert m % tm == 0
    row = lambda i: (i, 0)
    const = lambda i: (0, 0)

    def o(width, dtype):
        return jax.ShapeDtypeStruct((m, width), dtype), pl.BlockSpec((tm, width), row)

    outs = [o(MIX_W, BF), o(128, F32), o(128, BF), o(IDX_HEADS * IDX_DIM, BF), o(128, F32), o(IDX_DIM, BF),
            o(MIX_W, BF), o(128, F32), o(128, F32), o(128, BF), o(128, F32), o(128, BF),
            o(128, F32), o(MIX_W, BF), o(2 * MIX_W, F32), o(2 * MIX_W, BF), o(128, F32), o(128, F32),
            o(MIX_W, BF), o(2 * MIX_W, F32), o(2 * MIX_W, BF)]
    names = ('qa', 'akv', 'akv_b', 'qidx', 'amisc', 'kidx_b', 'qb', 'bcmp', 'bslc', 'bslc_b', 'bwin', 'bwin_b',
             'gates', 'qc', 'ckv', 'ckv_b', 'logf', 'call', 'qd', 'dkv', 'dkv_b')
    res = pl.pallas_call(
        functools.partial(_proj_kernel, tiles_per_seq=tiles_per_seq, with_cumsum=with_cumsum),
        grid=(m // tm,),
        in_specs=[pl.BlockSpec((tm, d), row), pl.BlockSpec((1, d), const), pl.BlockSpec((d, N_PROJ), const),
                  pl.BlockSpec((1, N_PROJ), const), pl.BlockSpec((1, N_PROJ), const), pl.BlockSpec((1, 128), const),
                  pl.BlockSpec((MIX_W, MIX_W), const), pl.BlockSpec((tm, tm), const)],
        out_specs=[s for _, s in outs],
        out_shape=[s for s, _ in outs],
        scratch_shapes=[pltpu.VMEM((SUBLANES, 128), F32)],
        compiler_params=_cparams(("arbitrary",)),
        name="proj",
    )(x2d, lw['g_attn'], lw['w_proj'], lw['gain'], lw['nmask'], lw['b_f'], lw['bd'], lw['tri'][:tm, :tm])
    return dict(zip(names, res))


def _merge_kernel(x_ref, g_ref, wg_ref, oa_ref, ob_ref, oc_ref, od_ref, wbr_ref, wo_ref, y_ref):
    x = x_ref[...]
    d = x.shape[1]
    h = _rms_rows(x, g_ref[...]).astype(BF)
    m = None
    for i, o_ref in enumerate((oa_ref, ob_ref, oc_ref, od_ref)):
        gate = jax.nn.sigmoid(_dot(h, wg_ref[:, i * d:(i + 1) * d]))
        term = gate * _dot(o_ref[...], wbr_ref[i])
        m = term if m is None else m + term
    y_ref[...] = x + _dot(m.astype(BF), wo_ref[...])


def _merge(x2d, o_a, o_b, o_c, o_d, lw, *, tm):
    m, d = x2d.shape
    row = lambda i: (i, 0)
    const = lambda i: (0, 0)
    return pl.pallas_call(
        _merge_kernel,
        grid=(m // tm,),
        in_specs=[pl.BlockSpec((tm, d), row), pl.BlockSpec((1, d), const), pl.BlockSpec((d, N_HEADS * d), const),
                  pl.BlockSpec((tm, MIX_W), row), pl.BlockSpec((tm, MIX_W), row), pl.BlockSpec((tm, MIX_W), row),
                  pl.BlockSpec((tm, MIX_W), row), pl.BlockSpec((4, MIX_W, d), lambda i: (0, 0, 0)),
                  pl.BlockSpec((d, d), const)],
        out_specs=pl.BlockSpec((tm, d), row),
        out_shape=jax.ShapeDtypeStruct((m, d), F32),
        compiler_params=_cparams(("arbitrary",)),
        name="merge",
    )(x2d, lw['g_attn'], lw['w_gate'], o_a, o_b, o_c, o_d, lw['w_br'], lw['w_o'])


def _ffn_kernel(x_ref, g_ref, wup_ref, cw_ref, cb_ref, wdn_ref, st0_ref, st1_ref, y_ref, conv_o, prev_scr,
                *, carry_mode, tiles_per_seq, d_ff, cw):
    x = x_ref[...]
    tm = x.shape[0]
    h = _rms_rows(x, g_ref[...]).astype(BF)
    row = lax.broadcasted_iota(jnp.int32, (tm, cw), 0)
    if carry_mode:
        t = pl.program_id(0)

        @pl.when(t % tiles_per_seq == 0)
        def _():
            prev_scr[...] = jnp.zeros_like(prev_scr)
    else:
        rowmod = row % SUBLANES

    def conv_cols(c0):
        up = _dot(h, wup_ref[:, c0:c0 + cw])
        r1 = pltpu.roll(up, 1, axis=0)
        r2 = pltpu.roll(up, 2, axis=0)
        if carry_mode:
            p6 = prev_scr[6:7, c0:c0 + cw]
            p7 = prev_scr[7:8, c0:c0 + cw]
            u1 = jnp.where(row == 0, p7, r1)
            u2 = jnp.where(row == 0, p6, jnp.where(row == 1, p7, r2))
            prev_scr[:, c0:c0 + cw] = up[tm - SUBLANES:tm, :]
            conv_o[:, c0:c0 + cw] = up[tm - SUBLANES:tm, :]
        else:
            s0 = st0_ref[:, c0:c0 + cw]
            s1 = st1_ref[:, c0:c0 + cw]
            u1 = jnp.where(rowmod == 0, s1, r1)
            u2 = jnp.where(rowmod == 0, s0, jnp.where(rowmod == 1, s1, r2))
            conv_o[:, c0:c0 + cw] = up
        conv = (u2 * cw_ref[0:1, c0:c0 + cw] + u1 * cw_ref[1:2, c0:c0 + cw]) + up * cw_ref[2:3, c0:c0 + cw]
        return cb_ref[:, c0:c0 + cw] + conv

    acc = None
    for c in range(d_ff // cw):
        val = conv_cols(c * cw)
        gate = conv_cols(d_ff + c * cw)
        act = (gate * jax.nn.sigmoid(gate)) * val
        part = _dot(act.astype(BF), wdn_ref[c * cw:(c + 1) * cw, :])
        acc = part if acc is None else acc + part
    y_ref[...] = x + acc


def _ffn(x2d, lw, st0, st1, *, tm, carry_mode, tiles_per_seq):
    m, d = x2d.shape
    d_ff = lw['w_down'].shape[0]
    cw = 256
    assert d_ff % cw == 0 and m % tm == 0
    row = lambda i: (i, 0)
    const = lambda i: (0, 0)
    if carry_mode:
        n_seq = m // (tm * tiles_per_seq)
        conv_shape = jax.ShapeDtypeStruct((n_seq * SUBLANES, 2 * d_ff), F32)
        conv_spec = pl.BlockSpec((SUBLANES, 2 * d_ff), lambda i: (i // tiles_per_seq, 0))
        st_spec = pl.BlockSpec((SUBLANES, 2 * d_ff), const)
    else:
        conv_shape = jax.ShapeDtypeStruct((m, 2 * d_ff), F32)
        conv_spec = pl.BlockSpec((tm, 2 * d_ff), row)
        st_spec = pl.BlockSpec((tm, 2 * d_ff), row)
    return pl.pallas_call(
        functools.partial(_ffn_kernel, carry_mode=carry_mode, tiles_per_seq=tiles_per_seq, d_ff=d_ff, cw=cw),
        grid=(m // tm,),
        in_specs=[pl.BlockSpec((tm, d), row), pl.BlockSpec((1, d), const), pl.BlockSpec((d, 2 * d_ff), const),
                  pl.BlockSpec((CONV_W, 2 * d_ff), const), pl.BlockSpec((1, 2 * d_ff), const),
                  pl.BlockSpec((d_ff, d), const), st_spec, st_spec],
        out_specs=[pl.BlockSpec((tm, d), row), conv_spec],
        out_shape=[jax.ShapeDtypeStruct((m, d), F32), conv_shape],
        scratch_shapes=[pltpu.VMEM((SUBLANES, 2 * d_ff), F32)],
        compiler_params=_cparams(("arbitrary",)),
        name="ffn",
    )(x2d, lw['g_ffn'], lw['w_up'], lw['conv_w'], lw['conv_b'], lw['w_down'], st0, st1)


def _osm_update(s, mask, m, l, acc, v, v_t=False):
    if mask is not None:
        s = jnp.where(mask, s, NEG)
    m_new = jnp.maximum(m, jnp.max(s, axis=1, keepdims=True))
    p = jnp.exp(s - m_new)
    if mask is not None:
        p = jnp.where(mask, p, 0.0)
    alpha = jnp.exp(m - m_new)
    l = alpha * l + jnp.sum(p, axis=1, keepdims=True)
    pv = _dot_nt(p.astype(BF), v) if v_t else _dot(p.astype(BF), v)
    return m_new, l, alpha * acc + pv


def _osm_ref_init(m_ref, l_ref, acc_ref):
    m_ref[...] = jnp.full(m_ref.shape, NEG, F32)
    l_ref[...] = jnp.zeros(l_ref.shape, F32)
    acc_ref[...] = jnp.zeros(acc_ref.shape, F32)


def _osm_ref_step(s, mask, v, m_ref, l_ref, acc_ref):
    reps = s.shape[1] // LANES
    if mask is not None:
        s = jnp.where(mask, s, NEG)
    m_prev = m_ref[...]
    m_new = jnp.maximum(m_prev, jnp.max(s, axis=1, keepdims=True))
    p = jnp.exp(s - jnp.tile(m_new, (1, reps)))
    if mask is not None:
        p = jnp.where(mask, p, 0.0)
    alpha = jnp.exp(m_prev - m_new)
    l_ref[...] = alpha * l_ref[...] + jnp.sum(p, axis=1, keepdims=True)
    d = acc_ref.shape[1]
    a = alpha[:, 0:d] if d <= LANES else jnp.tile(alpha, (1, d // LANES))
    acc_ref[...] = a * acc_ref[...] + _dot(p.astype(BF), v)
    m_ref[...] = m_new


def _key_chunk(t):
    return 4 * TKB if t % (4 * TKB) == 0 else TKB


def _tile_bias(bias_ref, i, first_tile, ntiles):
    parts = [bias_ref[jnp.clip(i - (first_tile + u), 0, 2)].reshape(N_HEADS * TQ, TKB) for u in range(ntiles)]
    return parts[0] if ntiles == 1 else jnp.concatenate(parts, axis=1)


def _pad_lanes(q):
    return jnp.concatenate([q, jnp.zeros_like(q)], axis=1)


def _heads_to_rows(q):
    return jnp.concatenate([q[:, h * HEAD_DIM:(h + 1) * HEAD_DIM] for h in range(N_HEADS)], axis=0)


def _rows_to_heads(o, r):
    return jnp.concatenate([o[h * r:(h + 1) * r, :] for h in range(N_HEADS)], axis=1)


def _tile_rows(a, n):
    return jnp.concatenate([a] * n, axis=0)


def _unsortable(key):
    return lax.bitcast_convert_type(key ^ ((key >> 31) & 0x7FFFFFFF), F32)


VALUE_BISECTIONS = 14
KEY_BISECTIONS = 34


def _kth_search(count_ge, k, vmin, vmax, n_valid):
    lo0 = _sortable(vmin)
    hi0 = _sortable(vmax) + 1

    def done_of(lo, hi, c_lo):
        return (c_lo <= k) | (hi == lo + 1)

    def step(st, midpoint):
        lo, hi, c_lo, c_hi = st
        done = done_of(lo, hi, c_lo)
        cand = jnp.where(done, lo, jnp.clip(midpoint(lo, hi), lo + 1, hi - 1))
        c = count_ge(cand)
        up = (c >= k) & jnp.logical_not(done)
        down = (c < k) & jnp.logical_not(done)
        return jnp.where(up, cand, lo), jnp.where(down, cand, hi), jnp.where(up, c, c_lo), jnp.where(down, c, c_hi)

    def value_mid(lo, hi):
        return _sortable(0.5 * _unsortable(lo) + 0.5 * _unsortable(hi - 1))

    def key_mid(lo, hi):
        return (lo >> 1) + (hi >> 1) + (lo & hi & 1)

    st = (lo0, hi0, n_valid, jnp.zeros_like(n_valid))
    st = lax.fori_loop(0, VALUE_BISECTIONS, lambda _, s: step(s, value_mid), st)

    def cond(carry):
        it, (lo, hi, c_lo, _) = carry
        pending = jnp.max(jnp.where(done_of(lo, hi, c_lo), 0.0, 1.0)) > 0.0
        return (it < KEY_BISECTIONS) & pending

    _, (lo, _, c_lo, c_hi) = lax.while_loop(cond, lambda c: (c[0] + 1, step(c[1], key_mid)), (jnp.int32(0), st))
    return lo, c_lo, c_hi


def _dsa_prompt_kernel(qidx_ref, amisc_ref, qa_ref, kidx_ref, akv_ref, bias_ref, o_ref, key_scr, m_scr, l_scr, acc_scr,
                       *, topk, kt):
    i = pl.program_id(1)
    chunk = kt // TKB
    nchunk = i // chunk + 1
    amisc = amisc_ref[...]
    qidx = qidx_ref[...]
    qh = [qidx[:, h * IDX_DIM:(h + 1) * IDX_DIM] for h in range(IDX_HEADS)]
    wb = [jnp.broadcast_to(amisc[:, IDX_DIM + h:IDX_DIM + h + 1], (TQ, LANES)) for h in range(IDX_HEADS)]
    qpos = i * TQ + lax.broadcasted_iota(jnp.int32, (TQ, kt), 0)
    kcol = lax.broadcasted_iota(jnp.int32, (TQ, kt), 1)

    def score_chunk(c, carry):
        vmax, vmin = carry
        off = pl.multiple_of(c * kt, kt)
        kb = kidx_ref[pl.ds(off, kt), :]
        acc = None
        for h in range(IDX_HEADS):
            term = jnp.maximum(_dot_nt(qh[h], kb), 0.0) * jnp.tile(wb[h], (1, chunk))
            acc = term if acc is None else acc + term
        visible = off + kcol <= qpos
        key_scr[:, pl.ds(off, kt)] = jnp.where(visible, _sortable(acc), INT_MIN)
        hi_part = jnp.where(visible, acc, -jnp.inf)
        lo_part = jnp.where(visible, acc, jnp.inf)
        for u in range(chunk):
            vmax = jnp.maximum(vmax, hi_part[:, u * TKB:(u + 1) * TKB])
            vmin = jnp.minimum(vmin, lo_part[:, u * TKB:(u + 1) * TKB])
        return vmax, vmin

    vmax, vmin = lax.fori_loop(0, nchunk, score_chunk,
                               (jnp.full((TQ, TKB), -jnp.inf, F32), jnp.full((TQ, TKB), jnp.inf, F32)))
    vmax = jnp.max(vmax, axis=1, keepdims=True)
    vmin = jnp.min(vmin, axis=1, keepdims=True)
    n_valid = i * TQ + lax.broadcasted_iota(jnp.int32, (TQ, 1), 0) + 1

    cw = kt

    def count(pred):
        def body(c, acc):
            kc = key_scr[:, pl.ds(pl.multiple_of(c * cw, cw), cw)]
            hit = jnp.where(pred(kc, c * cw), 1.0, 0.0)
            part = hit[:, 0:TKB]
            for u in range(1, chunk):
                part = part + hit[:, u * TKB:(u + 1) * TKB]
            return acc + part
        acc = lax.fori_loop(0, nchunk, body, jnp.zeros((TQ, TKB), F32))
        return jnp.sum(acc, axis=1, keepdims=True).astype(jnp.int32)

    thr, c_lo, c_hi = _kth_search(lambda cand: count(lambda kc, o: kc >= cand), topk, vmin, vmax, n_valid)

    need = topk - c_hi
    tie = c_lo > topk
    any_tie = jnp.max(jnp.where(tie, 1.0, 0.0)) > 0.0
    lane_c = lax.broadcasted_iota(jnp.int32, (TQ, cw), 1)

    @pl.when(any_tie)
    def _():
        nbits = max(1, int(math.ceil(math.log2(key_scr.shape[1]))))

        def bit_body(b, lo):
            cand = lo + jnp.left_shift(jnp.int32(1), nbits - 1 - b)
            cnt = count(lambda kc, o: (kc == thr) & ((lane_c + o) < cand))
            return jnp.where(cnt < need, cand, lo)

        jmax = lax.fori_loop(0, nbits, bit_body, jnp.zeros((TQ, 1), jnp.int32))

        def demote(c, carry):
            sl = pl.ds(pl.multiple_of(c * cw, cw), cw)
            kc = key_scr[:, sl]
            drop = tie & (kc == thr) & ((lane_c + c * cw) > jmax)
            key_scr[:, sl] = jnp.where(drop, thr - 1, kc)
            return carry

        lax.fori_loop(0, nchunk, demote, 0)

    thr_eff = jnp.maximum(thr, INT_MIN + 1)
    q4 = _pad_lanes(_heads_to_rows(qa_ref[...]))
    _osm_ref_init(m_scr, l_scr, acc_scr)

    def att_chunk(c, carry):
        off = pl.multiple_of(c * kt, kt)
        kv = akv_ref[pl.ds(off, kt), :]
        sel = key_scr[:, pl.ds(off, kt)] >= thr_eff
        s = _dot_nt(q4, kv) + _tile_bias(bias_ref, i, c * chunk, chunk)
        _osm_ref_step(s, _tile_rows(sel, N_HEADS), kv, m_scr, l_scr, acc_scr)
        return carry

    lax.fori_loop(0, nchunk, att_chunk, 0)
    o4 = acc_scr[...] / l_scr[...]
    o_ref[...] = _rows_to_heads(o4[:, HEAD_DIM:2 * HEAD_DIM], TQ).astype(BF)


def _dsa_prompt(pr, bias, b, t):
    nq = t // TQ
    topk = min(DSA_TOPK, t // 4)
    rows = N_HEADS * TQ
    q_spec = lambda w: pl.BlockSpec((TQ, w), lambda bi, qi: (bi * nq + qi, 0))
    kv_spec = lambda w: pl.BlockSpec((t, w), lambda bi, qi: (bi, 0))
    return pl.pallas_call(
        functools.partial(_dsa_prompt_kernel, topk=topk, kt=_key_chunk(t)),
        grid=(b, nq),
        in_specs=[q_spec(IDX_HEADS * IDX_DIM), q_spec(128), q_spec(MIX_W), kv_spec(IDX_DIM), kv_spec(128),
                  pl.BlockSpec((3, N_HEADS, TQ, TKB), lambda bi, qi: (0, 0, 0, 0))],
        out_specs=q_spec(MIX_W),
        out_shape=jax.ShapeDtypeStruct((b * t, MIX_W), BF),
        scratch_shapes=[pltpu.VMEM((TQ, t), jnp.int32), pltpu.VMEM((rows, LANES), F32),
                        pltpu.VMEM((rows, LANES), F32), pltpu.VMEM((rows, LANES), F32)],
        compiler_params=_cparams(("arbitrary", "arbitrary")),
        name="dsa_prompt",
    )(pr['qidx'], pr['amisc'], pr['qa'], pr['kidx_b'], pr['akv_b'], bias)


def _compress_kernel(*refs, n_in):
    x_refs, (pe_ref, w_ref, o_ref) = refs[:n_in], refs[n_in:]
    x = x_refs[0][...] if n_in == 1 else jnp.concatenate([r[...] for r in x_refs], axis=0)
    o_ref[:, 0:128] = _dot((x + pe_ref[0:1, :]).astype(BF), w_ref[0])
    o_ref[:, 128:256] = _dot((x + pe_ref[1:2, :]).astype(BF), w_ref[1])


def _compress_weights(lw):
    w4 = lw['cmp_w'].reshape(2, CMP_LEN, HEAD_DIM, HEAD_DIM)
    pe = lw['cmp_pe']

    def half(rs):
        wk, wv = w4[0, rs], w4[1, rs]
        z = jnp.zeros_like(wk)
        rows = jnp.stack([jnp.concatenate([wk, z], axis=-1), jnp.concatenate([z, wv], axis=-1)], axis=1)
        return rows.reshape(CMP_STRIDE * 2 * HEAD_DIM, 2 * HEAD_DIM)

    top, bot = slice(0, CMP_STRIDE), slice(CMP_STRIDE, CMP_LEN)
    w = jnp.stack([half(top), half(bot)]).astype(BF)
    pef = jnp.stack([jnp.transpose(pe[:, top], (1, 0, 2)).reshape(-1), jnp.transpose(pe[:, bot], (1, 0, 2)).reshape(-1)])
    return pef, w


def _compress_dense(chunks, lw, rows_per_step):
    n, width = chunks.shape
    pef, w = _compress_weights(lw)
    return pl.pallas_call(
        functools.partial(_compress_kernel, n_in=1),
        grid=(n // rows_per_step,),
        in_specs=[pl.BlockSpec((rows_per_step, width), lambda i: (i, 0)), pl.BlockSpec((2, width), lambda i: (0, 0)),
                  pl.BlockSpec((2, width, 128), lambda i: (0, 0, 0))],
        out_specs=pl.BlockSpec((rows_per_step, 256), lambda i: (i, 0)),
        out_shape=jax.ShapeDtypeStruct((n, 256), F32),
        compiler_params=_cparams(("arbitrary",)),
        name="compress_prompt",
    )(chunks, pef, w)


def _combine_compressed(ab, gk):
    n = ab.shape[0]
    kv = ab[:, 0:128] + pltpu.roll(ab[:, 128:256], n - 1, axis=0)
    ck_raw = kv[:, 0:HEAD_DIM]
    ms = jnp.mean(ck_raw * ck_raw, axis=-1, keepdims=True)
    ck = (ck_raw * lax.rsqrt(ms + EPS)) * gk
    return ck.astype(BF), kv[:, HEAD_DIM:2 * HEAD_DIM].astype(BF)


def _masked_softmax_rows(s, valid):
    s = jnp.where(valid, s, NEG)
    m = jnp.max(s, axis=1, keepdims=True)
    p = jnp.where(valid, jnp.exp(s - m), 0.0)
    l = jnp.sum(p, axis=1, keepdims=True)
    return p * (1.0 / jnp.where(l > 0.0, l, 1.0))


def _gate_cols(g, c):
    return jnp.concatenate([g[:, h * 3 + c:h * 3 + c + 1] for h in range(N_HEADS)], axis=0)


def _nsa_prompt_kernel(qb_ref, gates_ref, ab_ref, gk_ref, cbias_ref, gmat_ref, emat_ref, tie_ref, slc_ref, win_ref, bias_ref,
                       o_ref, ck_scr, cv_scr, selm_scr, m_scr, l_scr, acc_scr, *, n_sel, ns, nch, kt, wt, nq):
    i = pl.program_id(1)

    @pl.when(i == 0)
    def _():
        ck, cv = _combine_compressed(ab_ref[...], gk_ref[...])
        ck_scr[...] = ck
        cv_scr[...] = cv

    q4 = _heads_to_rows(qb_ref[...])
    qpos = i * TQ + lax.broadcasted_iota(jnp.int32, (TQ, 1), 0)

    cend = lax.broadcasted_iota(jnp.int32, (TQ, nch), 1) * CMP_STRIDE + (CMP_LEN - 1)
    cvalid = cend <= qpos
    s = _dot_nt(q4, ck_scr[...]) + cbias_ref[...].reshape(N_HEADS * TQ, nch)
    pc = _masked_softmax_rows(s, _tile_rows(cvalid, N_HEADS))
    o_c = _dot(pc.astype(BF), cv_scr[...])
    pcs = pc[0:TQ] + pc[TQ:2 * TQ] + pc[2 * TQ:3 * TQ] + pc[3 * TQ:4 * TQ]
    imp = _dot3(pcs, gmat_ref[...])

    j_io = lax.broadcasted_iota(jnp.int32, (TQ, ns), 1)
    cur = qpos >> SEL_SHIFT
    gap = cur - j_io
    score = jnp.where(gap >= 0, jnp.where((j_io == 0) | (gap <= 1), FORCE, imp), NEG)
    rank = jnp.zeros((TQ, ns), F32)
    for jj in range(ns):
        col = score[:, jj:jj + 1]
        rank = rank + (jnp.where(col > score, 1.0, 0.0) + jnp.where(col == score, tie_ref[jj:jj + 1, :], 0.0))
    sel = jnp.where(gap >= 0, rank, float(ns)) < n_sel
    selm_scr[...] = _dot(jnp.where(sel, 1.0, 0.0).astype(BF), emat_ref[...])

    q4p = _pad_lanes(q4)
    chunk = kt // TKB
    qrow = i * TQ + lax.broadcasted_iota(jnp.int32, (TQ, kt), 0)
    kcol = lax.broadcasted_iota(jnp.int32, (TQ, kt), 1)
    _osm_ref_init(m_scr, l_scr, acc_scr)

    def sel_chunk(c, carry):
        off = pl.multiple_of(c * kt, kt)
        kv = slc_ref[pl.ds(off, kt), :]
        mask = (selm_scr[:, pl.ds(off, kt)] > 0.5) & (off + kcol <= qrow)
        s = _dot_nt(q4p, kv) + _tile_bias(bias_ref, i, c * chunk, chunk)
        _osm_ref_step(s, _tile_rows(mask, N_HEADS), kv, m_scr, l_scr, acc_scr)
        return carry

    lax.fori_loop(0, i // chunk + 1, sel_chunk, 0)
    o_s = (acc_scr[...] / l_scr[...])[:, HEAD_DIM:2 * HEAD_DIM]

    w0 = jnp.clip(i - (wt - 1), 0, nq - wt)
    woff = pl.multiple_of(w0 * TKB, TKB)
    kv = win_ref[pl.ds(woff, wt * TKB), :]
    wd = (i * TQ + lax.broadcasted_iota(jnp.int32, (TQ, wt * TKB), 0)) - (
        woff + lax.broadcasted_iota(jnp.int32, (TQ, wt * TKB), 1))
    s = _dot_nt(q4p, kv) + _tile_bias(bias_ref, i, w0, wt)
    pw = _masked_softmax_rows(s, _tile_rows((wd >= 0) & (wd <= WINDOW), N_HEADS))
    o_w = _dot(pw.astype(BF), kv)[:, HEAD_DIM:2 * HEAD_DIM]

    g = gates_ref[...]
    o = _gate_cols(g, 0) * o_c + _gate_cols(g, 1) * o_s + _gate_cols(g, 2) * o_w
    o_ref[...] = _rows_to_heads(o, TQ).astype(BF)


def _nsa_prompt(pr, ab, lw, tabs, b, t):
    nq = t // TQ
    nch = t // CMP_STRIDE
    ns = t // SEL_BLOCK
    n_sel = min(N_SEL, ns)
    q_spec = lambda w: pl.BlockSpec((TQ, w), lambda bi, qi: (bi * nq + qi, 0))
    kv_spec = lambda w: pl.BlockSpec((t, w), lambda bi, qi: (bi, 0))
    const2 = lambda bi, qi: (0, 0)
    return pl.pallas_call(
        functools.partial(_nsa_prompt_kernel, n_sel=n_sel, ns=ns, nch=nch, kt=_key_chunk(t),
                          wt=min(WINDOW // TKB + 1, nq), nq=nq),
        grid=(b, nq),
        in_specs=[q_spec(MIX_W), q_spec(128), pl.BlockSpec((nch, 256), lambda bi, qi: (bi, 0)),
                  pl.BlockSpec((1, HEAD_DIM), const2),
                  pl.BlockSpec((None, N_HEADS, TQ, nch), lambda bi, qi: (qi, 0, 0, 0)),
                  pl.BlockSpec((nch, ns), const2), pl.BlockSpec((ns, t), const2), pl.BlockSpec((ns, ns), const2),
                  kv_spec(128), kv_spec(128),
                  pl.BlockSpec((3, N_HEADS, TQ, TKB), lambda bi, qi: (0, 0, 0, 0))],
        out_specs=q_spec(MIX_W),
        out_shape=jax.ShapeDtypeStruct((b * t, MIX_W), BF),
        scratch_shapes=[pltpu.VMEM((nch, HEAD_DIM), BF), pltpu.VMEM((nch, HEAD_DIM), BF), pltpu.VMEM((TQ, t), F32),
                        pltpu.VMEM((N_HEADS * TQ, LANES), F32), pltpu.VMEM((N_HEADS * TQ, LANES), F32),
                        pltpu.VMEM((N_HEADS * TQ, LANES), F32)],
        compiler_params=_cparams(("arbitrary", "arbitrary")),
        name="nsa_prompt",
    )(pr['qb'], pr['gates'], ab, lw['gk_b'], tabs['cbias_p'], tabs['gmat_p'], tabs['emat_p'], tabs['tie_p'],
      pr['bslc_b'], pr['bwin_b'], tabs['nsa_tiles'])


def _fox_prompt_kernel(q_ref, cq_ref, ckv_ref, ckt_ref, o_ref, m_scr, l_scr, acc_scr, *, kt):
    i = pl.program_id(1)
    chunk = kt // TKB
    nfull = i // chunk
    q4 = _head_block_diag(q_ref[...])
    cq = cq_ref[...]
    cq4 = jnp.concatenate([jnp.broadcast_to(cq[:, h:h + 1], (TQ, LANES)) for h in range(N_HEADS)], axis=0)
    qpos = i * TQ + lax.broadcasted_iota(jnp.int32, (TQ, kt), 0)
    kcol = lax.broadcasted_iota(jnp.int32, (TQ, kt), 1)
    _osm_ref_init(m_scr, l_scr, acc_scr)

    def do_chunk(c, masked):
        off = pl.multiple_of(c * kt, kt)
        kv = ckv_ref[pl.ds(off, kt), :]
        ck4 = jnp.concatenate([jnp.broadcast_to(ckt_ref[h:h + 1, pl.ds(off, kt)], (TQ, kt)) for h in range(N_HEADS)],
                              axis=0)
        s = (_dot_nt(q4, kv[:, 0:MIX_W]) + jnp.tile(cq4, (1, chunk))) - ck4
        mask = _tile_rows(off + kcol <= qpos, N_HEADS) if masked else None
        _osm_ref_step(s, mask, kv[:, MIX_W:2 * MIX_W], m_scr, l_scr, acc_scr)

    def body(c, carry):
        do_chunk(c, False)
        return carry

    lax.fori_loop(0, nfull, body, 0)
    do_chunk(nfull, True)
    o4 = acc_scr[...] / jnp.tile(l_scr[...], (1, MIX_W // LANES))
    o_ref[...] = _head_diag_pick(o4, TQ).astype(BF)


def _fox_prompt(pr, ckt, b, t):
    nq = t // TQ
    rows = N_HEADS * TQ
    q_spec = lambda w: pl.BlockSpec((TQ, w), lambda bi, qi: (bi * nq + qi, 0))
    return pl.pallas_call(
        functools.partial(_fox_prompt_kernel, kt=_key_chunk(t)),
        grid=(b, nq),
        in_specs=[q_spec(MIX_W), q_spec(128), pl.BlockSpec((t, 2 * MIX_W), lambda bi, qi: (bi, 0)),
                  pl.BlockSpec((SUBLANES, t), lambda bi, qi: (bi, 0))],
        out_specs=q_spec(MIX_W),
        out_shape=jax.ShapeDtypeStruct((b * t, MIX_W), BF),
        scratch_shapes=[pltpu.VMEM((rows, LANES), F32), pltpu.VMEM((rows, LANES), F32), pltpu.VMEM((rows, MIX_W), F32)],
        compiler_params=_cparams(("arbitrary", "arbitrary")),
        name="fox_prompt",
    )(pr['qc'], pr['call'], pr['ckv_b'], ckt)


def _stick_terms(z):
    e = jnp.log(1.0 + jnp.exp(-jnp.abs(z)))
    nsp = -(jnp.maximum(z, 0.0) + e)
    return nsp, z + nsp


def _suffix_sums(x, uaug, ntiles):
    r = x.shape[0]
    stack = x if ntiles == 1 else jnp.concatenate([x[:, u * TKB:(u + 1) * TKB] for u in range(ntiles)], axis=0)
    hi = stack.astype(BF)
    lo = (stack - hi.astype(F32)).astype(BF)
    rs = _dot(hi, uaug) + _dot(lo, uaug)
    return [(rs[u * r:(u + 1) * r, 0:TKB], rs[u * r:(u + 1) * r, TKB:2 * TKB]) for u in range(ntiles)]


def _stick_prompt_kernel(q_ref, dkv_ref, u_ref, o_ref, acc_scr, run_scr, *, kt):
    i = pl.program_id(1)
    chunk = kt // TKB
    last = i // chunk
    q4 = _head_block_diag(q_ref[...])
    uaug = u_ref[...]
    qpos = i * TQ + lax.broadcasted_iota(jnp.int32, (TQ, kt), 0)
    kcol = lax.broadcasted_iota(jnp.int32, (TQ, kt), 1)
    acc_scr[...] = jnp.zeros(acc_scr.shape, F32)
    run_scr[...] = jnp.zeros(run_scr.shape, F32)

    def do_chunk(c, masked):
        off = pl.multiple_of(c * kt, kt)
        kv = dkv_ref[pl.ds(off, kt), :]
        nsp, lsig = _stick_terms(_dot_nt(q4, kv[:, 0:MIX_W]))
        if masked:
            ok = _tile_rows(off + kcol < qpos, N_HEADS)
            nsp = jnp.where(ok, nsp, 0.0)
        sums = _suffix_sums(nsp, uaug, chunk)
        run = run_scr[...]
        pieces = [None] * chunk
        for u in reversed(range(chunk)):
            pieces[u] = sums[u][0] + run
            run = run + sums[u][1]
        run_scr[...] = run
        a = jnp.exp(lsig + (pieces[0] if chunk == 1 else jnp.concatenate(pieces, axis=1)))
        if masked:
            a = jnp.where(ok, a, 0.0)
        acc_scr[...] = acc_scr[...] + _dot(a.astype(BF), kv[:, MIX_W:2 * MIX_W])

    def body(jj, carry):
        @pl.when(jj == 0)
        def _():
            do_chunk(last, True)

        @pl.when(jj > 0)
        def _():
            do_chunk(last - jj, False)

        return carry

    lax.fori_loop(0, last + 1, body, 0)
    o_ref[...] = _head_diag_pick(acc_scr[...], TQ).astype(BF)


def _stick_prompt(pr, uaug, b, t):
    nq = t // TQ
    rows = N_HEADS * TQ
    q_spec = lambda w: pl.BlockSpec((TQ, w), lambda bi, qi: (bi * nq + qi, 0))
    return pl.pallas_call(
        functools.partial(_stick_prompt_kernel, kt=_key_chunk(t)),
        grid=(b, nq),
        in_specs=[q_spec(MIX_W), pl.BlockSpec((t, 2 * MIX_W), lambda bi, qi: (bi, 0)),
                  pl.BlockSpec((TKB, 2 * TKB), lambda bi, qi: (0, 0))],
        out_specs=q_spec(MIX_W),
        out_shape=jax.ShapeDtypeStruct((b * t, MIX_W), BF),
        scratch_shapes=[pltpu.VMEM((rows, MIX_W), F32), pltpu.VMEM((rows, LANES), F32)],
        compiler_params=_cparams(("arbitrary", "arbitrary")),
        name="stick_prompt",
    )(pr['qd'], pr['dkv_b'], uaug)


PAGE = 128
TD = SUBLANES


def _page_specs(page_shape, l, pg, page_of):
    tail = (0,) * len(page_shape)

    def spec(i):
        def index_map(*args):
            pt = args[-1]
            return (l, pt[args[0], page_of(*args[1:-1], i)]) + tail
        return pl.BlockSpec((None, None) + page_shape, index_map)

    return [spec(i) for i in range(pg)]


def _pad_rows(a, n):
    return jnp.concatenate([a, jnp.zeros((n - a.shape[0], a.shape[1]), a.dtype)], axis=0)


def _topk_select_ref(key_ref, k):
    rows, width = key_ref.shape

    def count(pred):
        return jnp.sum(jnp.where(pred(key_ref[...]), 1.0, 0.0), axis=1, keepdims=True).astype(jnp.int32)

    keys = key_ref[...]
    valid = keys > INT_MIN
    vals = _unsortable(keys)
    vmax = jnp.max(jnp.where(valid, vals, -jnp.inf), axis=1, keepdims=True)
    vmin = jnp.min(jnp.where(valid, vals, jnp.inf), axis=1, keepdims=True)
    thr, c_lo, c_hi = _kth_search(lambda cand: count(lambda kk: kk >= cand), k, vmin, vmax,
                                  count(lambda kk: kk > INT_MIN))
    need = k - c_hi
    tie = c_lo > k
    any_tie = jnp.max(jnp.where(tie, 1.0, 0.0)) > 0.0

    @pl.when(any_tie)
    def _():
        idx = lax.broadcasted_iota(jnp.int32, (rows, width), 1)
        nbits = max(1, int(math.ceil(math.log2(width))))

        def bit_body(b, lo):
            cand = lo + jnp.left_shift(jnp.int32(1), nbits - 1 - b)
            cnt = count(lambda kk: (kk == thr) & (idx < cand))
            return jnp.where(cnt < need, cand, lo)

        jmax = lax.fori_loop(0, nbits, bit_body, jnp.zeros((rows, 1), jnp.int32))
        kk = key_ref[...]
        key_ref[...] = jnp.where(tie & (kk == thr) & (idx > jmax), thr - 1, kk)

    return jnp.maximum(thr, INT_MIN + 1)


def _osm_scratch_update(s, mask, v, m_scr, l_scr, acc_scr, v_t=False):
    m, l, acc = _osm_update(s, mask, m_scr[...], l_scr[...], acc_scr[...], v, v_t)
    m_scr[...] = m
    l_scr[...] = l
    acc_scr[...] = acc


def _osm_scratch_init(m_scr, l_scr, acc_scr):
    m_scr[...] = jnp.full(m_scr.shape, NEG, F32)
    l_scr[...] = jnp.zeros(l_scr.shape, F32)
    acc_scr[...] = jnp.zeros(acc_scr.shape, F32)


def _head_block_diag(q):
    lane = lax.broadcasted_iota(jnp.int32, q.shape, 1)
    return jnp.concatenate([jnp.where(lane >> HEAD_SHIFT == h, q, jnp.zeros_like(q)) for h in range(N_HEADS)], axis=0)


def _head_diag_pick(o, r):
    lane = lax.broadcasted_iota(jnp.int32, (r, o.shape[1]), 1)
    out = None
    for h in range(N_HEADS):
        part = jnp.where(lane >> HEAD_SHIFT == h, o[h * r:(h + 1) * r, :], 0.0)
        out = part if out is None else out + part
    return out


def _causal_new_mask(strict):
    lane = lax.broadcasted_iota(jnp.int32, (TD, PAGE), 1)
    rowi = lax.broadcasted_iota(jnp.int32, (TD, PAGE), 0)
    return lane < rowi if strict else lane <= rowi


def _compress_paged(cache, page_table, l, lw, nstep, pg):
    s = page_table.shape[0]
    pef, w = _compress_weights(lw)
    width = pef.shape[1]
    rows = PAGE // CMP_STRIDE

    def body(pt_ref, *refs):
        _compress_kernel(*refs, n_in=pg)

    return pl.pallas_call(
        body,
        grid_spec=pltpu.PrefetchScalarGridSpec(
            num_scalar_prefetch=1, grid=(s, nstep),
            in_specs=_page_specs((rows, width), l, pg, lambda p, i: p * pg + i)
            + [pl.BlockSpec((2, width), lambda si, p, pt: (0, 0)), pl.BlockSpec((2, width, 128), lambda si, p, pt: (0, 0, 0))],
            out_specs=pl.BlockSpec((None, pg * rows, 256), lambda si, p, pt: (si, p, 0))),
        out_shape=jax.ShapeDtypeStruct((s, nstep * pg * rows, 256), F32),
        compiler_params=_cparams(("arbitrary", "arbitrary")),
        name="compress_decode",
    )(page_table, *([cache] * pg), pef, w)


def _dsa_decode_kernel(pt_ref, qidx_ref, amisc_ref, qa_ref, kidxn_ref, akvn_ref, blast_ref, bfar_ref, bnew_ref, *rest,
                       topk, nstep, pg):
    kid_refs, kv_refs = rest[:pg], rest[pg:2 * pg]
    o_ref, key_scr, thr_scr, m_scr, l_scr, acc_scr = rest[2 * pg:]
    ph, p = pl.program_id(1), pl.program_id(2)
    wstep = pg * PAGE
    npast = nstep * wstep
    last = p == nstep - 1

    @pl.when(ph == 0)
    def _():
        qidx = qidx_ref[...]
        w = amisc_ref[:, IDX_DIM:IDX_DIM + IDX_HEADS]
        q64 = jnp.concatenate([qidx[:, h * IDX_DIM:(h + 1) * IDX_DIM] for h in range(IDX_HEADS)], axis=0)

        def scores(sc):
            acc = None
            for h in range(IDX_HEADS):
                term = jnp.maximum(sc[h * TD:(h + 1) * TD, :], 0.0) * w[:, h:h + 1]
                acc = term if acc is None else acc + term
            return acc

        kid_t = jnp.concatenate([r[...].astype(BF) for r in kid_refs], axis=1)
        key_scr[:, pl.ds(pl.multiple_of(p * wstep, wstep), wstep)] = _sortable(scores(_dot(q64, kid_t)))

        @pl.when(last)
        def _():
            acc = scores(_dot_nt(q64, _pad_rows(kidxn_ref[...], PAGE)))
            key_scr[:, npast:npast + PAGE] = jnp.where(_causal_new_mask(False), _sortable(acc), INT_MIN)
            thr_scr[...] = jnp.broadcast_to(_topk_select_ref(key_scr, topk), thr_scr.shape)

    @pl.when(ph == 1)
    def _():
        @pl.when(p == 0)
        def _():
            _osm_scratch_init(m_scr, l_scr, acc_scr)

        thr = thr_scr[:, 0:1]
        q4 = _heads_to_rows(qa_ref[...])
        k_t = jnp.concatenate([r[0:HEAD_DIM, :].astype(BF) for r in kv_refs], axis=1)
        v_t = jnp.concatenate([r[HEAD_DIM:2 * HEAD_DIM, :].astype(BF) for r in kv_refs], axis=1)
        bias = jnp.where(last, blast_ref[...], bfar_ref[...])
        s = _dot(q4, k_t) + bias
        sel = key_scr[:, pl.ds(pl.multiple_of(p * wstep, wstep), wstep)] >= thr
        _osm_scratch_update(s, _tile_rows(sel, N_HEADS), v_t, m_scr, l_scr, acc_scr, v_t=True)

        @pl.when(last)
        def _():
            kvn = _pad_rows(akvn_ref[...], PAGE)
            s = _dot_nt(q4, kvn[:, 0:HEAD_DIM]) + bnew_ref[...]
            sel = key_scr[:, npast:npast + PAGE] >= thr
            m, l, acc = _osm_update(s, _tile_rows(sel, N_HEADS), m_scr[...], l_scr[...], acc_scr[...],
                                    kvn[:, HEAD_DIM:2 * HEAD_DIM])
            o_ref[...] = _rows_to_heads(acc / l, TD).astype(BF)


def _seq_spec(width, ngrid):
    if ngrid == 2:
        return pl.BlockSpec((None, TD, width), lambda si, p, pt: (si, 0, 0))
    return pl.BlockSpec((None, TD, width), lambda si, ph, p, pt: (si, 0, 0))


def _const_spec(shape, ngrid):
    zeros = (0,) * len(shape)
    if ngrid == 2:
        return pl.BlockSpec(shape, lambda si, p, pt: zeros)
    return pl.BlockSpec(shape, lambda si, ph, p, pt: zeros)


def _dsa_decode(prs, cache_kidx, cache_akv, page_table, l, tabs, nstep, pg):
    s = page_table.shape[0]
    npast = nstep * pg * PAGE
    topk = min(DSA_TOPK, (npast + TD) // 4)
    wstep = pg * PAGE
    kid_specs = _page_specs((IDX_DIM, PAGE), l, pg, lambda ph, p, i: jnp.where(ph == 0, p, nstep - 1) * pg + i)
    kv_specs = _page_specs((2 * HEAD_DIM, PAGE), l, pg, lambda ph, p, i: jnp.where(ph == 0, 0, p) * pg + i)
    return pl.pallas_call(
        functools.partial(_dsa_decode_kernel, topk=topk, nstep=nstep, pg=pg),
        grid_spec=pltpu.PrefetchScalarGridSpec(
            num_scalar_prefetch=1, grid=(s, 2, nstep),
            in_specs=[_seq_spec(IDX_HEADS * IDX_DIM, 3), _seq_spec(128, 3), _seq_spec(MIX_W, 3), _seq_spec(IDX_DIM, 3),
                      _seq_spec(128, 3), _const_spec((N_HEADS * TD, wstep), 3), _const_spec((N_HEADS * TD, 1), 3),
                      _const_spec((N_HEADS * TD, PAGE), 3)] + kid_specs + kv_specs,
            out_specs=_seq_spec(MIX_W, 3),
            scratch_shapes=[pltpu.VMEM((TD, npast + PAGE), jnp.int32), pltpu.VMEM((TD, 128), jnp.int32),
                            pltpu.VMEM((N_HEADS * TD, 1), F32), pltpu.VMEM((N_HEADS * TD, 1), F32),
                            pltpu.VMEM((N_HEADS * TD, HEAD_DIM), F32)]),
        out_shape=jax.ShapeDtypeStruct((s, TD, MIX_W), BF),
        compiler_params=_cparams(("arbitrary", "arbitrary", "arbitrary")),
        name="dsa_decode",
    )(page_table, prs['qidx'], prs['amisc'], prs['qa'], prs['kidx_b'], prs['akv_b'],
      tabs['a_last'], tabs['a_far'], tabs['a_new'], *([cache_kidx] * pg), *([cache_akv] * pg))


def _nsa_decode_kernel(pt_ref, qb_ref, gates_ref, ab_ref, gk_ref, cbias_ref, gmat_ref, slcn_ref, winp_ref, winn_ref,
                       blast_ref, bfar_ref, bnew_ref, wbp_ref, *rest, n_sel, nstep, pg):
    slc_refs = rest[:pg]
    o_ref, selm_scr, bkey_scr, oc_scr, m_scr, l_scr, acc_scr = rest[pg:]
    p = pl.program_id(1)
    wstep = pg * PAGE
    npast = nstep * wstep
    nch = npast // CMP_STRIDE
    nsb = npast // SEL_BLOCK
    last = p == nstep - 1
    q4 = _heads_to_rows(qb_ref[...])

    @pl.when(p == 0)
    def _():
        ck, cv = _combine_compressed(ab_ref[...], gk_ref[...])
        qpos = npast + lax.broadcasted_iota(jnp.int32, (TD, 1), 0)
        cend = lax.broadcasted_iota(jnp.int32, (TD, nch), 1) * CMP_STRIDE + (CMP_LEN - 1)
        s = _dot_nt(q4, ck) + cbias_ref[...]
        pc = _masked_softmax_rows(s, _tile_rows(cend <= qpos, N_HEADS))
        oc_scr[...] = _dot(pc.astype(BF), cv)
        pcs = pc[0:TD] + pc[TD:2 * TD] + pc[2 * TD:3 * TD] + pc[3 * TD:4 * TD]
        imp = _dot3(pcs, gmat_ref[...])
        j_io = lax.broadcasted_iota(jnp.int32, (TD, nsb), 1)
        forced = (j_io == 0) | (j_io == nsb - 1)
        bkey_scr[...] = _sortable(jnp.where(forced, FORCE, imp))
        thr = _topk_select_ref(bkey_scr, n_sel - 1)
        selb = jnp.where(bkey_scr[...] >= thr, 1.0, 0.0).astype(BF)
        blk = lax.broadcasted_iota(jnp.int32, (nsb, wstep), 0)
        col = lax.broadcasted_iota(jnp.int32, (nsb, wstep), 1)
        for c in range(nstep):
            expand = jnp.where(blk == (c * wstep + col) >> SEL_SHIFT, 1.0, 0.0).astype(BF)
            selm_scr[:, c * wstep:(c + 1) * wstep] = _dot(selb, expand)
        _osm_scratch_init(m_scr, l_scr, acc_scr)

    k_t = jnp.concatenate([r[0:HEAD_DIM, :].astype(BF) for r in slc_refs], axis=1)
    v_t = jnp.concatenate([r[HEAD_DIM:2 * HEAD_DIM, :].astype(BF) for r in slc_refs], axis=1)
    bias = jnp.where(last, blast_ref[...], bfar_ref[...])
    s = _dot(q4, k_t) + bias
    mask = selm_scr[:, pl.ds(pl.multiple_of(p * wstep, wstep), wstep)] > 0.5
    _osm_scratch_update(s, _tile_rows(mask, N_HEADS), v_t, m_scr, l_scr, acc_scr, v_t=True)

    @pl.when(last)
    def _():
        lo, hi = slice(0, HEAD_DIM), slice(HEAD_DIM, 2 * HEAD_DIM)
        new_mask = _tile_rows(_causal_new_mask(False), N_HEADS)
        kvn = _pad_rows(slcn_ref[...], PAGE)
        s = _dot_nt(q4, kvn[:, lo]) + bnew_ref[...]
        _, l_s, acc_s = _osm_update(s, new_mask, m_scr[...], l_scr[...], acc_scr[...], kvn[:, hi])
        wp = winp_ref[...].astype(BF)
        nw = wp.shape[1]
        wd = nw + lax.broadcasted_iota(jnp.int32, (TD, nw), 0) - lax.broadcasted_iota(jnp.int32, (TD, nw), 1)
        s = _dot(q4, wp[lo, :]) + wbp_ref[...]
        carry = _osm_update(s, _tile_rows(wd <= WINDOW, N_HEADS), jnp.full((N_HEADS * TD, 1), NEG, F32),
                            jnp.zeros((N_HEADS * TD, 1), F32), jnp.zeros((N_HEADS * TD, HEAD_DIM), F32), wp[hi, :],
                            v_t=True)
        kvw = _pad_rows(winn_ref[...], PAGE)
        s = _dot_nt(q4, kvw[:, lo]) + bnew_ref[...]
        _, l_w, acc_w = _osm_update(s, new_mask, *carry, kvw[:, hi])
        g = gates_ref[...]
        o = _gate_cols(g, 0) * oc_scr[...] + _gate_cols(g, 1) * (acc_s / l_s) + _gate_cols(g, 2) * (acc_w / l_w)
        o_ref[...] = _rows_to_heads(o, TD).astype(BF)


def _nsa_decode(prs, ab, win_past, cache_slc, page_table, l, lw, tabs, nstep, pg):
    s = page_table.shape[0]
    wstep = pg * PAGE
    npast = nstep * wstep
    nch, nsb = npast // CMP_STRIDE, npast // SEL_BLOCK
    n_sel = min(N_SEL, nsb + 1)
    assert n_sel >= 2 and win_past.shape[2] == WINDOW
    return pl.pallas_call(
        functools.partial(_nsa_decode_kernel, n_sel=n_sel, nstep=nstep, pg=pg),
        grid_spec=pltpu.PrefetchScalarGridSpec(
            num_scalar_prefetch=1, grid=(s, nstep),
            in_specs=[_seq_spec(MIX_W, 2), _seq_spec(128, 2),
                      pl.BlockSpec((None, nch, 256), lambda si, p, pt: (si, 0, 0)), _const_spec((1, HEAD_DIM), 2),
                      _const_spec((N_HEADS * TD, nch), 2), _const_spec((nch, nsb), 2), _seq_spec(128, 2),
                      pl.BlockSpec((None, 2 * HEAD_DIM, WINDOW), lambda si, p, pt: (si, 0, 0)), _seq_spec(128, 2),
                      _const_spec((N_HEADS * TD, wstep), 2), _const_spec((N_HEADS * TD, 1), 2),
                      _const_spec((N_HEADS * TD, PAGE), 2), _const_spec((N_HEADS * TD, WINDOW), 2)]
            + _page_specs((2 * HEAD_DIM, PAGE), l, pg, lambda p, i: p * pg + i),
            out_specs=_seq_spec(MIX_W, 2),
            scratch_shapes=[pltpu.VMEM((TD, npast), F32), pltpu.VMEM((TD, nsb), jnp.int32),
                            pltpu.VMEM((N_HEADS * TD, HEAD_DIM), F32), pltpu.VMEM((N_HEADS * TD, 1), F32),
                            pltpu.VMEM((N_HEADS * TD, 1), F32), pltpu.VMEM((N_HEADS * TD, HEAD_DIM), F32)]),
        out_shape=jax.ShapeDtypeStruct((s, TD, MIX_W), BF),
        compiler_params=_cparams(("arbitrary", "arbitrary")),
        name="nsa_decode",
    )(page_table, prs['qb'], prs['gates'], ab, lw['gk_b'], tabs['cbias_d'], tabs['gmat_d'], prs['bslc_b'], win_past,
      prs['bwin_b'], tabs['b_last'], tabs['b_far'], tabs['b_new'], tabs['wb_past'], *([cache_slc] * pg))


def _fox_decode_kernel(pt_ref, q_ref, lfn_ref, lftn_ref, ckvn_ref, u_ref, tinc_ref, *rest, nstep, pg):
    kv_refs, lf_refs = rest[:pg], rest[pg:2 * pg]
    o_ref, qbd_scr, cq_scr, m_scr, l_scr, acc_scr, carry_scr = rest[2 * pg:]
    p = pl.program_id(1)

    @pl.when(p == 0)
    def _():
        qbd = _head_block_diag(q_ref[...])
        qbd_scr[...] = qbd
        lf = lfn_ref[...]
        rows = [lf[0:1, :]]
        for r in range(1, TD):
            rows.append(rows[-1] + lf[r:r + 1, :])
        npf = jnp.concatenate(rows, axis=0)
        cq4 = jnp.concatenate([npf[:, h:h + 1] for h in range(N_HEADS)], axis=0)
        cq_scr[...] = cq4
        npt = _dot3(lftn_ref[...], tinc_ref[...])
        ck4 = jnp.concatenate([jnp.broadcast_to(npt[h:h + 1, :], (TD, PAGE)) for h in range(N_HEADS)], axis=0)
        kvn = _pad_rows(ckvn_ref[...], PAGE)
        s = (_dot_nt(qbd, kvn[:, 0:MIX_W]) + cq4) - ck4
        _osm_scratch_init(m_scr, l_scr, acc_scr)
        _osm_scratch_update(s, _tile_rows(_causal_new_mask(False), N_HEADS), kvn[:, MIX_W:2 * MIX_W], m_scr, l_scr, acc_scr)
        carry_scr[...] = jnp.zeros_like(carry_scr)

    x = jnp.concatenate([r[...] for r in lf_refs], axis=0)
    rs = _dot3(x, u_ref[...])
    tot = jnp.sum(x, axis=1, keepdims=True)
    off = carry_scr[:, 0:1]
    pieces = [None] * pg
    for i in reversed(range(pg)):
        s_i = rs[i * SUBLANES:(i + 1) * SUBLANES, :] + off
        pieces[i] = jnp.concatenate([jnp.broadcast_to(s_i[h:h + 1, :], (TD, PAGE)) for h in range(N_HEADS)], axis=0)
        off = off + tot[i * SUBLANES:(i + 1) * SUBLANES, :]
    carry_scr[...] = jnp.broadcast_to(off, carry_scr.shape)
    k_t = jnp.concatenate([r[0:MIX_W, :].astype(BF) for r in kv_refs], axis=1)
    v_t = jnp.concatenate([r[MIX_W:2 * MIX_W, :].astype(BF) for r in kv_refs], axis=1)
    s = (_dot(qbd_scr[...], k_t) + cq_scr[...]) + jnp.concatenate(pieces, axis=1)
    _osm_scratch_update(s, None, v_t, m_scr, l_scr, acc_scr, v_t=True)

    @pl.when(p == nstep - 1)
    def _():
        o_ref[...] = _head_diag_pick(acc_scr[...] / l_scr[...], TD).astype(BF)


def _fox_decode(prs, lft_new, cache_ckv, cache_lft, page_table, l, tabs, nstep, pg):
    s = page_table.shape[0]
    rev = lambda p, i: (nstep - 1 - p) * pg + i
    return pl.pallas_call(
        functools.partial(_fox_decode_kernel, nstep=nstep, pg=pg),
        grid_spec=pltpu.PrefetchScalarGridSpec(
            num_scalar_prefetch=1, grid=(s, nstep),
            in_specs=[_seq_spec(MIX_W, 2), _seq_spec(128, 2), _seq_spec(128, 2), _seq_spec(2 * MIX_W, 2),
                      _const_spec((PAGE, PAGE), 2), _const_spec((PAGE, PAGE), 2)]
            + _page_specs((2 * MIX_W, PAGE), l, pg, rev) + _page_specs((SUBLANES, PAGE), l, pg, rev),
            out_specs=_seq_spec(MIX_W, 2),
            scratch_shapes=[pltpu.VMEM((N_HEADS * TD, MIX_W), BF), pltpu.VMEM((N_HEADS * TD, 1), F32),
                            pltpu.VMEM((N_HEADS * TD, 1), F32), pltpu.VMEM((N_HEADS * TD, 1), F32),
                            pltpu.VMEM((N_HEADS * TD, MIX_W), F32), pltpu.VMEM((SUBLANES, 128), F32)]),
        out_shape=jax.ShapeDtypeStruct((s, TD, MIX_W), BF),
        compiler_params=_cparams(("arbitrary", "arbitrary")),
        name="fox_decode",
    )(page_table, prs['qc'], prs['logf'], lft_new, prs['ckv_b'], tabs['umat'], tabs['tinc'],
      *([cache_ckv] * pg), *([cache_lft] * pg))


def _stick_decode_kernel(pt_ref, q_ref, dkvn_ref, u_ref, *rest, nstep, pg):
    kv_refs = rest[:pg]
    o_ref, qbd_scr, acc_scr, run_scr = rest[pg:]
    p = pl.program_id(1)
    umat = u_ref[...]
    rows = N_HEADS * TD

    @pl.when(p == 0)
    def _():
        qbd = _head_block_diag(q_ref[...])
        qbd_scr[...] = qbd
        kvn = _pad_rows(dkvn_ref[...], PAGE)
        strict = _tile_rows(_causal_new_mask(True), N_HEADS)
        nsp, lsig = _stick_terms(_dot_nt(qbd, kvn[:, 0:MIX_W]))
        nsp = jnp.where(strict, nsp, 0.0)
        a = jnp.where(strict, jnp.exp(lsig + _dot3(nsp, umat)), 0.0)
        acc_scr[...] = _dot(a.astype(BF), kvn[:, MIX_W:2 * MIX_W])
        run_scr[...] = jnp.sum(nsp, axis=1, keepdims=True)

    k_t = jnp.concatenate([r[0:MIX_W, :].astype(BF) for r in kv_refs], axis=1)
    v_t = jnp.concatenate([r[MIX_W:2 * MIX_W, :].astype(BF) for r in kv_refs], axis=1)
    nsp, lsig = _stick_terms(_dot(qbd_scr[...], k_t))
    stack = jnp.concatenate([nsp[:, i * PAGE:(i + 1) * PAGE] for i in range(pg)], axis=0)
    rs = _dot3(stack, umat)
    tot = jnp.sum(stack, axis=1, keepdims=True)
    off = run_scr[...]
    pieces = [None] * pg
    for i in reversed(range(pg)):
        pieces[i] = rs[i * rows:(i + 1) * rows, :] + off
        off = off + tot[i * rows:(i + 1) * rows, :]
    run_scr[...] = off
    a = jnp.exp(lsig + jnp.concatenate(pieces, axis=1))
    acc_scr[...] = acc_scr[...] + _dot_nt(a.astype(BF), v_t)

    @pl.when(p == nstep - 1)
    def _():
        o_ref[...] = _head_diag_pick(acc_scr[...], TD).astype(BF)


def _stick_decode(prs, cache_dkv, page_table, l, tabs, nstep, pg):
    s = page_table.shape[0]
    return pl.pallas_call(
        functools.partial(_stick_decode_kernel, nstep=nstep, pg=pg),
        grid_spec=pltpu.PrefetchScalarGridSpec(
            num_scalar_prefetch=1, grid=(s, nstep),
            in_specs=[_seq_spec(MIX_W, 2), _seq_spec(2 * MIX_W, 2), _const_spec((PAGE, PAGE), 2)]
            + _page_specs((2 * MIX_W, PAGE), l, pg, lambda p, i: (nstep - 1 - p) * pg + i),
            out_specs=_seq_spec(MIX_W, 2),
            scratch_shapes=[pltpu.VMEM((N_HEADS * TD, MIX_W), BF), pltpu.VMEM((N_HEADS * TD, MIX_W), F32),
                            pltpu.VMEM((N_HEADS * TD, 1), F32)]),
        out_shape=jax.ShapeDtypeStruct((s, TD, MIX_W), BF),
        compiler_params=_cparams(("arbitrary", "arbitrary")),
        name="stick_decode",
    )(page_table, prs['qd'], prs['dkv_b'], tabs['umat'], *([cache_dkv] * pg))


_IN_A, _IN_B, _IN_C, _IN_D = 968, 1620, 2392, 3160


def _bucket(dist):
    n = jnp.maximum(dist, 0)
    exact = N_BUCKETS // 2
    nf = jnp.maximum(n, 1).astype(F32)
    large = exact + (jnp.log(nf / exact) / math.log(MAX_DIST / exact) * (N_BUCKETS - exact)).astype(jnp.int32)
    return jnp.where(n < exact, n, jnp.minimum(large, N_BUCKETS - 1))


def _bias_table(tab, dist):
    b = _bucket(dist)
    out = jnp.zeros((tab.shape[1],) + dist.shape, F32)
    for j in range(N_BUCKETS):
        out = jnp.where(b[None] == j, tab[j].reshape((-1,) + (1,) * dist.ndim), out)
    return out


def _prep_layer(l, p, tm_max):
    w_in = p['w_in'][l]
    d = w_in.shape[0]
    z = lambda n: jnp.zeros((d, n), w_in.dtype)
    w_proj = jnp.concatenate([w_in[:, 0:_IN_A], z(1024 - _IN_A), w_in[:, _IN_A:_IN_B], z(768 - (_IN_B - _IN_A)),
                              w_in[:, _IN_B:_IN_C], z(896 - (_IN_C - _IN_B)), w_in[:, _IN_C:_IN_D]], axis=1)
    assert w_proj.shape[1] == N_PROJ
    qk = p['qk_gain'][l]
    gain = jnp.ones((N_PROJ,), F32)
    nmask = jnp.zeros((N_PROJ,), F32)
    for c0, g, rep in ((C_AQ, qk[0, 0], 4), (C_AKV, qk[0, 1], 1), (C_BQ, qk[1, 0], 4), (C_BSLC, qk[1, 1], 1),
                       (C_BWIN, qk[1, 1], 1), (C_CQ, qk[2, 0], 4), (C_CK, qk[2, 1], 4)):
        gain = gain.at[c0:c0 + rep * HEAD_DIM].set(jnp.tile(g, rep))
        nmask = nmask.at[c0:c0 + rep * HEAD_DIM].set(1.0)
    gidx = np.arange(MIX_W) // HEAD_DIM
    bd = jnp.asarray((gidx[:, None] == gidx[None, :]).astype(np.float32) / HEAD_DIM, BF)
    tri = jnp.asarray(np.tril(np.ones((tm_max, tm_max), np.float32)), BF)
    return {
        'g_attn': p['norm_attn'][l][None, :], 'w_proj': w_proj.astype(BF), 'gain': gain[None, :],
        'nmask': nmask[None, :], 'b_f': jnp.zeros((1, 128), F32).at[0, :N_HEADS].set(p['b_forget'][l]),
        'bd': bd, 'tri': tri,
        'w_gate': w_in[:, _IN_D:].astype(BF), 'w_br': p['w_branch'][l].astype(BF), 'w_o': p['w_out'][l].astype(BF),
        'g_ffn': p['norm_ffn'][l][None, :], 'w_up': p['w_up'][l].astype(BF), 'conv_w': p['conv_w'][l],
        'conv_b': p['conv_b'][l][None, :], 'w_down': p['w_down'][l].astype(BF),
        'gk_b': qk[1, 1][None, :], 'cmp_w': p['cmp_w'][l], 'cmp_pe': p['cmp_pe'][l],
    }


def _toeplitz_bias(tab):
    r = jnp.arange(TQ)[:, None]
    c = jnp.arange(TKB)[None, :]
    return jnp.stack([_bias_table(tab, dd * TKB + r - c) for dd in range(3)])


def _prompt_tables(rel_bias, t):
    tab_a, tab_b = rel_bias[:, :N_HEADS], rel_bias[:, N_HEADS:]
    nq, nch, ns = t // TQ, t // CMP_STRIDE, t // SEL_BLOCK
    qpos = jnp.arange(t).reshape(nq, TQ)
    cend = jnp.arange(nch) * CMP_STRIDE + (CMP_LEN - 1)
    cbias = jnp.transpose(_bias_table(tab_b, qpos[:, :, None] - cend[None, None, :]), (1, 0, 2, 3))
    n = np.arange(nch)
    gmat = ((n[:, None] // (SEL_BLOCK // CMP_STRIDE) == np.arange(ns)[None, :]) & (n[:, None] < nch - 1))
    emat = np.arange(t)[None, :] // SEL_BLOCK == np.arange(ns)[:, None]
    c = np.arange(TKB)
    return {
        'dsa_tiles': _toeplitz_bias(tab_a), 'nsa_tiles': _toeplitz_bias(tab_b), 'cbias_p': cbias,
        'gmat_p': jnp.asarray(gmat.astype(np.float32), BF), 'emat_p': jnp.asarray(emat.astype(np.float32), BF),
        'tie_p': jnp.asarray((np.arange(ns)[None, :] > np.arange(ns)[:, None]).astype(np.float32)),
        'uaug': jnp.asarray(np.concatenate([(c[:, None] > c[None, :]), np.ones((TKB, TKB), bool)], axis=1)
                            .astype(np.float32), BF),
    }


def _layer_prompt(x, lw, tabs):
    b, t, d = x.shape
    assert t % TQ == 0
    tm = next(c for c in (DENSE_TM, 256, TQ) if t % c == 0)
    x2 = x.reshape(b * t, d)
    pr = _proj(x2, lw, tm=tm, tiles_per_seq=t // tm, with_cumsum=True)
    o_a = _dsa_prompt(pr, tabs['dsa_tiles'], b, t)
    ab = _compress_dense(pr['bcmp'].reshape(b * t // CMP_STRIDE, CMP_STRIDE * 2 * HEAD_DIM), lw, t // CMP_STRIDE)
    o_b = _nsa_prompt(pr, ab, lw, tabs, b, t)
    ckt = jnp.transpose(pr['call'].reshape(b, t, 128)[:, :, :SUBLANES], (0, 2, 1)).reshape(b * SUBLANES, t)
    o_c = _fox_prompt(pr, ckt, b, t)
    o_d = _stick_prompt(pr, tabs['uaug'], b, t)
    xm = _merge(x2, o_a, o_b, o_c, o_d, lw, tm=tm)
    dummy = jnp.zeros((SUBLANES, lw['w_up'].shape[1]), F32)
    y2, conv = _ffn(xm, lw, dummy, dummy, tm=tm, carry_mode=True, tiles_per_seq=t // tm)
    keep = min(WINDOW, t)
    new = {
        'a_kv': pr['akv'].reshape(b, t, 2, HEAD_DIM),
        'a_kidx': pr['amisc'][:, :IDX_DIM].reshape(b, t, IDX_DIM),
        'b_cmp_kv': pr['bcmp'].reshape(b, t, 2, HEAD_DIM),
        'b_slc_kv': pr['bslc'].reshape(b, t, 2, HEAD_DIM),
        'b_win_kv': pr['bwin'].reshape(b, t, 2, HEAD_DIM)[:, t - keep:],
        'c_kv': pr['ckv'].reshape(b, t, 2, N_HEADS, HEAD_DIM),
        'c_logf': pr['logf'][:, :N_HEADS].reshape(b, t, N_HEADS),
        'd_kv': pr['dkv'].reshape(b, t, 2, N_HEADS, HEAD_DIM),
        'ffn_conv': conv.reshape(b, SUBLANES, -1)[:, SUBLANES - (CONV_W - 1):],
    }
    return y2.reshape(b, t, d), new, (o_a, o_b, o_c, o_d)


def _decode_tables(rel_bias, npast, pg):
    tab_a, tab_b = rel_bias[:, :N_HEADS], rel_bias[:, N_HEADS:]
    wstep = pg * PAGE
    tq = jnp.arange(TD)
    rows = lambda tab, dist: _bias_table(tab, dist).reshape(N_HEADS * TD, -1)
    far = lambda tab: jnp.repeat(tab[N_BUCKETS - 1], TD)[:, None]
    d_last = (npast + tq)[:, None] - (npast - wstep + jnp.arange(wstep))[None, :]
    d_new = tq[:, None] - jnp.arange(PAGE)[None, :]
    nch, nsb = npast // CMP_STRIDE, npast // SEL_BLOCK
    d_cmp = (npast + tq)[:, None] - (jnp.arange(nch) * CMP_STRIDE + (CMP_LEN - 1))[None, :]
    d_win = WINDOW + tq[:, None] - jnp.arange(WINDOW)[None, :]
    n = np.arange(nch)
    gmat = (n[:, None] // (SEL_BLOCK // CMP_STRIDE) == np.arange(nsb)[None, :]) & (n[:, None] < nch - 1)
    c = np.arange(PAGE)
    return {
        'a_last': rows(tab_a, d_last), 'a_far': far(tab_a), 'a_new': rows(tab_a, d_new),
        'b_last': rows(tab_b, d_last), 'b_far': far(tab_b), 'b_new': rows(tab_b, d_new),
        'cbias_d': rows(tab_b, d_cmp), 'wb_past': rows(tab_b, d_win),
        'gmat_d': jnp.asarray(gmat.astype(np.float32), BF),
        'umat': jnp.asarray((c[:, None] > c[None, :]).astype(np.float32), BF),
        'tinc': jnp.asarray((c[:, None] <= c[None, :]).astype(np.float32), BF),
    }


def _layer_decode(x, lw, tabs, l, caches, ffn_state, page_table, nstep, pg):
    s, td, d = x.shape
    assert td == TD
    m = s * td
    x2 = x.reshape(m, d)
    pr = _proj(x2, lw, tm=m, tiles_per_seq=1, with_cumsum=False)
    prs = {k: v.reshape(s, td, v.shape[-1]) for k, v in pr.items()}
    o_a = _dsa_decode(prs, caches['a_kidx'], caches['a_kv'], page_table, l, tabs, nstep, pg)
    ab = _compress_paged(caches['b_cmp'], page_table, l, lw, nstep, pg)
    o_b = _nsa_decode(prs, ab, caches['b_win_t'][l], caches['b_slc'], page_table, l, lw, tabs, nstep, pg)
    lft_new = jnp.pad(jnp.transpose(prs['logf'][:, :, :SUBLANES], (0, 2, 1)), ((0, 0), (0, 0), (0, PAGE - td)))
    o_c = _fox_decode(prs, lft_new, caches['c_kv'], caches['c_lft'], page_table, l, tabs, nstep, pg)
    o_d = _stick_decode(prs, caches['d_kv'], page_table, l, tabs, nstep, pg)
    flat = lambda o: o.reshape(m, MIX_W)
    xm = _merge(x2, flat(o_a), flat(o_b), flat(o_c), flat(o_d), lw, tm=m)
    st0 = jnp.repeat(ffn_state[:, 0], td, axis=0)
    st1 = jnp.repeat(ffn_state[:, 1], td, axis=0)
    y2, conv = _ffn(xm, lw, st0, st1, tm=m, carry_mode=False, tiles_per_seq=1)
    win_new = pr['bwin'].reshape(s, td, 2, HEAD_DIM)
    win_all = jnp.concatenate([caches['b_win'][l], win_new], axis=1)
    keep = min(WINDOW, win_all.shape[1])
    new = {
        'a_kv': pr['akv'].reshape(s, td, 2, HEAD_DIM),
        'a_kidx': pr['amisc'][:, :IDX_DIM].reshape(s, td, IDX_DIM),
        'b_cmp_kv': pr['bcmp'].reshape(s, td, 2, HEAD_DIM),
        'b_slc_kv': pr['bslc'].reshape(s, td, 2, HEAD_DIM),
        'b_win_kv': win_all[:, win_all.shape[1] - keep:],
        'c_kv': pr['ckv'].reshape(s, td, 2, N_HEADS, HEAD_DIM),
        'c_logf': pr['logf'][:, :N_HEADS].reshape(s, td, N_HEADS),
        'd_kv': pr['dkv'].reshape(s, td, 2, N_HEADS, HEAD_DIM),
        'ffn_conv': conv.reshape(s, td, -1)[:, td - (CONV_W - 1):],
    }
    return y2.reshape(s, td, d), new, (o_a, o_b, o_c, o_d)


_STATE_KEYS = ('a_kv', 'a_kidx', 'b_cmp_kv', 'b_slc_kv', 'b_win_kv', 'c_kv', 'c_logf', 'd_kv', 'ffn_conv')


def kernel(x_prompt, x_sample, cache_a_kv, cache_a_kidx, cache_b_cmp_kv, cache_b_slc_kv, state_b_win_kv,
           cache_c_kv, cache_c_logf, cache_d_kv, state_ffn_conv, page_table, rel_bias, norm_attn, w_in,
           b_forget, qk_gain, cmp_w, cmp_pe, w_branch, w_out, norm_ffn, w_up, conv_w, conv_b, w_down):
    params = dict(norm_attn=norm_attn, w_in=w_in, b_forget=b_forget, qk_gain=qk_gain, cmp_w=cmp_w, cmp_pe=cmp_pe,
                  w_branch=w_branch, w_out=w_out, norm_ffn=norm_ffn, w_up=w_up, conv_w=conv_w, conv_b=conv_b,
                  w_down=w_down)
    depth, n_pool = w_in.shape[0], cache_a_kv.shape[1]
    n_seq, n_pages = page_table.shape
    assert cache_a_kv.shape[2] == PAGE
    pg = PAGES_PER_STEP if n_pages % PAGES_PER_STEP == 0 else 1
    nstep = n_pages // pg
    npast = n_pages * PAGE
    def token_last(a):
        nd = a.ndim
        return jnp.transpose(a, (0, 1) + tuple(range(3, nd)) + (2,)).reshape(a.shape[0], a.shape[1], -1, a.shape[2])

    caches = {
        'a_kv': token_last(cache_a_kv), 'a_kidx': token_last(cache_a_kidx),
        'b_cmp': cache_b_cmp_kv.reshape(depth, n_pool, PAGE // CMP_STRIDE, CMP_STRIDE * 2 * HEAD_DIM),
        'b_slc': token_last(cache_b_slc_kv), 'b_win': state_b_win_kv, 'b_win_t': token_last(state_b_win_kv),
        'c_kv': token_last(cache_c_kv), 'd_kv': token_last(cache_d_kv),
        'c_lft': jnp.pad(token_last(cache_c_logf.astype(F32)), ((0, 0), (0, 0), (0, SUBLANES - N_HEADS), (0, 0))),
    }
    tabs_p = _prompt_tables(rel_bias, x_prompt.shape[1])
    tabs_d = _decode_tables(rel_bias, npast, pg)
    y_p, y_s = x_prompt, x_sample
    new_p = {k: [] for k in _STATE_KEYS}
    new_s = {k: [] for k in _STATE_KEYS}
    for l in range(depth):
        lw = _prep_layer(l, params, DENSE_TM)
        y_p, st, _ = _layer_prompt(y_p, lw, tabs_p)
        for k in _STATE_KEYS:
            new_p[k].append(st[k])
        y_s, st, _ = _layer_decode(y_s, lw, tabs_d, l, caches, state_ffn_conv[l], page_table, nstep, pg)
        for k in _STATE_KEYS:
            new_s[k].append(st[k])
    sp = {k: jnp.stack(v) for k, v in new_p.items()}
    ss = {k: jnp.stack(v) for k, v in new_s.items()}
    return (y_p, y_s,
            sp['a_kv'], ss['a_kv'], sp['a_kidx'], ss['a_kidx'],
            sp['b_cmp_kv'], ss['b_cmp_kv'], sp['b_slc_kv'], ss['b_slc_kv'],
            sp['b_win_kv'], ss['b_win_kv'], sp['c_kv'], ss['c_kv'],
            sp['c_logf'], ss['c_logf'], sp['d_kv'], ss['d_kv'],
            sp['ffn_conv'], ss['ffn_conv'])
```

```python
import functools
import math

import jax
import jax.numpy as jnp
import numpy as np
from jax import lax
from jax.experimental import pallas as pl
from jax.experimental.pallas import tpu as pltpu

HEAD_DIM = 64
N_HEADS = 4
MIX_W = N_HEADS * HEAD_DIM
IDX_HEADS = 8
IDX_DIM = 64
DSA_TOPK = 256
CMP_LEN = 32
CMP_STRIDE = 16
SEL_BLOCK = 64
SEL_SHIFT = 6
HEAD_SHIFT = 6
N_SEL = 16
WINDOW = 512
N_BUCKETS = 32
MAX_DIST = 128
CONV_W = 3
EPS = 1e-6
NEG = -1e30
FORCE = 1e4
QK_SCALE = HEAD_DIM ** -0.5

LANES = 128
SUBLANES = 8
VMEM_LIMIT = 56 * 1024 * 1024
INT_MIN = -2 ** 31

DENSE_TM = 512
TQ = 128
TKB = 128
PAGES_PER_STEP = 32

BF = jnp.bfloat16
F32 = jnp.float32

C_AQ, C_AKV, C_AQIDX, C_AMISC = 0, 256, 384, 896
C_BQ, C_BCMP, C_BSLC, C_BWIN, C_BGATE = 1024, 1280, 1408, 1536, 1664
C_CQ, C_CK, C_CV, C_CF = 1792, 2048, 2304, 2560
C_DQ, C_DKV = 2688, 2944
N_PROJ = 3456


def _cparams(sem):
    return pltpu.CompilerParams(dimension_semantics=sem, vmem_limit_bytes=VMEM_LIMIT)


def _dot(a, b):
    return jnp.dot(a, b, preferred_element_type=F32)


def _dot_nt(a, b):
    return lax.dot_general(a, b, (((1,), (1,)), ((), ())), preferred_element_type=F32)


def _split3(x):
    hi = x.astype(BF)
    r1 = x - hi.astype(F32)
    mid = r1.astype(BF)
    lo = (r1 - mid.astype(F32)).astype(BF)
    return hi, mid, lo


def _dot3(x, m):
    hi, mid, lo = _split3(x)
    return _dot(hi, m) + _dot(mid, m) + _dot(lo, m)


def _dot3_l(m, x):
    hi, mid, lo = _split3(x)
    return _dot(m, hi) + _dot(m, mid) + _dot(m, lo)


def _sortable(x):
    x = jnp.where(x == 0.0, 0.0, x)
    b = lax.bitcast_convert_type(x, jnp.int32)
    return b ^ ((b >> 31) & 0x7FFFFFFF)


def _log_sigmoid(v):
    return jnp.minimum(v, 0.0) - jnp.log1p(jnp.exp(-jnp.abs(v)))


def _rms_rows(x, g):
    ms = jnp.mean(x * x, axis=-1, keepdims=True)
    return (x * lax.rsqrt(ms + EPS)) * g


def _proj_kernel(x_ref, g_ref, w_ref, gain_ref, nmask_ref, bf_ref, bd_ref, tri_ref,
                 qa_o, akv_o, akv_b, qidx_o, amisc_o, kidx_b, qb_o, bcmp_o, bslc_o, bslc_b, bwin_o, bwin_b,
                 gates_o, qc_o, ckv_o, ckv_b, logf_o, call_o, qd_o, dkv_o, dkv_b, carry_scr,
                 *, tiles_per_seq, with_cumsum):
    h = _rms_rows(x_ref[...], g_ref[...]).astype(BF)

    def cols(c0, width):
        return _dot(h, w_ref[:, c0:c0 + width])

    def normed(c0, width):
        slab = cols(c0, width)
        ms = _dot((slab * slab).astype(BF), bd_ref[0:width, 0:width])
        scale = lax.rsqrt(ms + EPS) * gain_ref[:, c0:c0 + width]
        return slab * jnp.where(nmask_ref[:, c0:c0 + width] > 0.0, scale, 1.0)

    qa_o[...] = (normed(C_AQ, MIX_W) * QK_SCALE).astype(BF)
    akv = normed(C_AKV, 128)
    akv_o[...] = akv
    akv_b[...] = akv.astype(BF)
    qidx_o[...] = cols(C_AQIDX, IDX_HEADS * IDX_DIM).astype(BF)
    amisc = cols(C_AMISC, 128)
    amisc_o[...] = amisc
    kidx_b[...] = amisc[:, 0:IDX_DIM].astype(BF)
    qb_o[...] = (normed(C_BQ, MIX_W) * QK_SCALE).astype(BF)
    bcmp_o[...] = cols(C_BCMP, 128)
    bslc = normed(C_BSLC, 128)
    bslc_o[...] = bslc
    bslc_b[...] = bslc.astype(BF)
    bwin = normed(C_BWIN, 128)
    bwin_o[...] = bwin
    bwin_b[...] = bwin.astype(BF)
    gates_o[...] = jax.nn.sigmoid(cols(C_BGATE, 128))
    qc_o[...] = (normed(C_CQ, MIX_W) * QK_SCALE).astype(BF)
    ck = normed(C_CK, MIX_W)
    cv = cols(C_CV, MIX_W)
    ckv_o[:, 0:MIX_W] = ck
    ckv_o[:, MIX_W:2 * MIX_W] = cv
    ckv_b[:, 0:MIX_W] = ck.astype(BF)
    ckv_b[:, MIX_W:2 * MIX_W] = cv.astype(BF)
    logf = _log_sigmoid(cols(C_CF, 128) + bf_ref[...])
    logf_o[...] = logf
    if with_cumsum:
        t = pl.program_id(0)

        @pl.when(t % tiles_per_seq == 0)
        def _():
            carry_scr[...] = jnp.zeros_like(carry_scr)

        c = _dot3_l(tri_ref[...], logf) + carry_scr[0:1, :]
        call_o[...] = c
        carry_scr[...] = jnp.broadcast_to(c[-1:, :], carry_scr.shape)
    else:
        call_o[...] = logf
    qd_o[...] = (cols(C_DQ, MIX_W) * QK_SCALE).astype(BF)
    dkv = cols(C_DKV, 2 * MIX_W)
    dkv_o[...] = dkv
    dkv_b[...] = dkv.astype(BF)


def _proj(x2d, lw, *, tm, tiles_per_seq, with_cumsum):
    m, d = x2d.shape
    assert m % tm == 0
    row = lambda i: (i, 0)
    const = lambda i: (0, 0)

    def o(width, dtype):
        return jax.ShapeDtypeStruct((m, width), dtype), pl.BlockSpec((tm, width), row)

    outs = [o(MIX_W, BF), o(128, F32), o(128, BF), o(IDX_HEADS * IDX_DIM, BF), o(128, F32), o(IDX_DIM, BF),
            o(MIX_W, BF), o(128, F32), o(128, F32), o(128, BF), o(128, F32), o(128, BF),
            o(128, F32), o(MIX_W, BF), o(2 * MIX_W, F32), o(2 * MIX_W, BF), o(128, F32), o(128, F32),
            o(MIX_W, BF), o(2 * MIX_W, F32), o(2 * MIX_W, BF)]
    names = ('qa', 'akv', 'akv_b', 'qidx', 'amisc', 'kidx_b', 'qb', 'bcmp', 'bslc', 'bslc_b', 'bwin', 'bwin_b',
             'gates', 'qc', 'ckv', 'ckv_b', 'logf', 'call', 'qd', 'dkv', 'dkv_b')
    res = pl.pallas_call(
        functools.partial(_proj_kernel, tiles_per_seq=tiles_per_seq, with_cumsum=with_cumsum),
        grid=(m // tm,),
        in_specs=[pl.BlockSpec((tm, d), row), pl.BlockSpec((1, d), const), pl.BlockSpec((d, N_PROJ), const),
                  pl.BlockSpec((1, N_PROJ), const), pl.BlockSpec((1, N_PROJ), const), pl.BlockSpec((1, 128), const),
                  pl.BlockSpec((MIX_W, MIX_W), const), pl.BlockSpec((tm, tm), const)],
        out_specs=[s for _, s in outs],
        out_shape=[s for s, _ in outs],
        scratch_shapes=[pltpu.VMEM((SUBLANES, 128), F32)],
        compiler_params=_cparams(("arbitrary",)),
        name="proj",
    )(x2d, lw['g_attn'], lw['w_proj'], lw['gain'], lw['nmask'], lw['b_f'], lw['bd'], lw['tri'][:tm, :tm])
    return dict(zip(names, res))


def _merge_kernel(x_ref, g_ref, wg_ref, oa_ref, ob_ref, oc_ref, od_ref, wbr_ref, wo_ref, y_ref):
    x = x_ref[...]
    d = x.shape[1]
    h = _rms_rows(x, g_ref[...]).astype(BF)
    m = None
    for i, o_ref in enumerate((oa_ref, ob_ref, oc_ref, od_ref)):
        gate = jax.nn.sigmoid(_dot(h, wg_ref[:, i * d:(i + 1) * d]))
        term = gate * _dot(o_ref[...], wbr_ref[i])
        m = term if m is None else m + term
    y_ref[...] = x + _dot(m.astype(BF), wo_ref[...])


def _merge(x2d, o_a, o_b, o_c, o_d, lw, *, tm):
    m, d = x2d.shape
    row = lambda i: (i, 0)
    const = lambda i: (0, 0)
    return pl.pallas_call(
        _merge_kernel,
        grid=(m // tm,),
        in_specs=[pl.BlockSpec((tm, d), row), pl.BlockSpec((1, d), const), pl.BlockSpec((d, N_HEADS * d), const),
                  pl.BlockSpec((tm, MIX_W), row), pl.BlockSpec((tm, MIX_W), row), pl.BlockSpec((tm, MIX_W), row),
                  pl.BlockSpec((tm, MIX_W), row), pl.BlockSpec((4, MIX_W, d), lambda i: (0, 0, 0)),
                  pl.BlockSpec((d, d), const)],
        out_specs=pl.BlockSpec((tm, d), row),
        out_shape=jax.ShapeDtypeStruct((m, d), F32),
        compiler_params=_cparams(("arbitrary",)),
        name="merge",
    )(x2d, lw['g_attn'], lw['w_gate'], o_a, o_b, o_c, o_d, lw['w_br'], lw['w_o'])


def _ffn_kernel(x_ref, g_ref, wup_ref, cw_ref, cb_ref, wdn_ref, st0_ref, st1_ref, y_ref, conv_o, prev_scr,
                *, carry_mode, tiles_per_seq, d_ff, cw):
    x = x_ref[...]
    tm = x.shape[0]
    h = _rms_rows(x, g_ref[...]).astype(BF)
    row = lax.broadcasted_iota(jnp.int32, (tm, cw), 0)
    if carry_mode:
        t = pl.program_id(0)

        @pl.when(t % tiles_per_seq == 0)
        def _():
            prev_scr[...] = jnp.zeros_like(prev_scr)
    else:
        rowmod = row % SUBLANES

    def conv_cols(c0):
        up = _dot(h, wup_ref[:, c0:c0 + cw])
        r1 = pltpu.roll(up, 1, axis=0)
        r2 = pltpu.roll(up, 2, axis=0)
        if carry_mode:
            p6 = prev_scr[6:7, c0:c0 + cw]
            p7 = prev_scr[7:8, c0:c0 + cw]
            u1 = jnp.where(row == 0, p7, r1)
            u2 = jnp.where(row == 0, p6, jnp.where(row == 1, p7, r2))
            prev_scr[:, c0:c0 + cw] = up[tm - SUBLANES:tm, :]
            conv_o[:, c0:c0 + cw] = up[tm - SUBLANES:tm, :]
        else:
            s0 = st0_ref[:, c0:c0 + cw]
            s1 = st1_ref[:, c0:c0 + cw]
            u1 = jnp.where(rowmod == 0, s1, r1)
            u2 = jnp.where(rowmod == 0, s0, jnp.where(rowmod == 1, s1, r2))
            conv_o[:, c0:c0 + cw] = up
        conv = (u2 * cw_ref[0:1, c0:c0 + cw] + u1 * cw_ref[1:2, c0:c0 + cw]) + up * cw_ref[2:3, c0:c0 + cw]
        return cb_ref[:, c0:c0 + cw] + conv

    acc = None
    for c in range(d_ff // cw):
        val = conv_cols(c * cw)
        gate = conv_cols(d_ff + c * cw)
        act = (gate * jax.nn.sigmoid(gate)) * val
        part = _dot(act.astype(BF), wdn_ref[c * cw:(c + 1) * cw, :])
        acc = part if acc is None else acc + part
    y_ref[...] = x + acc


def _ffn(x2d, lw, st0, st1, *, tm, carry_mode, tiles_per_seq):
    m, d = x2d.shape
    d_ff = lw['w_down'].shape[0]
    cw = 256
    assert d_ff % cw == 0 and m % tm == 0
    row = lambda i: (i, 0)
    const = lambda i: (0, 0)
    if carry_mode:
        n_seq = m // (tm * tiles_per_seq)
        conv_shape = jax.ShapeDtypeStruct((n_seq * SUBLANES, 2 * d_ff), F32)
        conv_spec = pl.BlockSpec((SUBLANES, 2 * d_ff), lambda i: (i // tiles_per_seq, 0))
        st_spec = pl.BlockSpec((SUBLANES, 2 * d_ff), const)
    else:
        conv_shape = jax.ShapeDtypeStruct((m, 2 * d_ff), F32)
        conv_spec = pl.BlockSpec((tm, 2 * d_ff), row)
        st_spec = pl.BlockSpec((tm, 2 * d_ff), row)
    return pl.pallas_call(
        functools.partial(_ffn_kernel, carry_mode=carry_mode, tiles_per_seq=tiles_per_seq, d_ff=d_ff, cw=cw),
        grid=(m // tm,),
        in_specs=[pl.BlockSpec((tm, d), row), pl.BlockSpec((1, d), const), pl.BlockSpec((d, 2 * d_ff), const),
                  pl.BlockSpec((CONV_W, 2 * d_ff), const), pl.BlockSpec((1, 2 * d_ff), const),
                  pl.BlockSpec((d_ff, d), const), st_spec, st_spec],
        out_specs=[pl.BlockSpec((tm, d), row), conv_spec],
        out_shape=[jax.ShapeDtypeStruct((m, d), F32), conv_shape],
        scratch_shapes=[pltpu.VMEM((SUBLANES, 2 * d_ff), F32)],
        compiler_params=_cparams(("arbitrary",)),
        name="ffn",
    )(x2d, lw['g_ffn'], lw['w_up'], lw['conv_w'], lw['conv_b'], lw['w_down'], st0, st1)


def _osm_update(s, mask, m, l, acc, v, v_t=False):
    if mask is not None:
        s = jnp.where(mask, s, NEG)
    m_new = jnp.maximum(m, jnp.max(s, axis=1, keepdims=True))
    p = jnp.exp(s - m_new)
    if mask is not None:
        p = jnp.where(mask, p, 0.0)
    alpha = jnp.exp(m - m_new)
    l = alpha * l + jnp.sum(p, axis=1, keepdims=True)
    pv = _dot_nt(p.astype(BF), v) if v_t else _dot(p.astype(BF), v)
    return m_new, l, alpha * acc + pv


def _osm_ref_init(m_ref, l_ref, acc_ref):
    m_ref[...] = jnp.full(m_ref.shape, NEG, F32)
    l_ref[...] = jnp.zeros(l_ref.shape, F32)
    acc_ref[...] = jnp.zeros(acc_ref.shape, F32)


def _osm_ref_step(s, mask, v, m_ref, l_ref, acc_ref):
    reps = s.shape[1] // LANES
    if mask is not None:
        s = jnp.where(mask, s, NEG)
    m_prev = m_ref[...]
    m_new = jnp.maximum(m_prev, jnp.max(s, axis=1, keepdims=True))
    p = jnp.exp(s - jnp.tile(m_new, (1, reps)))
    if mask is not None:
        p = jnp.where(mask, p, 0.0)
    alpha = jnp.exp(m_prev - m_new)
    l_ref[...] = alpha * l_ref[...] + jnp.sum(p, axis=1, keepdims=True)
    d = acc_ref.shape[1]
    a = alpha[:, 0:d] if d <= LANES else jnp.tile(alpha, (1, d // LANES))
    acc_ref[...] = a * acc_ref[...] + _dot(p.astype(BF), v)
    m_ref[...] = m_new


def _key_chunk(t):
    return 4 * TKB if t % (4 * TKB) == 0 else TKB


def _tile_bias(bias_ref, i, first_tile, ntiles):
    parts = [bias_ref[jnp.clip(i - (first_tile + u), 0, 2)].reshape(N_HEADS * TQ, TKB) for u in range(ntiles)]
    return parts[0] if ntiles == 1 else jnp.concatenate(parts, axis=1)


def _pad_lanes(q):
    return jnp.concatenate([q, jnp.zeros_like(q)], axis=1)


def _heads_to_rows(q):
    return jnp.concatenate([q[:, h * HEAD_DIM:(h + 1) * HEAD_DIM] for h in range(N_HEADS)], axis=0)


def _rows_to_heads(o, r):
    return jnp.concatenate([o[h * r:(h + 1) * r, :] for h in range(N_HEADS)], axis=1)


def _tile_rows(a, n):
    return jnp.concatenate([a] * n, axis=0)


def _unsortable(key):
    return lax.bitcast_convert_type(key ^ ((key >> 31) & 0x7FFFFFFF), F32)


VALUE_BISECTIONS = 14
KEY_BISECTIONS = 34


def _kth_search(count_ge, k, vmin, vmax, n_valid):
    lo0 = _sortable(vmin)
    hi0 = _sortable(vmax) + 1

    def done_of(lo, hi, c_lo):
        return (c_lo <= k) | (hi == lo + 1)

    def step(st, midpoint):
        lo, hi, c_lo, c_hi = st
        done = done_of(lo, hi, c_lo)
        cand = jnp.where(done, lo, jnp.clip(midpoint(lo, hi), lo + 1, hi - 1))
        c = count_ge(cand)
        up = (c >= k) & jnp.logical_not(done)
        down = (c < k) & jnp.logical_not(done)
        return jnp.where(up, cand, lo), jnp.where(down, cand, hi), jnp.where(up, c, c_lo), jnp.where(down, c, c_hi)

    def value_mid(lo, hi):
        return _sortable(0.5 * _unsortable(lo) + 0.5 * _unsortable(hi - 1))

    def key_mid(lo, hi):
        return (lo >> 1) + (hi >> 1) + (lo & hi & 1)

    zero = jnp.zeros_like(n_valid)
    c0 = count_ge(zero)
    c1 = count_ge(zero + 1)
    at_zero = (c0 >= k) & (c1 < k)
    pos = c1 >= k
    st = (jnp.where(at_zero, 0, jnp.where(pos, 1, lo0)), jnp.where(at_zero, 1, jnp.where(pos, hi0, 0)),
          jnp.where(at_zero, c0, jnp.where(pos, c1, n_valid)), jnp.where(at_zero, c1, jnp.where(pos, 0, c0)))
    st = lax.fori_loop(0, VALUE_BISECTIONS, lambda _, s: step(s, value_mid), st)

    def cond(carry):
        it, (lo, hi, c_lo, _) = carry
        pending = jnp.max(jnp.where(done_of(lo, hi, c_lo), 0.0, 1.0)) > 0.0
        return (it < KEY_BISECTIONS) & pending

    _, (lo, _, c_lo, c_hi) = lax.while_loop(cond, lambda c: (c[0] + 1, step(c[1], key_mid)), (jnp.int32(0), st))
    return lo, c_lo, c_hi


def _dsa_prompt_kernel(qidx_ref, amisc_ref, qa_ref, kidx_ref, akv_ref, bias_ref, o_ref, key_scr, m_scr, l_scr, acc_scr,
                       *, topk, kt):
    i = pl.program_id(1)
    chunk = kt // TKB
    nchunk = i // chunk + 1
    amisc = amisc_ref[...]
    qidx = qidx_ref[...]
    qh = [qidx[:, h * IDX_DIM:(h + 1) * IDX_DIM] for h in range(IDX_HEADS)]
    wb = [jnp.broadcast_to(amisc[:, IDX_DIM + h:IDX_DIM + h + 1], (TQ, LANES)) for h in range(IDX_HEADS)]
    qpos = i * TQ + lax.broadcasted_iota(jnp.int32, (TQ, kt), 0)
    kcol = lax.broadcasted_iota(jnp.int32, (TQ, kt), 1)

    def score_chunk(c, carry):
        vmax, vmin = carry
        off = pl.multiple_of(c * kt, kt)
        kb = kidx_ref[pl.ds(off, kt), :]
        acc = None
        for h in range(IDX_HEADS):
            term = jnp.maximum(_dot_nt(qh[h], kb), 0.0) * jnp.tile(wb[h], (1, chunk))
            acc = term if acc is None else acc + term
        visible = off + kcol <= qpos
        key_scr[:, pl.ds(off, kt)] = jnp.where(visible, _sortable(acc), INT_MIN)
        hi_part = jnp.where(visible, acc, -jnp.inf)
        lo_part = jnp.where(visible, acc, jnp.inf)
        for u in range(chunk):
            vmax = jnp.maximum(vmax, hi_part[:, u * TKB:(u + 1) * TKB])
            vmin = jnp.minimum(vmin, lo_part[:, u * TKB:(u + 1) * TKB])
        return vmax, vmin

    vmax, vmin = lax.fori_loop(0, nchunk, score_chunk,
                               (jnp.full((TQ, TKB), -jnp.inf, F32), jnp.full((TQ, TKB), jnp.inf, F32)))
    vmax = jnp.max(vmax, axis=1, keepdims=True)
    vmin = jnp.min(vmin, axis=1, keepdims=True)
    n_valid = i * TQ + lax.broadcasted_iota(jnp.int32, (TQ, 1), 0) + 1

    cw = kt

    def count(pred):
        def body(c, acc):
            kc = key_scr[:, pl.ds(pl.multiple_of(c * cw, cw), cw)]
            hit = jnp.where(pred(kc, c * cw), 1.0, 0.0)
            part = hit[:, 0:TKB]
            for u in range(1, chunk):
                part = part + hit[:, u * TKB:(u + 1) * TKB]
            return acc + part
        acc = lax.fori_loop(0, nchunk, body, jnp.zeros((TQ, TKB), F32))
        return jnp.sum(acc, axis=1, keepdims=True).astype(jnp.int32)

    thr, c_lo, c_hi = _kth_search(lambda cand: count(lambda kc, o: kc >= cand), topk, vmin, vmax, n_valid)

    need = topk - c_hi
    tie = c_lo > topk
    any_tie = jnp.max(jnp.where(tie, 1.0, 0.0)) > 0.0
    lane_c = lax.broadcasted_iota(jnp.int32, (TQ, cw), 1)

    @pl.when(any_tie)
    def _():
        nbits = max(1, int(math.ceil(math.log2(key_scr.shape[1]))))

        def bit_body(b, lo):
            cand = lo + jnp.left_shift(jnp.int32(1), nbits - 1 - b)
            cnt = count(lambda kc, o: (kc == thr) & ((lane_c + o) < cand))
            return jnp.where(cnt < need, cand, lo)

        jmax = lax.fori_loop(0, nbits, bit_body, jnp.zeros((TQ, 1), jnp.int32))

        def demote(c, carry):
            sl = pl.ds(pl.multiple_of(c * cw, cw), cw)
            kc = key_scr[:, sl]
            drop = tie & (kc == thr) & ((lane_c + c * cw) > jmax)
            key_scr[:, sl] = jnp.where(drop, thr - 1, kc)
            return carry

        lax.fori_loop(0, nchunk, demote, 0)

    thr_eff = jnp.maximum(thr, INT_MIN + 1)
    q4 = _pad_lanes(_heads_to_rows(qa_ref[...]))
    _osm_ref_init(m_scr, l_scr, acc_scr)

    def att_chunk(c, carry):
        off = pl.multiple_of(c * kt, kt)
        kv = akv_ref[pl.ds(off, kt), :]
        sel = key_scr[:, pl.ds(off, kt)] >= thr_eff
        s = _dot_nt(q4, kv) + _tile_bias(bias_ref, i, c * chunk, chunk)
        _osm_ref_step(s, _tile_rows(sel, N_HEADS), kv, m_scr, l_scr, acc_scr)
        return carry

    lax.fori_loop(0, nchunk, att_chunk, 0)
    o4 = acc_scr[...] / l_scr[...]
    o_ref[...] = _rows_to_heads(o4[:, HEAD_DIM:2 * HEAD_DIM], TQ).astype(BF)


def _dsa_prompt(pr, bias, b, t):
    nq = t // TQ
    topk = min(DSA_TOPK, t // 4)
    rows = N_HEADS * TQ
    q_spec = lambda w: pl.BlockSpec((TQ, w), lambda bi, qi: (bi * nq + qi, 0))
    kv_spec = lambda w: pl.BlockSpec((t, w), lambda bi, qi: (bi, 0))
    return pl.pallas_call(
        functools.partial(_dsa_prompt_kernel, topk=topk, kt=_key_chunk(t)),
        grid=(b, nq),
        in_specs=[q_spec(IDX_HEADS * IDX_DIM), q_spec(128), q_spec(MIX_W), kv_spec(IDX_DIM), kv_spec(128),
                  pl.BlockSpec((3, N_HEADS, TQ, TKB), lambda bi, qi: (0, 0, 0, 0))],
        out_specs=q_spec(MIX_W),
        out_shape=jax.ShapeDtypeStruct((b * t, MIX_W), BF),
        scratch_shapes=[pltpu.VMEM((TQ, t), jnp.int32), pltpu.VMEM((rows, LANES), F32),
                        pltpu.VMEM((rows, LANES), F32), pltpu.VMEM((rows, LANES), F32)],
        compiler_params=_cparams(("arbitrary", "arbitrary")),
        name="dsa_prompt",
    )(pr['qidx'], pr['amisc'], pr['qa'], pr['kidx_b'], pr['akv_b'], bias)


def _compress_kernel(*refs, n_in):
    x_refs, (pe_ref, w_ref, o_ref) = refs[:n_in], refs[n_in:]
    x = x_refs[0][...] if n_in == 1 else jnp.concatenate([r[...] for r in x_refs], axis=0)
    o_ref[:, 0:128] = _dot((x + pe_ref[0:1, :]).astype(BF), w_ref[0])
    o_ref[:, 128:256] = _dot((x + pe_ref[1:2, :]).astype(BF), w_ref[1])


def _compress_weights(lw):
    w4 = lw['cmp_w'].reshape(2, CMP_LEN, HEAD_DIM, HEAD_DIM)
    pe = lw['cmp_pe']

    def half(rs):
        wk, wv = w4[0, rs], w4[1, rs]
        z = jnp.zeros_like(wk)
        rows = jnp.stack([jnp.concatenate([wk, z], axis=-1), jnp.concatenate([z, wv], axis=-1)], axis=1)
        return rows.reshape(CMP_STRIDE * 2 * HEAD_DIM, 2 * HEAD_DIM)

    top, bot = slice(0, CMP_STRIDE), slice(CMP_STRIDE, CMP_LEN)
    w = jnp.stack([half(top), half(bot)]).astype(BF)
    pef = jnp.stack([jnp.transpose(pe[:, top], (1, 0, 2)).reshape(-1), jnp.transpose(pe[:, bot], (1, 0, 2)).reshape(-1)])
    return pef, w


def _compress_dense(chunks, lw, rows_per_step):
    n, width = chunks.shape
    pef, w = _compress_weights(lw)
    return pl.pallas_call(
        functools.partial(_compress_kernel, n_in=1),
        grid=(n // rows_per_step,),
        in_specs=[pl.BlockSpec((rows_per_step, width), lambda i: (i, 0)), pl.BlockSpec((2, width), lambda i: (0, 0)),
                  pl.BlockSpec((2, width, 128), lambda i: (0, 0, 0))],
        out_specs=pl.BlockSpec((rows_per_step, 256), lambda i: (i, 0)),
        out_shape=jax.ShapeDtypeStruct((n, 256), F32),
        compiler_params=_cparams(("arbitrary",)),
        name="compress_prompt",
    )(chunks, pef, w)


def _combine_compressed(ab, gk):
    n = ab.shape[0]
    kv = ab[:, 0:128] + pltpu.roll(ab[:, 128:256], n - 1, axis=0)
    ck_raw = kv[:, 0:HEAD_DIM]
    ms = jnp.mean(ck_raw * ck_raw, axis=-1, keepdims=True)
    ck = (ck_raw * lax.rsqrt(ms + EPS)) * gk
    return ck.astype(BF), kv[:, HEAD_DIM:2 * HEAD_DIM].astype(BF)


def _masked_softmax_rows(s, valid):
    s = jnp.where(valid, s, NEG)
    m = jnp.max(s, axis=1, keepdims=True)
    p = jnp.where(valid, jnp.exp(s - m), 0.0)
    l = jnp.sum(p, axis=1, keepdims=True)
    return p * (1.0 / jnp.where(l > 0.0, l, 1.0))


def _gate_cols(g, c):
    return jnp.concatenate([g[:, h * 3 + c:h * 3 + c + 1] for h in range(N_HEADS)], axis=0)


def _nsa_prompt_kernel(qb_ref, gates_ref, ab_ref, gk_ref, cbias_ref, gmat_ref, emat_ref, tie_ref, slc_ref, win_ref, bias_ref,
                       o_ref, ck_scr, cv_scr, selm_scr, m_scr, l_scr, acc_scr, *, n_sel, ns, nch, kt, wt, nq):
    i = pl.program_id(1)

    @pl.when(i == 0)
    def _():
        ck, cv = _combine_compressed(ab_ref[...], gk_ref[...])
        ck_scr[...] = ck
        cv_scr[...] = cv

    q4 = _heads_to_rows(qb_ref[...])
    qpos = i * TQ + lax.broadcasted_iota(jnp.int32, (TQ, 1), 0)

    cend = lax.broadcasted_iota(jnp.int32, (TQ, nch), 1) * CMP_STRIDE + (CMP_LEN - 1)
    cvalid = cend <= qpos
    s = _dot_nt(q4, ck_scr[...]) + cbias_ref[...].reshape(N_HEADS * TQ, nch)
    pc = _masked_softmax_rows(s, _tile_rows(cvalid, N_HEADS))
    o_c = _dot(pc.astype(BF), cv_scr[...])
    pcs = pc[0:TQ] + pc[TQ:2 * TQ] + pc[2 * TQ:3 * TQ] + pc[3 * TQ:4 * TQ]
    imp = _dot3(pcs, gmat_ref[...])

    j_io = lax.broadcasted_iota(jnp.int32, (TQ, ns), 1)
    cur = qpos >> SEL_SHIFT
    gap = cur - j_io
    score = jnp.where(gap >= 0, jnp.where((j_io == 0) | (gap <= 1), FORCE, imp), NEG)
    rank = jnp.zeros((TQ, ns), F32)
    for jj in range(ns):
        col = score[:, jj:jj + 1]
        rank = rank + (jnp.where(col > score, 1.0, 0.0) + jnp.where(col == score, tie_ref[jj:jj + 1, :], 0.0))
    sel = jnp.where(gap >= 0, rank, float(ns)) < n_sel
    selm_scr[...] = _dot(jnp.where(sel, 1.0, 0.0).astype(BF), emat_ref[...])

    q4p = _pad_lanes(q4)
    chunk = kt // TKB
    qrow = i * TQ + lax.broadcasted_iota(jnp.int32, (TQ, kt), 0)
    kcol = lax.broadcasted_iota(jnp.int32, (TQ, kt), 1)
    _osm_ref_init(m_scr, l_scr, acc_scr)

    def sel_chunk(c, carry):
        off = pl.multiple_of(c * kt, kt)
        kv = slc_ref[pl.ds(off, kt), :]
        mask = (selm_scr[:, pl.ds(off, kt)] > 0.5) & (off + kcol <= qrow)
        s = _dot_nt(q4p, kv) + _tile_bias(bias_ref, i, c * chunk, chunk)
        _osm_ref_step(s, _tile_rows(mask, N_HEADS), kv, m_scr, l_scr, acc_scr)
        return carry

    lax.fori_loop(0, i // chunk + 1, sel_chunk, 0)
    o_s = (acc_scr[...] / l_scr[...])[:, HEAD_DIM:2 * HEAD_DIM]

    w0 = jnp.clip(i - (wt - 1), 0, nq - wt)
    woff = pl.multiple_of(w0 * TKB, TKB)
    kv = win_ref[pl.ds(woff, wt * TKB), :]
    wd = (i * TQ + lax.broadcasted_iota(jnp.int32, (TQ, wt * TKB), 0)) - (
        woff + lax.broadcasted_iota(jnp.int32, (TQ, wt * TKB), 1))
    s = _dot_nt(q4p, kv) + _tile_bias(bias_ref, i, w0, wt)
    pw = _masked_softmax_rows(s, _tile_rows((wd >= 0) & (wd <= WINDOW), N_HEADS))
    o_w = _dot(pw.astype(BF), kv)[:, HEAD_DIM:2 * HEAD_DIM]

    g = gates_ref[...]
    o = _gate_cols(g, 0) * o_c + _gate_cols(g, 1) * o_s + _gate_cols(g, 2) * o_w
    o_ref[...] = _rows_to_heads(o, TQ).astype(BF)


def _nsa_prompt(pr, ab, lw, tabs, b, t):
    nq = t // TQ
    nch = t // CMP_STRIDE
    ns = t // SEL_BLOCK
    n_sel = min(N_SEL, ns)
    q_spec = lambda w: pl.BlockSpec((TQ, w), lambda bi, qi: (bi * nq + qi, 0))
    kv_spec = lambda w: pl.BlockSpec((t, w), lambda bi, qi: (bi, 0))
    const2 = lambda bi, qi: (0, 0)
    return pl.pallas_call(
        functools.partial(_nsa_prompt_kernel, n_sel=n_sel, ns=ns, nch=nch, kt=_key_chunk(t),
                          wt=min(WINDOW // TKB + 1, nq), nq=nq),
        grid=(b, nq),
        in_specs=[q_spec(MIX_W), q_spec(128), pl.BlockSpec((nch, 256), lambda bi, qi: (bi, 0)),
                  pl.BlockSpec((1, HEAD_DIM), const2),
                  pl.BlockSpec((None, N_HEADS, TQ, nch), lambda bi, qi: (qi, 0, 0, 0)),
                  pl.BlockSpec((nch, ns), const2), pl.BlockSpec((ns, t), const2), pl.BlockSpec((ns, ns), const2),
                  kv_spec(128), kv_spec(128),
                  pl.BlockSpec((3, N_HEADS, TQ, TKB), lambda bi, qi: (0, 0, 0, 0))],
        out_specs=q_spec(MIX_W),
        out_shape=jax.ShapeDtypeStruct((b * t, MIX_W), BF),
        scratch_shapes=[pltpu.VMEM((nch, HEAD_DIM), BF), pltpu.VMEM((nch, HEAD_DIM), BF), pltpu.VMEM((TQ, t), F32),
                        pltpu.VMEM((N_HEADS * TQ, LANES), F32), pltpu.VMEM((N_HEADS * TQ, LANES), F32),
                        pltpu.VMEM((N_HEADS * TQ, LANES), F32)],
        compiler_params=_cparams(("arbitrary", "arbitrary")),
        name="nsa_prompt",
    )(pr['qb'], pr['gates'], ab, lw['gk_b'], tabs['cbias_p'], tabs['gmat_p'], tabs['emat_p'], tabs['tie_p'],
      pr['bslc_b'], pr['bwin_b'], tabs['nsa_tiles'])


def _fox_prompt_kernel(q_ref, cq_ref, ckv_ref, ckt_ref, o_ref, m_scr, l_scr, acc_scr, *, kt):
    i = pl.program_id(1)
    chunk = kt // TKB
    nfull = i // chunk
    q4 = _head_block_diag(q_ref[...])
    cq = cq_ref[...]
    cq4 = jnp.concatenate([jnp.broadcast_to(cq[:, h:h + 1], (TQ, LANES)) for h in range(N_HEADS)], axis=0)
    qpos = i * TQ + lax.broadcasted_iota(jnp.int32, (TQ, kt), 0)
    kcol = lax.broadcasted_iota(jnp.int32, (TQ, kt), 1)
    _osm_ref_init(m_scr, l_scr, acc_scr)

    def do_chunk(c, masked):
        off = pl.multiple_of(c * kt, kt)
        kv = ckv_ref[pl.ds(off, kt), :]
        ck4 = jnp.concatenate([jnp.broadcast_to(ckt_ref[h:h + 1, pl.ds(off, kt)], (TQ, kt)) for h in range(N_HEADS)],
                              axis=0)
        s = (_dot_nt(q4, kv[:, 0:MIX_W]) + jnp.tile(cq4, (1, chunk))) - ck4
        mask = _tile_rows(off + kcol <= qpos, N_HEADS) if masked else None
        _osm_ref_step(s, mask, kv[:, MIX_W:2 * MIX_W], m_scr, l_scr, acc_scr)

    def body(c, carry):
        do_chunk(c, False)
        return carry

    lax.fori_loop(0, nfull, body, 0)
    do_chunk(nfull, True)
    o4 = acc_scr[...] / jnp.tile(l_scr[...], (1, MIX_W // LANES))
    o_ref[...] = _head_diag_pick(o4, TQ).astype(BF)


def _fox_prompt(pr, ckt, b, t):
    nq = t // TQ
    rows = N_HEADS * TQ
    q_spec = lambda w: pl.BlockSpec((TQ, w), lambda bi, qi: (bi * nq + qi, 0))
    return pl.pallas_call(
        functools.partial(_fox_prompt_kernel, kt=_key_chunk(t)),
        grid=(b, nq),
        in_specs=[q_spec(MIX_W), q_spec(128), pl.BlockSpec((t, 2 * MIX_W), lambda bi, qi: (bi, 0)),
                  pl.BlockSpec((SUBLANES, t), lambda bi, qi: (bi, 0))],
        out_specs=q_spec(MIX_W),
        out_shape=jax.ShapeDtypeStruct((b * t, MIX_W), BF),
        scratch_shapes=[pltpu.VMEM((rows, LANES), F32), pltpu.VMEM((rows, LANES), F32), pltpu.VMEM((rows, MIX_W), F32)],
        compiler_params=_cparams(("arbitrary", "arbitrary")),
        name="fox_prompt",
    )(pr['qc'], pr['call'], pr['ckv_b'], ckt)


def _stick_terms(z):
    e = jnp.log(1.0 + jnp.exp(-jnp.abs(z)))
    nsp = -(jnp.maximum(z, 0.0) + e)
    return nsp, z + nsp


def _suffix_sums(x, uaug, ntiles):
    r = x.shape[0]
    stack = x if ntiles == 1 else jnp.concatenate([x[:, u * TKB:(u + 1) * TKB] for u in range(ntiles)], axis=0)
    hi = stack.astype(BF)
    lo = (stack - hi.astype(F32)).astype(BF)
    rs = _dot(hi, uaug) + _dot(lo, uaug)
    return [(rs[u * r:(u + 1) * r, 0:TKB], rs[u * r:(u + 1) * r, TKB:2 * TKB]) for u in range(ntiles)]


def _stick_prompt_kernel(q_ref, dkv_ref, u_ref, o_ref, acc_scr, run_scr, *, kt):
    i = pl.program_id(1)
    chunk = kt // TKB
    last = i // chunk
    q4 = _head_block_diag(q_ref[...])
    uaug = u_ref[...]
    qpos = i * TQ + lax.broadcasted_iota(jnp.int32, (TQ, kt), 0)
    kcol = lax.broadcasted_iota(jnp.int32, (TQ, kt), 1)
    acc_scr[...] = jnp.zeros(acc_scr.shape, F32)
    run_scr[...] = jnp.zeros(run_scr.shape, F32)

    def do_chunk(c, masked):
        off = pl.multiple_of(c * kt, kt)
        kv = dkv_ref[pl.ds(off, kt), :]
        nsp, lsig = _stick_terms(_dot_nt(q4, kv[:, 0:MIX_W]))
        if masked:
            ok = _tile_rows(off + kcol < qpos, N_HEADS)
            nsp = jnp.where(ok, nsp, 0.0)
        sums = _suffix_sums(nsp, uaug, chunk)
        run = run_scr[...]
        pieces = [None] * chunk
        for u in reversed(range(chunk)):
            pieces[u] = sums[u][0] + run
            run = run + sums[u][1]
        run_scr[...] = run
        a = jnp.exp(lsig + (pieces[0] if chunk == 1 else jnp.concatenate(pieces, axis=1)))
        if masked:
            a = jnp.where(ok, a, 0.0)
        acc_scr[...] = acc_scr[...] + _dot(a.astype(BF), kv[:, MIX_W:2 * MIX_W])

    def body(jj, carry):
        @pl.when(jj == 0)
        def _():
            do_chunk(last, True)

        @pl.when(jj > 0)
        def _():
            do_chunk(last - jj, False)

        return carry

    lax.fori_loop(0, last + 1, body, 0)
    o_ref[...] = _head_diag_pick(acc_scr[...], TQ).astype(BF)


def _stick_prompt(pr, uaug, b, t):
    nq = t // TQ
    rows = N_HEADS * TQ
    q_spec = lambda w: pl.BlockSpec((TQ, w), lambda bi, qi: (bi * nq + qi, 0))
    return pl.pallas_call(
        functools.partial(_stick_prompt_kernel, kt=_key_chunk(t)),
        grid=(b, nq),
        in_specs=[q_spec(MIX_W), pl.BlockSpec((t, 2 * MIX_W), lambda bi, qi: (bi, 0)),
                  pl.BlockSpec((TKB, 2 * TKB), lambda bi, qi: (0, 0))],
        out_specs=q_spec(MIX_W),
        out_shape=jax.ShapeDtypeStruct((b * t, MIX_W), BF),
        scratch_shapes=[pltpu.VMEM((rows, MIX_W), F32), pltpu.VMEM((rows, LANES), F32)],
        compiler_params=_cparams(("arbitrary", "arbitrary")),
        name="stick_prompt",
    )(pr['qd'], pr['dkv_b'], uaug)


PAGE = 128
TD = SUBLANES


def _page_specs(page_shape, l, pg, page_of):
    tail = (0,) * len(page_shape)

    def spec(i):
        def index_map(*args):
            pt = args[-1]
            return (l, pt[args[0], page_of(*args[1:-1], i)]) + tail
        return pl.BlockSpec((None, None) + page_shape, index_map)

    return [spec(i) for i in range(pg)]


def _pad_rows(a, n):
    return jnp.concatenate([a, jnp.zeros((n - a.shape[0], a.shape[1]), a.dtype)], axis=0)


def _topk_select_ref(key_ref, k):
    rows, width = key_ref.shape

    def count(pred):
        return jnp.sum(jnp.where(pred(key_ref[...]), 1.0, 0.0), axis=1, keepdims=True).astype(jnp.int32)

    keys = key_ref[...]
    valid = keys > INT_MIN
    vals = _unsortable(keys)
    vmax = jnp.max(jnp.where(valid, vals, -jnp.inf), axis=1, keepdims=True)
    vmin = jnp.min(jnp.where(valid, vals, jnp.inf), axis=1, keepdims=True)
    thr, c_lo, c_hi = _kth_search(lambda cand: count(lambda kk: kk >= cand), k, vmin, vmax,
                                  count(lambda kk: kk > INT_MIN))
    need = k - c_hi
    tie = c_lo > k
    any_tie = jnp.max(jnp.where(tie, 1.0, 0.0)) > 0.0

    @pl.when(any_tie)
    def _():
        idx = lax.broadcasted_iota(jnp.int32, (rows, width), 1)
        nbits = max(1, int(math.ceil(math.log2(width))))

        def bit_body(b, lo):
            cand = lo + jnp.left_shift(jnp.int32(1), nbits - 1 - b)
            cnt = count(lambda kk: (kk == thr) & (idx < cand))
            return jnp.where(cnt < need, cand, lo)

        jmax = lax.fori_loop(0, nbits, bit_body, jnp.zeros((rows, 1), jnp.int32))
        kk = key_ref[...]
        key_ref[...] = jnp.where(tie & (kk == thr) & (idx > jmax), thr - 1, kk)

    return jnp.maximum(thr, INT_MIN + 1)


def _osm_scratch_update(s, mask, v, m_scr, l_scr, acc_scr, v_t=False):
    m, l, acc = _osm_update(s, mask, m_scr[...], l_scr[...], acc_scr[...], v, v_t)
    m_scr[...] = m
    l_scr[...] = l
    acc_scr[...] = acc


def _osm_scratch_init(m_scr, l_scr, acc_scr):
    m_scr[...] = jnp.full(m_scr.shape, NEG, F32)
    l_scr[...] = jnp.zeros(l_scr.shape, F32)
    acc_scr[...] = jnp.zeros(acc_scr.shape, F32)


def _head_block_diag(q):
    lane = lax.broadcasted_iota(jnp.int32, q.shape, 1)
    return jnp.concatenate([jnp.where(lane >> HEAD_SHIFT == h, q, jnp.zeros_like(q)) for h in range(N_HEADS)], axis=0)


def _head_diag_pick(o, r):
    lane = lax.broadcasted_iota(jnp.int32, (r, o.shape[1]), 1)
    out = None
    for h in range(N_HEADS):
        part = jnp.where(lane >> HEAD_SHIFT == h, o[h * r:(h + 1) * r, :], 0.0)
        out = part if out is None else out + part
    return out


def _causal_new_mask(strict):
    lane = lax.broadcasted_iota(jnp.int32, (TD, PAGE), 1)
    rowi = lax.broadcasted_iota(jnp.int32, (TD, PAGE), 0)
    return lane < rowi if strict else lane <= rowi


def _compress_paged(cache, page_table, l, lw, nstep, pg):
    s = page_table.shape[0]
    pef, w = _compress_weights(lw)
    width = pef.shape[1]
    rows = PAGE // CMP_STRIDE

    def body(pt_ref, *refs):
        _compress_kernel(*refs, n_in=pg)

    return pl.pallas_call(
        body,
        grid_spec=pltpu.PrefetchScalarGridSpec(
            num_scalar_prefetch=1, grid=(s, nstep),
            in_specs=_page_specs((rows, width), l, pg, lambda p, i: p * pg + i)
            + [pl.BlockSpec((2, width), lambda si, p, pt: (0, 0)), pl.BlockSpec((2, width, 128), lambda si, p, pt: (0, 0, 0))],
            out_specs=pl.BlockSpec((None, pg * rows, 256), lambda si, p, pt: (si, p, 0))),
        out_shape=jax.ShapeDtypeStruct((s, nstep * pg * rows, 256), F32),
        compiler_params=_cparams(("arbitrary", "arbitrary")),
        name="compress_decode",
    )(page_table, *([cache] * pg), pef, w)


def _dsa_decode_kernel(pt_ref, qidx_ref, amisc_ref, qa_ref, kidxn_ref, akvn_ref, blast_ref, bfar_ref, bnew_ref, *rest,
                       topk, nstep, pg):
    kid_refs, kv_refs = rest[:pg], rest[pg:2 * pg]
    o_ref, key_scr, thr_scr, m_scr, l_scr, acc_scr = rest[2 * pg:]
    ph, p = pl.program_id(1), pl.program_id(2)
    wstep = pg * PAGE
    npast = nstep * wstep
    last = p == nstep - 1

    @pl.when(ph == 0)
    def _():
        qidx = qidx_ref[...]
        w = amisc_ref[:, IDX_DIM:IDX_DIM + IDX_HEADS]
        q64 = jnp.concatenate([qidx[:, h * IDX_DIM:(h + 1) * IDX_DIM] for h in range(IDX_HEADS)], axis=0)

        def scores(sc):
            acc = None
            for h in range(IDX_HEADS):
                term = jnp.maximum(sc[h * TD:(h + 1) * TD, :], 0.0) * w[:, h:h + 1]
                acc = term if acc is None else acc + term
            return acc

        kid_t = jnp.concatenate([r[...].astype(BF) for r in kid_refs], axis=1)
        key_scr[:, pl.ds(pl.multiple_of(p * wstep, wstep), wstep)] = _sortable(scores(_dot(q64, kid_t)))

        @pl.when(last)
        def _():
            acc = scores(_dot_nt(q64, _pad_rows(kidxn_ref[...], PAGE)))
            key_scr[:, npast:npast + PAGE] = jnp.where(_causal_new_mask(False), _sortable(acc), INT_MIN)
            thr_scr[...] = jnp.broadcast_to(_topk_select_ref(key_scr, topk), thr_scr.shape)

    @pl.when(ph == 1)
    def _():
        @pl.when(p == 0)
        def _():
            _osm_scratch_init(m_scr, l_scr, acc_scr)

        thr = thr_scr[:, 0:1]
        q4 = _heads_to_rows(qa_ref[...])
        k_t = jnp.concatenate([r[0:HEAD_DIM, :].astype(BF) for r in kv_refs], axis=1)
        v_t = jnp.concatenate([r[HEAD_DIM:2 * HEAD_DIM, :].astype(BF) for r in kv_refs], axis=1)
        bias = jnp.where(last, blast_ref[...], bfar_ref[...])
        s = _dot(q4, k_t) + bias
        sel = key_scr[:, pl.ds(pl.multiple_of(p * wstep, wstep), wstep)] >= thr
        _osm_scratch_update(s, _tile_rows(sel, N_HEADS), v_t, m_scr, l_scr, acc_scr, v_t=True)

        @pl.when(last)
        def _():
            kvn = _pad_rows(akvn_ref[...], PAGE)
            s = _dot_nt(q4, kvn[:, 0:HEAD_DIM]) + bnew_ref[...]
            sel = key_scr[:, npast:npast + PAGE] >= thr
            m, l, acc = _osm_update(s, _tile_rows(sel, N_HEADS), m_scr[...], l_scr[...], acc_scr[...],
                                    kvn[:, HEAD_DIM:2 * HEAD_DIM])
            o_ref[...] = _rows_to_heads(acc / l, TD).astype(BF)


def _seq_spec(width, ngrid):
    if ngrid == 2:
        return pl.BlockSpec((None, TD, width), lambda si, p, pt: (si, 0, 0))
    return pl.BlockSpec((None, TD, width), lambda si, ph, p, pt: (si, 0, 0))


def _const_spec(shape, ngrid):
    zeros = (0,) * len(shape)
    if ngrid == 2:
        return pl.BlockSpec(shape, lambda si, p, pt: zeros)
    return pl.BlockSpec(shape, lambda si, ph, p, pt: zeros)


def _dsa_decode(prs, cache_kidx, cache_akv, page_table, l, tabs, nstep, pg):
    s = page_table.shape[0]
    npast = nstep * pg * PAGE
    topk = min(DSA_TOPK, (npast + TD) // 4)
    wstep = pg * PAGE
    kid_specs = _page_specs((IDX_DIM, PAGE), l, pg, lambda ph, p, i: jnp.where(ph == 0, p, nstep - 1) * pg + i)
    kv_specs = _page_specs((2 * HEAD_DIM, PAGE), l, pg, lambda ph, p, i: jnp.where(ph == 0, 0, p) * pg + i)
    return pl.pallas_call(
        functools.partial(_dsa_decode_kernel, topk=topk, nstep=nstep, pg=pg),
        grid_spec=pltpu.PrefetchScalarGridSpec(
            num_scalar_prefetch=1, grid=(s, 2, nstep),
            in_specs=[_seq_spec(IDX_HEADS * IDX_DIM, 3), _seq_spec(128, 3), _seq_spec(MIX_W, 3), _seq_spec(IDX_DIM, 3),
                      _seq_spec(128, 3), _const_spec((N_HEADS * TD, wstep), 3), _const_spec((N_HEADS * TD, 1), 3),
                      _const_spec((N_HEADS * TD, PAGE), 3)] + kid_specs + kv_specs,
            out_specs=_seq_spec(MIX_W, 3),
            scratch_shapes=[pltpu.VMEM((TD, npast + PAGE), jnp.int32), pltpu.VMEM((TD, 128), jnp.int32),
                            pltpu.VMEM((N_HEADS * TD, 1), F32), pltpu.VMEM((N_HEADS * TD, 1), F32),
                            pltpu.VMEM((N_HEADS * TD, HEAD_DIM), F32)]),
        out_shape=jax.ShapeDtypeStruct((s, TD, MIX_W), BF),
        compiler_params=_cparams(("arbitrary", "arbitrary", "arbitrary")),
        name="dsa_decode",
    )(page_table, prs['qidx'], prs['amisc'], prs['qa'], prs['kidx_b'], prs['akv_b'],
      tabs['a_last'], tabs['a_far'], tabs['a_new'], *([cache_kidx] * pg), *([cache_akv] * pg))


def _nsa_decode_kernel(pt_ref, qb_ref, gates_ref, ab_ref, gk_ref, cbias_ref, gmat_ref, slcn_ref, winp_ref, winn_ref,
                       blast_ref, bfar_ref, bnew_ref, wbp_ref, *rest, n_sel, nstep, pg):
    slc_refs = rest[:pg]
    o_ref, selm_scr, bkey_scr, oc_scr, m_scr, l_scr, acc_scr = rest[pg:]
    p = pl.program_id(1)
    wstep = pg * PAGE
    npast = nstep * wstep
    nch = npast // CMP_STRIDE
    nsb = npast // SEL_BLOCK
    last = p == nstep - 1
    q4 = _heads_to_rows(qb_ref[...])

    @pl.when(p == 0)
    def _():
        ck, cv = _combine_compressed(ab_ref[...], gk_ref[...])
        qpos = npast + lax.broadcasted_iota(jnp.int32, (TD, 1), 0)
        cend = lax.broadcasted_iota(jnp.int32, (TD, nch), 1) * CMP_STRIDE + (CMP_LEN - 1)
        s = _dot_nt(q4, ck) + cbias_ref[...]
        pc = _masked_softmax_rows(s, _tile_rows(cend <= qpos, N_HEADS))
        oc_scr[...] = _dot(pc.astype(BF), cv)
        pcs = pc[0:TD] + pc[TD:2 * TD] + pc[2 * TD:3 * TD] + pc[3 * TD:4 * TD]
        imp = _dot3(pcs, gmat_ref[...])
        j_io = lax.broadcasted_iota(jnp.int32, (TD, nsb), 1)
        forced = (j_io == 0) | (j_io == nsb - 1)
        bkey_scr[...] = _sortable(jnp.where(forced, FORCE, imp))
        thr = _topk_select_ref(bkey_scr, n_sel - 1)
        selb = jnp.where(bkey_scr[...] >= thr, 1.0, 0.0).astype(BF)
        blk = lax.broadcasted_iota(jnp.int32, (nsb, wstep), 0)
        col = lax.broadcasted_iota(jnp.int32, (nsb, wstep), 1)
        for c in range(nstep):
            expand = jnp.where(blk == (c * wstep + col) >> SEL_SHIFT, 1.0, 0.0).astype(BF)
            selm_scr[:, c * wstep:(c + 1) * wstep] = _dot(selb, expand)
        _osm_scratch_init(m_scr, l_scr, acc_scr)

    k_t = jnp.concatenate([r[0:HEAD_DIM, :].astype(BF) for r in slc_refs], axis=1)
    v_t = jnp.concatenate([r[HEAD_DIM:2 * HEAD_DIM, :].astype(BF) for r in slc_refs], axis=1)
    bias = jnp.where(last, blast_ref[...], bfar_ref[...])
    s = _dot(q4, k_t) + bias
    mask = selm_scr[:, pl.ds(pl.multiple_of(p * wstep, wstep), wstep)] > 0.5
    _osm_scratch_update(s, _tile_rows(mask, N_HEADS), v_t, m_scr, l_scr, acc_scr, v_t=True)

    @pl.when(last)
    def _():
        lo, hi = slice(0, HEAD_DIM), slice(HEAD_DIM, 2 * HEAD_DIM)
        new_mask = _tile_rows(_causal_new_mask(False), N_HEADS)
        kvn = _pad_rows(slcn_ref[...], PAGE)
        s = _dot_nt(q4, kvn[:, lo]) + bnew_ref[...]
        _, l_s, acc_s = _osm_update(s, new_mask, m_scr[...], l_scr[...], acc_scr[...], kvn[:, hi])
        wp = winp_ref[...].astype(BF)
        nw = wp.shape[1]
        wd = nw + lax.broadcasted_iota(jnp.int32, (TD, nw), 0) - lax.broadcasted_iota(jnp.int32, (TD, nw), 1)
        s = _dot(q4, wp[lo, :]) + wbp_ref[...]
        carry = _osm_update(s, _tile_rows(wd <= WINDOW, N_HEADS), jnp.full((N_HEADS * TD, 1), NEG, F32),
                            jnp.zeros((N_HEADS * TD, 1), F32), jnp.zeros((N_HEADS * TD, HEAD_DIM), F32), wp[hi, :],
                            v_t=True)
        kvw = _pad_rows(winn_ref[...], PAGE)
        s = _dot_nt(q4, kvw[:, lo]) + bnew_ref[...]
        _, l_w, acc_w = _osm_update(s, new_mask, *carry, kvw[:, hi])
        g = gates_ref[...]
        o = _gate_cols(g, 0) * oc_scr[...] + _gate_cols(g, 1) * (acc_s / l_s) + _gate_cols(g, 2) * (acc_w / l_w)
        o_ref[...] = _rows_to_heads(o, TD).astype(BF)


def _nsa_decode(prs, ab, win_past, cache_slc, page_table, l, lw, tabs, nstep, pg):
    s = page_table.shape[0]
    wstep = pg * PAGE
    npast = nstep * wstep
    nch, nsb = npast // CMP_STRIDE, npast // SEL_BLOCK
    n_sel = min(N_SEL, nsb + 1)
    assert n_sel >= 2 and win_past.shape[2] == WINDOW
    return pl.pallas_call(
        functools.partial(_nsa_decode_kernel, n_sel=n_sel, nstep=nstep, pg=pg),
        grid_spec=pltpu.PrefetchScalarGridSpec(
            num_scalar_prefetch=1, grid=(s, nstep),
            in_specs=[_seq_spec(MIX_W, 2), _seq_spec(128, 2),
                      pl.BlockSpec((None, nch, 256), lambda si, p, pt: (si, 0, 0)), _const_spec((1, HEAD_DIM), 2),
                      _const_spec((N_HEADS * TD, nch), 2), _const_spec((nch, nsb), 2), _seq_spec(128, 2),
                      pl.BlockSpec((None, 2 * HEAD_DIM, WINDOW), lambda si, p, pt: (si, 0, 0)), _seq_spec(128, 2),
                      _const_spec((N_HEADS * TD, wstep), 2), _const_spec((N_HEADS * TD, 1), 2),
                      _const_spec((N_HEADS * TD, PAGE), 2), _const_spec((N_HEADS * TD, WINDOW), 2)]
            + _page_specs((2 * HEAD_DIM, PAGE), l, pg, lambda p, i: p * pg + i),
            out_specs=_seq_spec(MIX_W, 2),
            scratch_shapes=[pltpu.VMEM((TD, npast), F32), pltpu.VMEM((TD, nsb), jnp.int32),
                            pltpu.VMEM((N_HEADS * TD, HEAD_DIM), F32), pltpu.VMEM((N_HEADS * TD, 1), F32),
                            pltpu.VMEM((N_HEADS * TD, 1), F32), pltpu.VMEM((N_HEADS * TD, HEAD_DIM), F32)]),
        out_shape=jax.ShapeDtypeStruct((s, TD, MIX_W), BF),
        compiler_params=_cparams(("arbitrary", "arbitrary")),
        name="nsa_decode",
    )(page_table, prs['qb'], prs['gates'], ab, lw['gk_b'], tabs['cbias_d'], tabs['gmat_d'], prs['bslc_b'], win_past,
      prs['bwin_b'], tabs['b_last'], tabs['b_far'], tabs['b_new'], tabs['wb_past'], *([cache_slc] * pg))


def _fox_decode_kernel(pt_ref, q_ref, lfn_ref, lftn_ref, ckvn_ref, u_ref, tinc_ref, *rest, nstep, pg):
    kv_refs, lf_refs = rest[:pg], rest[pg:2 * pg]
    o_ref, qbd_scr, cq_scr, m_scr, l_scr, acc_scr, carry_scr = rest[2 * pg:]
    p = pl.program_id(1)

    @pl.when(p == 0)
    def _():
        qbd = _head_block_diag(q_ref[...])
        qbd_scr[...] = qbd
        lf = lfn_ref[...]
        rows = [lf[0:1, :]]
        for r in range(1, TD):
            rows.append(rows[-1] + lf[r:r + 1, :])
        npf = jnp.concatenate(rows, axis=0)
        cq4 = jnp.concatenate([npf[:, h:h + 1] for h in range(N_HEADS)], axis=0)
        cq_scr[...] = cq4
        npt = _dot3(lftn_ref[...], tinc_ref[...])
        ck4 = jnp.concatenate([jnp.broadcast_to(npt[h:h + 1, :], (TD, PAGE)) for h in range(N_HEADS)], axis=0)
        kvn = _pad_rows(ckvn_ref[...], PAGE)
        s = (_dot_nt(qbd, kvn[:, 0:MIX_W]) + cq4) - ck4
        _osm_scratch_init(m_scr, l_scr, acc_scr)
        _osm_scratch_update(s, _tile_rows(_causal_new_mask(False), N_HEADS), kvn[:, MIX_W:2 * MIX_W], m_scr, l_scr, acc_scr)
        carry_scr[...] = jnp.zeros_like(carry_scr)

    x = jnp.concatenate([r[...] for r in lf_refs], axis=0)
    rs = _dot3(x, u_ref[...])
    tot = jnp.sum(x, axis=1, keepdims=True)
    off = carry_scr[:, 0:1]
    pieces = [None] * pg
    for i in reversed(range(pg)):
        s_i = rs[i * SUBLANES:(i + 1) * SUBLANES, :] + off
        pieces[i] = jnp.concatenate([jnp.broadcast_to(s_i[h:h + 1, :], (TD, PAGE)) for h in range(N_HEADS)], axis=0)
        off = off + tot[i * SUBLANES:(i + 1) * SUBLANES, :]
    carry_scr[...] = jnp.broadcast_to(off, carry_scr.shape)
    k_t = jnp.concatenate([r[0:MIX_W, :].astype(BF) for r in kv_refs], axis=1)
    v_t = jnp.concatenate([r[MIX_W:2 * MIX_W, :].astype(BF) for r in kv_refs], axis=1)
    s = (_dot(qbd_scr[...], k_t) + cq_scr[...]) + jnp.concatenate(pieces, axis=1)
    _osm_scratch_update(s, None, v_t, m_scr, l_scr, acc_scr, v_t=True)

    @pl.when(p == nstep - 1)
    def _():
        o_ref[...] = _head_diag_pick(acc_scr[...] / l_scr[...], TD).astype(BF)


def _fox_decode(prs, lft_new, cache_ckv, cache_lft, page_table, l, tabs, nstep, pg):
    s = page_table.shape[0]
    rev = lambda p, i: (nstep - 1 - p) * pg + i
    return pl.pallas_call(
        functools.partial(_fox_decode_kernel, nstep=nstep, pg=pg),
        grid_spec=pltpu.PrefetchScalarGridSpec(
            num_scalar_prefetch=1, grid=(s, nstep),
            in_specs=[_seq_spec(MIX_W, 2), _seq_spec(128, 2), _seq_spec(128, 2), _seq_spec(2 * MIX_W, 2),
                      _const_spec((PAGE, PAGE), 2), _const_spec((PAGE, PAGE), 2)]
            + _page_specs((2 * MIX_W, PAGE), l, pg, rev) + _page_specs((SUBLANES, PAGE), l, pg, rev),
            out_specs=_seq_spec(MIX_W, 2),
            scratch_shapes=[pltpu.VMEM((N_HEADS * TD, MIX_W), BF), pltpu.VMEM((N_HEADS * TD, 1), F32),
                            pltpu.VMEM((N_HEADS * TD, 1), F32), pltpu.VMEM((N_HEADS * TD, 1), F32),
                            pltpu.VMEM((N_HEADS * TD, MIX_W), F32), pltpu.VMEM((SUBLANES, 128), F32)]),
        out_shape=jax.ShapeDtypeStruct((s, TD, MIX_W), BF),
        compiler_params=_cparams(("arbitrary", "arbitrary")),
        name="fox_decode",
    )(page_table, prs['qc'], prs['logf'], lft_new, prs['ckv_b'], tabs['umat'], tabs['tinc'],
      *([cache_ckv] * pg), *([cache_lft] * pg))


def _stick_decode_kernel(pt_ref, q_ref, dkvn_ref, u_ref, *rest, nstep, pg):
    kv_refs = rest[:pg]
    o_ref, qbd_scr, acc_scr, run_scr = rest[pg:]
    p = pl.program_id(1)
    umat = u_ref[...]
    rows = N_HEADS * TD

    @pl.when(p == 0)
    def _():
        qbd = _head_block_diag(q_ref[...])
        qbd_scr[...] = qbd
        kvn = _pad_rows(dkvn_ref[...], PAGE)
        strict = _tile_rows(_causal_new_mask(True), N_HEADS)
        nsp, lsig = _stick_terms(_dot_nt(qbd, kvn[:, 0:MIX_W]))
        nsp = jnp.where(strict, nsp, 0.0)
        a = jnp.where(strict, jnp.exp(lsig + _dot3(nsp, umat)), 0.0)
        acc_scr[...] = _dot(a.astype(BF), kvn[:, MIX_W:2 * MIX_W])
        run_scr[...] = jnp.sum(nsp, axis=1, keepdims=True)

    k_t = jnp.concatenate([r[0:MIX_W, :].astype(BF) for r in kv_refs], axis=1)
    v_t = jnp.concatenate([r[MIX_W:2 * MIX_W, :].astype(BF) for r in kv_refs], axis=1)
    nsp, lsig = _stick_terms(_dot(qbd_scr[...], k_t))
    stack = jnp.concatenate([nsp[:, i * PAGE:(i + 1) * PAGE] for i in range(pg)], axis=0)
    rs = _dot3(stack, umat)
    tot = jnp.sum(stack, axis=1, keepdims=True)
    off = run_scr[...]
    pieces = [None] * pg
    for i in reversed(range(pg)):
        pieces[i] = rs[i * rows:(i + 1) * rows, :] + off
        off = off + tot[i * rows:(i + 1) * rows, :]
    run_scr[...] = off
    a = jnp.exp(lsig + jnp.concatenate(pieces, axis=1))
    acc_scr[...] = acc_scr[...] + _dot_nt(a.astype(BF), v_t)

    @pl.when(p == nstep - 1)
    def _():
        o_ref[...] = _head_diag_pick(acc_scr[...], TD).astype(BF)


def _stick_decode(prs, cache_dkv, page_table, l, tabs, nstep, pg):
    s = page_table.shape[0]
    return pl.pallas_call(
        functools.partial(_stick_decode_kernel, nstep=nstep, pg=pg),
        grid_spec=pltpu.PrefetchScalarGridSpec(
            num_scalar_prefetch=1, grid=(s, nstep),
            in_specs=[_seq_spec(MIX_W, 2), _seq_spec(2 * MIX_W, 2), _const_spec((PAGE, PAGE), 2)]
            + _page_specs((2 * MIX_W, PAGE), l, pg, lambda p, i: (nstep - 1 - p) * pg + i),
            out_specs=_seq_spec(MIX_W, 2),
            scratch_shapes=[pltpu.VMEM((N_HEADS * TD, MIX_W), BF), pltpu.VMEM((N_HEADS * TD, MIX_W), F32),
                            pltpu.VMEM((N_HEADS * TD, 1), F32)]),
        out_shape=jax.ShapeDtypeStruct((s, TD, MIX_W), BF),
        compiler_params=_cparams(("arbitrary", "arbitrary")),
        name="stick_decode",
    )(page_table, prs['qd'], prs['dkv_b'], tabs['umat'], *([cache_dkv] * pg))


_IN_A, _IN_B, _IN_C, _IN_D = 968, 1620, 2392, 3160


def _bucket(dist):
    n = jnp.maximum(dist, 0)
    exact = N_BUCKETS // 2
    nf = jnp.maximum(n, 1).astype(F32)
    large = exact + (jnp.log(nf / exact) / math.log(MAX_DIST / exact) * (N_BUCKETS - exact)).astype(jnp.int32)
    return jnp.where(n < exact, n, jnp.minimum(large, N_BUCKETS - 1))


def _bias_table(tab, dist):
    b = _bucket(dist)
    out = jnp.zeros((tab.shape[1],) + dist.shape, F32)
    for j in range(N_BUCKETS):
        out = jnp.where(b[None] == j, tab[j].reshape((-1,) + (1,) * dist.ndim), out)
    return out


def _prep_layer(l, p, tm_max):
    w_in = p['w_in'][l]
    d = w_in.shape[0]
    z = lambda n: jnp.zeros((d, n), w_in.dtype)
    w_proj = jnp.concatenate([w_in[:, 0:_IN_A], z(1024 - _IN_A), w_in[:, _IN_A:_IN_B], z(768 - (_IN_B - _IN_A)),
                              w_in[:, _IN_B:_IN_C], z(896 - (_IN_C - _IN_B)), w_in[:, _IN_C:_IN_D]], axis=1)
    assert w_proj.shape[1] == N_PROJ
    qk = p['qk_gain'][l]
    gain = jnp.ones((N_PROJ,), F32)
    nmask = jnp.zeros((N_PROJ,), F32)
    for c0, g, rep in ((C_AQ, qk[0, 0], 4), (C_AKV, qk[0, 1], 1), (C_BQ, qk[1, 0], 4), (C_BSLC, qk[1, 1], 1),
                       (C_BWIN, qk[1, 1], 1), (C_CQ, qk[2, 0], 4), (C_CK, qk[2, 1], 4)):
        gain = gain.at[c0:c0 + rep * HEAD_DIM].set(jnp.tile(g, rep))
        nmask = nmask.at[c0:c0 + rep * HEAD_DIM].set(1.0)
    gidx = np.arange(MIX_W) // HEAD_DIM
    bd = jnp.asarray((gidx[:, None] == gidx[None, :]).astype(np.float32) / HEAD_DIM, BF)
    tri = jnp.asarray(np.tril(np.ones((tm_max, tm_max), np.float32)), BF)
    return {
        'g_attn': p['norm_attn'][l][None, :], 'w_proj': w_proj.astype(BF), 'gain': gain[None, :],
        'nmask': nmask[None, :], 'b_f': jnp.zeros((1, 128), F32).at[0, :N_HEADS].set(p['b_forget'][l]),
        'bd': bd, 'tri': tri,
        'w_gate': w_in[:, _IN_D:].astype(BF), 'w_br': p['w_branch'][l].astype(BF), 'w_o': p['w_out'][l].astype(BF),
        'g_ffn': p['norm_ffn'][l][None, :], 'w_up': p['w_up'][l].astype(BF), 'conv_w': p['conv_w'][l],
        'conv_b': p['conv_b'][l][None, :], 'w_down': p['w_down'][l].astype(BF),
        'gk_b': qk[1, 1][None, :], 'cmp_w': p['cmp_w'][l], 'cmp_pe': p['cmp_pe'][l],
    }


def _toeplitz_bias(tab):
    r = jnp.arange(TQ)[:, None]
    c = jnp.arange(TKB)[None, :]
    return jnp.stack([_bias_table(tab, dd * TKB + r - c) for dd in range(3)])


def _prompt_tables(rel_bias, t):
    tab_a, tab_b = rel_bias[:, :N_HEADS], rel_bias[:, N_HEADS:]
    nq, nch, ns = t // TQ, t // CMP_STRIDE, t // SEL_BLOCK
    qpos = jnp.arange(t).reshape(nq, TQ)
    cend = jnp.arange(nch) * CMP_STRIDE + (CMP_LEN - 1)
    cbias = jnp.transpose(_bias_table(tab_b, qpos[:, :, None] - cend[None, None, :]), (1, 0, 2, 3))
    n = np.arange(nch)
    gmat = ((n[:, None] // (SEL_BLOCK // CMP_STRIDE) == np.arange(ns)[None, :]) & (n[:, None] < nch - 1))
    emat = np.arange(t)[None, :] // SEL_BLOCK == np.arange(ns)[:, None]
    c = np.arange(TKB)
    return {
        'dsa_tiles': _toeplitz_bias(tab_a), 'nsa_tiles': _toeplitz_bias(tab_b), 'cbias_p': cbias,
        'gmat_p': jnp.asarray(gmat.astype(np.float32), BF), 'emat_p': jnp.asarray(emat.astype(np.float32), BF),
        'tie_p': jnp.asarray((np.arange(ns)[None, :] > np.arange(ns)[:, None]).astype(np.float32)),
        'uaug': jnp.asarray(np.concatenate([(c[:, None] > c[None, :]), np.ones((TKB, TKB), bool)], axis=1)
                            .astype(np.float32), BF),
    }


def _layer_prompt(x, lw, tabs):
    b, t, d = x.shape
    assert t % TQ == 0
    tm = next(c for c in (DENSE_TM, 256, TQ) if t % c == 0)
    x2 = x.reshape(b * t, d)
    pr = _proj(x2, lw, tm=tm, tiles_per_seq=t // tm, with_cumsum=True)
    o_a = _dsa_prompt(pr, tabs['dsa_tiles'], b, t)
    ab = _compress_dense(pr['bcmp'].reshape(b * t // CMP_STRIDE, CMP_STRIDE * 2 * HEAD_DIM), lw, t // CMP_STRIDE)
    o_b = _nsa_prompt(pr, ab, lw, tabs, b, t)
    ckt = jnp.transpose(pr['call'].reshape(b, t, 128)[:, :, :SUBLANES], (0, 2, 1)).reshape(b * SUBLANES, t)
    o_c = _fox_prompt(pr, ckt, b, t)
    o_d = _stick_prompt(pr, tabs['uaug'], b, t)
    xm = _merge(x2, o_a, o_b, o_c, o_d, lw, tm=tm)
    dummy = jnp.zeros((SUBLANES, lw['w_up'].shape[1]), F32)
    y2, conv = _ffn(xm, lw, dummy, dummy, tm=tm, carry_mode=True, tiles_per_seq=t // tm)
    keep = min(WINDOW, t)
    new = {
        'a_kv': pr['akv'].reshape(b, t, 2, HEAD_DIM),
        'a_kidx': pr['amisc'][:, :IDX_DIM].reshape(b, t, IDX_DIM),
        'b_cmp_kv': pr['bcmp'].reshape(b, t, 2, HEAD_DIM),
        'b_slc_kv': pr['bslc'].reshape(b, t, 2, HEAD_DIM),
        'b_win_kv': pr['bwin'].reshape(b, t, 2, HEAD_DIM)[:, t - keep:],
        'c_kv': pr['ckv'].reshape(b, t, 2, N_HEADS, HEAD_DIM),
        'c_logf': pr['logf'][:, :N_HEADS].reshape(b, t, N_HEADS),
        'd_kv': pr['dkv'].reshape(b, t, 2, N_HEADS, HEAD_DIM),
        'ffn_conv': conv.reshape(b, SUBLANES, -1)[:, SUBLANES - (CONV_W - 1):],
    }
    return y2.reshape(b, t, d), new, (o_a, o_b, o_c, o_d)


def _decode_tables(rel_bias, npast, pg):
    tab_a, tab_b = rel_bias[:, :N_HEADS], rel_bias[:, N_HEADS:]
    wstep = pg * PAGE
    tq = jnp.arange(TD)
    rows = lambda tab, dist: _bias_table(tab, dist).reshape(N_HEADS * TD, -1)
    far = lambda tab: jnp.repeat(tab[N_BUCKETS - 1], TD)[:, None]
    d_last = (npast + tq)[:, None] - (npast - wstep + jnp.arange(wstep))[None, :]
    d_new = tq[:, None] - jnp.arange(PAGE)[None, :]
    nch, nsb = npast // CMP_STRIDE, npast // SEL_BLOCK
    d_cmp = (npast + tq)[:, None] - (jnp.arange(nch) * CMP_STRIDE + (CMP_LEN - 1))[None, :]
    d_win = WINDOW + tq[:, None] - jnp.arange(WINDOW)[None, :]
    n = np.arange(nch)
    gmat = (n[:, None] // (SEL_BLOCK // CMP_STRIDE) == np.arange(nsb)[None, :]) & (n[:, None] < nch - 1)
    c = np.arange(PAGE)
    return {
        'a_last': rows(tab_a, d_last), 'a_far': far(tab_a), 'a_new': rows(tab_a, d_new),
        'b_last': rows(tab_b, d_last), 'b_far': far(tab_b), 'b_new': rows(tab_b, d_new),
        'cbias_d': rows(tab_b, d_cmp), 'wb_past': rows(tab_b, d_win),
        'gmat_d': jnp.asarray(gmat.astype(np.float32), BF),
        'umat': jnp.asarray((c[:, None] > c[None, :]).astype(np.float32), BF),
        'tinc': jnp.asarray((c[:, None] <= c[None, :]).astype(np.float32), BF),
    }


def _layer_decode(x, lw, tabs, l, caches, ffn_state, page_table, nstep, pg):
    s, td, d = x.shape
    assert td == TD
    m = s * td
    x2 = x.reshape(m, d)
    pr = _proj(x2, lw, tm=m, tiles_per_seq=1, with_cumsum=False)
    prs = {k: v.reshape(s, td, v.shape[-1]) for k, v in pr.items()}
    o_a = _dsa_decode(prs, caches['a_kidx'], caches['a_kv'], page_table, l, tabs, nstep, pg)
    ab = _compress_paged(caches['b_cmp'], page_table, l, lw, nstep, pg)
    o_b = _nsa_decode(prs, ab, caches['b_win_t'][l], caches['b_slc'], page_table, l, lw, tabs, nstep, pg)
    lft_new = jnp.pad(jnp.transpose(prs['logf'][:, :, :SUBLANES], (0, 2, 1)), ((0, 0), (0, 0), (0, PAGE - td)))
    o_c = _fox_decode(prs, lft_new, caches['c_kv'], caches['c_lft'], page_table, l, tabs, nstep, pg)
    o_d = _stick_decode(prs, caches['d_kv'], page_table, l, tabs, nstep, pg)
    flat = lambda o: o.reshape(m, MIX_W)
    xm = _merge(x2, flat(o_a), flat(o_b), flat(o_c), flat(o_d), lw, tm=m)
    st0 = jnp.repeat(ffn_state[:, 0], td, axis=0)
    st1 = jnp.repeat(ffn_state[:, 1], td, axis=0)
    y2, conv = _ffn(xm, lw, st0, st1, tm=m, carry_mode=False, tiles_per_seq=1)
    win_new = pr['bwin'].reshape(s, td, 2, HEAD_DIM)
    win_all = jnp.concatenate([caches['b_win'][l], win_new], axis=1)
    keep = min(WINDOW, win_all.shape[1])
    new = {
        'a_kv': pr['akv'].reshape(s, td, 2, HEAD_DIM),
        'a_kidx': pr['amisc'][:, :IDX_DIM].reshape(s, td, IDX_DIM),
        'b_cmp_kv': pr['bcmp'].reshape(s, td, 2, HEAD_DIM),
        'b_slc_kv': pr['bslc'].reshape(s, td, 2, HEAD_DIM),
        'b_win_kv': win_all[:, win_all.shape[1] - keep:],
        'c_kv': pr['ckv'].reshape(s, td, 2, N_HEADS, HEAD_DIM),
        'c_logf': pr['logf'][:, :N_HEADS].reshape(s, td, N_HEADS),
        'd_kv': pr['dkv'].reshape(s, td, 2, N_HEADS, HEAD_DIM),
        'ffn_conv': conv.reshape(s, td, -1)[:, td - (CONV_W - 1):],
    }
    return y2.reshape(s, td, d), new, (o_a, o_b, o_c, o_d)


_STATE_KEYS = ('a_kv', 'a_kidx', 'b_cmp_kv', 'b_slc_kv', 'b_win_kv', 'c_kv', 'c_logf', 'd_kv', 'ffn_conv')


def kernel(x_prompt, x_sample, cache_a_kv, cache_a_kidx, cache_b_cmp_kv, cache_b_slc_kv, state_b_win_kv,
           cache_c_kv, cache_c_logf, cache_d_kv, state_ffn_conv, page_table, rel_bias, norm_attn, w_in,
           b_forget, qk_gain, cmp_w, cmp_pe, w_branch, w_out, norm_ffn, w_up, conv_w, conv_b, w_down):
    params = dict(norm_attn=norm_attn, w_in=w_in, b_forget=b_forget, qk_gain=qk_gain, cmp_w=cmp_w, cmp_pe=cmp_pe,
                  w_branch=w_branch, w_out=w_out, norm_ffn=norm_ffn, w_up=w_up, conv_w=conv_w, conv_b=conv_b,
                  w_down=w_down)
    depth, n_pool = w_in.shape[0], cache_a_kv.shape[1]
    n_seq, n_pages = page_table.shape
    assert cache_a_kv.shape[2] == PAGE
    pg = PAGES_PER_STEP if n_pages % PAGES_PER_STEP == 0 else 1
    nstep = n_pages // pg
    npast = n_pages * PAGE
    def token_last(a):
        nd = a.ndim
        return jnp.transpose(a, (0, 1) + tuple(range(3, nd)) + (2,)).reshape(a.shape[0], a.shape[1], -1, a.shape[2])

    caches = {
        'a_kv': token_last(cache_a_kv), 'a_kidx': token_last(cache_a_kidx),
        'b_cmp': cache_b_cmp_kv.reshape(depth, n_pool, PAGE // CMP_STRIDE, CMP_STRIDE * 2 * HEAD_DIM),
        'b_slc': token_last(cache_b_slc_kv), 'b_win': state_b_win_kv, 'b_win_t': token_last(state_b_win_kv),
        'c_kv': token_last(cache_c_kv), 'd_kv': token_last(cache_d_kv),
        'c_lft': jnp.pad(token_last(cache_c_logf.astype(F32)), ((0, 0), (0, 0), (0, SUBLANES - N_HEADS), (0, 0))),
    }
    tabs_p = _prompt_tables(rel_bias, x_prompt.shape[1])
    tabs_d = _decode_tables(rel_bias, npast, pg)
    y_p, y_s = x_prompt, x_sample
    new_p = {k: [] for k in _STATE_KEYS}
    new_s = {k: [] for k in _STATE_KEYS}
    for l in range(depth):
        lw = _prep_layer(l, params, DENSE_TM)
        y_p, st, _ = _layer_prompt(y_p, lw, tabs_p)
        for k in _STATE_KEYS:
            new_p[k].append(st[k])
        y_s, st, _ = _layer_decode(y_s, lw, tabs_d, l, caches, state_ffn_conv[l], page_table, nstep, pg)
        for k in _STATE_KEYS:
            new_s[k].append(st[k])
    sp = {k: jnp.stack(v) for k, v in new_p.items()}
    ss = {k: jnp.stack(v) for k, v in new_s.items()}
    return (y_p, y_s,
            sp['a_kv'], ss['a_kv'], sp['a_kidx'], ss['a_kidx'],
            sp['b_cmp_kv'], ss['b_cmp_kv'], sp['b_slc_kv'], ss['b_slc_kv'],
            sp['b_win_kv'], ss['b_win_kv'], sp['c_kv'], ss['c_kv'],
            sp['c_logf'], ss['c_logf'], sp['d_kv'], ss['d_kv'],
            sp['ffn_conv'], ss['ffn_conv'])
```

```python
import functools
import math

import jax
import jax.numpy as jnp
import numpy as np
from jax import lax
from jax.experimental import pallas as pl
from jax.experimental.pallas import tpu as pltpu

HEAD_DIM = 64
N_HEADS = 4
MIX_W = N_HEADS * HEAD_DIM
IDX_HEADS = 8
IDX_DIM = 64
DSA_TOPK = 256
CMP_LEN = 32
CMP_STRIDE = 16
SEL_BLOCK = 64
SEL_SHIFT = 6
HEAD_SHIFT = 6
N_SEL = 16
WINDOW = 512
N_BUCKETS = 32
MAX_DIST = 128
CONV_W = 3
EPS = 1e-6
NEG = -1e30
FORCE = 1e4
QK_SCALE = HEAD_DIM ** -0.5

LANES = 128
SUBLANES = 8
VMEM_LIMIT = 56 * 1024 * 1024
INT_MIN = -2 ** 31

DENSE_TM = 512
TQ = 128
TKB = 128
PAGES_PER_STEP = 32

BF = jnp.bfloat16
F32 = jnp.float32

C_AQ, C_AKV, C_AQIDX, C_AMISC = 0, 256, 384, 896
C_BQ, C_BCMP, C_BSLC, C_BWIN, C_BGATE = 1024, 1280, 1408, 1536, 1664
C_CQ, C_CK, C_CV, C_CF = 1792, 2048, 2304, 2560
C_DQ, C_DKV = 2688, 2944
N_PROJ = 3456


def _cparams(sem):
    return pltpu.CompilerParams(dimension_semantics=sem, vmem_limit_bytes=VMEM_LIMIT)


def _dot(a, b):
    return jnp.dot(a, b, preferred_element_type=F32)


def _dot_nt(a, b):
    return lax.dot_general(a, b, (((1,), (1,)), ((), ())), preferred_element_type=F32)


def _split3(x):
    hi = x.astype(BF)
    r1 = x - hi.astype(F32)
    mid = r1.astype(BF)
    lo = (r1 - mid.astype(F32)).astype(BF)
    return hi, mid, lo


def _dot3(x, m):
    hi, mid, lo = _split3(x)
    return _dot(hi, m) + _dot(mid, m) + _dot(lo, m)


def _dot3_l(m, x):
    hi, mid, lo = _split3(x)
    return _dot(m, hi) + _dot(m, mid) + _dot(m, lo)


def _sortable(x):
    x = jnp.where(x == 0.0, 0.0, x)
    b = lax.bitcast_convert_type(x, jnp.int32)
    return b ^ ((b >> 31) & 0x7FFFFFFF)


def _log_sigmoid(v):
    return jnp.minimum(v, 0.0) - jnp.log1p(jnp.exp(-jnp.abs(v)))


def _rms_rows(x, g):
    ms = jnp.mean(x * x, axis=-1, keepdims=True)
    return (x * lax.rsqrt(ms + EPS)) * g


def _proj_kernel(x_ref, g_ref, w_ref, gain_ref, nmask_ref, bf_ref, bd_ref, tri_ref,
                 qa_o, akv_o, akv_b, qidx_o, amisc_o, kidx_b, qb_o, bcmp_o, bslc_o, bslc_b, bwin_o, bwin_b,
                 gates_o, qc_o, ckv_o, ckv_b, logf_o, call_o, qd_o, dkv_o, dkv_b, carry_scr,
                 *, tiles_per_seq, with_cumsum):
    h = _rms_rows(x_ref[...], g_ref[...]).astype(BF)

    def cols(c0, width):
        return _dot(h, w_ref[:, c0:c0 + width])

    def normed(c0, width):
        slab = cols(c0, width)
        ms = _dot((slab * slab).astype(BF), bd_ref[0:width, 0:width])
        scale = lax.rsqrt(ms + EPS) * gain_ref[:, c0:c0 + width]
        return slab * jnp.where(nmask_ref[:, c0:c0 + width] > 0.0, scale, 1.0)

    qa_o[...] = (normed(C_AQ, MIX_W) * QK_SCALE).astype(BF)
    akv = normed(C_AKV, 128)
    akv_o[...] = akv
    akv_b[...] = akv.astype(BF)
    qidx_o[...] = cols(C_AQIDX, IDX_HEADS * IDX_DIM).astype(BF)
    amisc = cols(C_AMISC, 128)
    amisc_o[...] = amisc
    kidx_b[...] = amisc[:, 0:IDX_DIM].astype(BF)
    qb_o[...] = (normed(C_BQ, MIX_W) * QK_SCALE).astype(BF)
    bcmp_o[...] = cols(C_BCMP, 128)
    bslc = normed(C_BSLC, 128)
    bslc_o[...] = bslc
    bslc_b[...] = bslc.astype(BF)
    bwin = normed(C_BWIN, 128)
    bwin_o[...] = bwin
    bwin_b[...] = bwin.astype(BF)
    gates_o[...] = jax.nn.sigmoid(cols(C_BGATE, 128))
    qc_o[...] = (normed(C_CQ, MIX_W) * QK_SCALE).astype(BF)
    ck = normed(C_CK, MIX_W)
    cv = cols(C_CV, MIX_W)
    ckv_o[:, 0:MIX_W] = ck
    ckv_o[:, MIX_W:2 * MIX_W] = cv
    ckv_b[:, 0:MIX_W] = ck.astype(BF)
    ckv_b[:, MIX_W:2 * MIX_W] = cv.astype(BF)
    logf = _log_sigmoid(cols(C_CF, 128) + bf_ref[...])
    logf_o[...] = logf
    if with_cumsum:
        t = pl.program_id(0)

        @pl.when(t % tiles_per_seq == 0)
        def _():
            carry_scr[...] = jnp.zeros_like(carry_scr)

        c = _dot3_l(tri_ref[...], logf) + carry_scr[0:1, :]
        call_o[...] = c
        carry_scr[...] = jnp.broadcast_to(c[-1:, :], carry_scr.shape)
    else:
        call_o[...] = logf
    qd_o[...] = (cols(C_DQ, MIX_W) * QK_SCALE).astype(BF)
    dkv = cols(C_DKV, 2 * MIX_W)
    dkv_o[...] = dkv
    dkv_b[...] = dkv.astype(BF)


def _proj(x2d, lw, *, tm, tiles_per_seq, with_cumsum):
    m, d = x2d.shape
    assert m % tm == 0
    row = lambda i: (i, 0)
    const = lambda i: (0, 0)

    def o(width, dtype):
        return jax.ShapeDtypeStruct((m, width), dtype), pl.BlockSpec((tm, width), row)

    outs = [o(MIX_W, BF), o(128, F32), o(128, BF), o(IDX_HEADS * IDX_DIM, BF), o(128, F32), o(IDX_DIM, BF),
            o(MIX_W, BF), o(128, F32), o(128, F32), o(128, BF), o(128, F32), o(128, BF),
            o(128, F32), o(MIX_W, BF), o(2 * MIX_W, F32), o(2 * MIX_W, BF), o(128, F32), o(128, F32),
            o(MIX_W, BF), o(2 * MIX_W, F32), o(2 * MIX_W, BF)]
    names = ('qa', 'akv', 'akv_b', 'qidx', 'amisc', 'kidx_b', 'qb', 'bcmp', 'bslc', 'bslc_b', 'bwin', 'bwin_b',
             'gates', 'qc', 'ckv', 'ckv_b', 'logf', 'call', 'qd', 'dkv', 'dkv_b')
    res = pl.pallas_call(
        functools.partial(_proj_kernel, tiles_per_seq=tiles_per_seq, with_cumsum=with_cumsum),
        grid=(m // tm,),
        in_specs=[pl.BlockSpec((tm, d), row), pl.BlockSpec((1, d), const), pl.BlockSpec((d, N_PROJ), const),
                  pl.BlockSpec((1, N_PROJ), const), pl.BlockSpec((1, N_PROJ), const), pl.BlockSpec((1, 128), const),
                  pl.BlockSpec((MIX_W, MIX_W), const), pl.BlockSpec((tm, tm), const)],
        out_specs=[s for _, s in outs],
        out_shape=[s for s, _ in outs],
        scratch_shapes=[pltpu.VMEM((SUBLANES, 128), F32)],
        compiler_params=_cparams(("arbitrary",)),
        name="proj",
    )(x2d, lw['g_attn'], lw['w_proj'], lw['gain'], lw['nmask'], lw['b_f'], lw['bd'], lw['tri'][:tm, :tm])
    return dict(zip(names, res))


def _merge_kernel(x_ref, g_ref, wg_ref, oa_ref, ob_ref, oc_ref, od_ref, wbr_ref, wo_ref, y_ref):
    x = x_ref[...]
    d = x.shape[1]
    h = _rms_rows(x, g_ref[...]).astype(BF)
    m = None
    for i, o_ref in enumerate((oa_ref, ob_ref, oc_ref, od_ref)):
        gate = jax.nn.sigmoid(_dot(h, wg_ref[:, i * d:(i + 1) * d]))
        term = gate * _dot(o_ref[...], wbr_ref[i])
        m = term if m is None else m + term
    y_ref[...] = x + _dot(m.astype(BF), wo_ref[...])


def _merge(x2d, o_a, o_b, o_c, o_d, lw, *, tm):
    m, d = x2d.shape
    row = lambda i: (i, 0)
    const = lambda i: (0, 0)
    return pl.pallas_call(
        _merge_kernel,
        grid=(m // tm,),
        in_specs=[pl.BlockSpec((tm, d), row), pl.BlockSpec((1, d), const), pl.BlockSpec((d, N_HEADS * d), const),
                  pl.BlockSpec((tm, MIX_W), row), pl.BlockSpec((tm, MIX_W), row), pl.BlockSpec((tm, MIX_W), row),
                  pl.BlockSpec((tm, MIX_W), row), pl.BlockSpec((4, MIX_W, d), lambda i: (0, 0, 0)),
                  pl.BlockSpec((d, d), const)],
        out_specs=pl.BlockSpec((tm, d), row),
        out_shape=jax.ShapeDtypeStruct((m, d), F32),
        compiler_params=_cparams(("arbitrary",)),
        name="merge",
    )(x2d, lw['g_attn'], lw['w_gate'], o_a, o_b, o_c, o_d, lw['w_br'], lw['w_o'])


def _ffn_kernel(x_ref, g_ref, wup_ref, cw_ref, cb_ref, wdn_ref, st0_ref, st1_ref, y_ref, conv_o, prev_scr,
                *, carry_mode, tiles_per_seq, d_ff, cw):
    x = x_ref[...]
    tm = x.shape[0]
    h = _rms_rows(x, g_ref[...]).astype(BF)
    row = lax.broadcasted_iota(jnp.int32, (tm, cw), 0)
    if carry_mode:
        t = pl.program_id(0)

        @pl.when(t % tiles_per_seq == 0)
        def _():
            prev_scr[...] = jnp.zeros_like(prev_scr)
    else:
        rowmod = row % SUBLANES

    def conv_cols(c0):
        up = _dot(h, wup_ref[:, c0:c0 + cw])
        r1 = pltpu.roll(up, 1, axis=0)
        r2 = pltpu.roll(up, 2, axis=0)
        if carry_mode:
            p6 = prev_scr[6:7, c0:c0 + cw]
            p7 = prev_scr[7:8, c0:c0 + cw]
            u1 = jnp.where(row == 0, p7, r1)
            u2 = jnp.where(row == 0, p6, jnp.where(row == 1, p7, r2))
            prev_scr[:, c0:c0 + cw] = up[tm - SUBLANES:tm, :]
            conv_o[:, c0:c0 + cw] = up[tm - SUBLANES:tm, :]
        else:
            s0 = st0_ref[:, c0:c0 + cw]
            s1 = st1_ref[:, c0:c0 + cw]
            u1 = jnp.where(rowmod == 0, s1, r1)
            u2 = jnp.where(rowmod == 0, s0, jnp.where(rowmod == 1, s1, r2))
            conv_o[:, c0:c0 + cw] = up
        conv = (u2 * cw_ref[0:1, c0:c0 + cw] + u1 * cw_ref[1:2, c0:c0 + cw]) + up * cw_ref[2:3, c0:c0 + cw]
        return cb_ref[:, c0:c0 + cw] + conv

    acc = None
    for c in range(d_ff // cw):
        val = conv_cols(c * cw)
        gate = conv_cols(d_ff + c * cw)
        act = (gate * jax.nn.sigmoid(gate)) * val
        part = _dot(act.astype(BF), wdn_ref[c * cw:(c + 1) * cw, :])
        acc = part if acc is None else acc + part
    y_ref[...] = x + acc


def _ffn(x2d, lw, st0, st1, *, tm, carry_mode, tiles_per_seq):
    m, d = x2d.shape
    d_ff = lw['w_down'].shape[0]
    cw = 256
    assert d_ff % cw == 0 and m % tm == 0
    row = lambda i: (i, 0)
    const = lambda i: (0, 0)
    if carry_mode:
        n_seq = m // (tm * tiles_per_seq)
        conv_shape = jax.ShapeDtypeStruct((n_seq * SUBLANES, 2 * d_ff), F32)
        conv_spec = pl.BlockSpec((SUBLANES, 2 * d_ff), lambda i: (i // tiles_per_seq, 0))
        st_spec = pl.BlockSpec((SUBLANES, 2 * d_ff), const)
    else:
        conv_shape = jax.ShapeDtypeStruct((m, 2 * d_ff), F32)
        conv_spec = pl.BlockSpec((tm, 2 * d_ff), row)
        st_spec = pl.BlockSpec((tm, 2 * d_ff), row)
    return pl.pallas_call(
        functools.partial(_ffn_kernel, carry_mode=carry_mode, tiles_per_seq=tiles_per_seq, d_ff=d_ff, cw=cw),
        grid=(m // tm,),
        in_specs=[pl.BlockSpec((tm, d), row), pl.BlockSpec((1, d), const), pl.BlockSpec((d, 2 * d_ff), const),
                  pl.BlockSpec((CONV_W, 2 * d_ff), const), pl.BlockSpec((1, 2 * d_ff), const),
                  pl.BlockSpec((d_ff, d), const), st_spec, st_spec],
        out_specs=[pl.BlockSpec((tm, d), row), conv_spec],
        out_shape=[jax.ShapeDtypeStruct((m, d), F32), conv_shape],
        scratch_shapes=[pltpu.VMEM((SUBLANES, 2 * d_ff), F32)],
        compiler_params=_cparams(("arbitrary",)),
        name="ffn",
    )(x2d, lw['g_ffn'], lw['w_up'], lw['conv_w'], lw['conv_b'], lw['w_down'], st0, st1)


def _osm_update(s, mask, m, l, acc, v, v_t=False):
    if mask is not None:
        s = jnp.where(mask, s, NEG)
    m_new = jnp.maximum(m, jnp.max(s, axis=1, keepdims=True))
    p = jnp.exp(s - m_new)
    if mask is not None:
        p = jnp.where(mask, p, 0.0)
    alpha = jnp.exp(m - m_new)
    l = alpha * l + jnp.sum(p, axis=1, keepdims=True)
    pv = _dot_nt(p.astype(BF), v) if v_t else _dot(p.astype(BF), v)
    return m_new, l, alpha * acc + pv


def _osm_ref_init(m_ref, l_ref, acc_ref):
    m_ref[...] = jnp.full(m_ref.shape, NEG, F32)
    l_ref[...] = jnp.zeros(l_ref.shape, F32)
    acc_ref[...] = jnp.zeros(acc_ref.shape, F32)


def _osm_ref_step(s, mask, v, m_ref, l_ref, acc_ref):
    reps = s.shape[1] // LANES
    if mask is not None:
        s = jnp.where(mask, s, NEG)
    m_prev = m_ref[...]
    m_new = jnp.maximum(m_prev, jnp.max(s, axis=1, keepdims=True))
    p = jnp.exp(s - jnp.tile(m_new, (1, reps)))
    if mask is not None:
        p = jnp.where(mask, p, 0.0)
    alpha = jnp.exp(m_prev - m_new)
    l_ref[...] = alpha * l_ref[...] + jnp.sum(p, axis=1, keepdims=True)
    d = acc_ref.shape[1]
    a = alpha[:, 0:d] if d <= LANES else jnp.tile(alpha, (1, d // LANES))
    acc_ref[...] = a * acc_ref[...] + _dot(p.astype(BF), v)
    m_ref[...] = m_new


def _key_chunk(t):
    return 4 * TKB if t % (4 * TKB) == 0 else TKB


def _tile_bias(bias_ref, i, first_tile, ntiles):
    parts = [bias_ref[jnp.clip(i - (first_tile + u), 0, 2)].reshape(N_HEADS * TQ, TKB) for u in range(ntiles)]
    return parts[0] if ntiles == 1 else jnp.concatenate(parts, axis=1)


def _pad_lanes(q):
    return jnp.concatenate([q, jnp.zeros_like(q)], axis=1)


def _heads_to_rows(q):
    return jnp.concatenate([q[:, h * HEAD_DIM:(h + 1) * HEAD_DIM] for h in range(N_HEADS)], axis=0)


def _rows_to_heads(o, r):
    return jnp.concatenate([o[h * r:(h + 1) * r, :] for h in range(N_HEADS)], axis=1)


def _tile_rows(a, n):
    return jnp.concatenate([a] * n, axis=0)


def _unsortable(key):
    return lax.bitcast_convert_type(key ^ ((key >> 31) & 0x7FFFFFFF), F32)


VALUE_BISECTIONS = 14
KEY_BISECTIONS = 34


def _kth_search(count_ge, k, vmin, vmax, n_valid):
    lo0 = _sortable(vmin)
    hi0 = _sortable(vmax) + 1

    def done_of(lo, hi, c_lo):
        return (c_lo <= k) | (hi == lo + 1)

    def step(st, midpoint):
        lo, hi, c_lo, c_hi = st
        done = done_of(lo, hi, c_lo)
        cand = jnp.where(done, lo, jnp.clip(midpoint(lo, hi), lo + 1, hi - 1))
        c = count_ge(cand)
        up = (c >= k) & jnp.logical_not(done)
        down = (c < k) & jnp.logical_not(done)
        return jnp.where(up, cand, lo), jnp.where(down, cand, hi), jnp.where(up, c, c_lo), jnp.where(down, c, c_hi)

    def value_mid(lo, hi):
        return _sortable(0.5 * _unsortable(lo) + 0.5 * _unsortable(hi - 1))

    def key_mid(lo, hi):
        return (lo >> 1) + (hi >> 1) + (lo & hi & 1)

    zero = jnp.zeros_like(n_valid)
    c0 = count_ge(zero)
    c1 = count_ge(zero + 1)
    at_zero = (c0 >= k) & (c1 < k)
    pos = c1 >= k
    st = (jnp.where(at_zero, 0, jnp.where(pos, 1, lo0)), jnp.where(at_zero, 1, jnp.where(pos, hi0, 0)),
          jnp.where(at_zero, c0, jnp.where(pos, c1, n_valid)), jnp.where(at_zero, c1, jnp.where(pos, 0, c0)))
    st = lax.fori_loop(0, VALUE_BISECTIONS, lambda _, s: step(s, value_mid), st)

    def cond(carry):
        it, (lo, hi, c_lo, _) = carry
        pending = jnp.max(jnp.where(done_of(lo, hi, c_lo), 0.0, 1.0)) > 0.0
        return (it < KEY_BISECTIONS) & pending

    _, (lo, _, c_lo, c_hi) = lax.while_loop(cond, lambda c: (c[0] + 1, step(c[1], key_mid)), (jnp.int32(0), st))
    return lo, c_lo, c_hi


def _dsa_prompt_kernel(qidx_ref, amisc_ref, qa_ref, kidx_ref, akv_ref, bias_ref, o_ref, key_scr, m_scr, l_scr, acc_scr,
                       *, topk, kt):
    i = pl.program_id(1)
    chunk = kt // TKB
    nchunk = i // chunk + 1
    amisc = amisc_ref[...]
    qidx = qidx_ref[...]
    qh = [qidx[:, h * IDX_DIM:(h + 1) * IDX_DIM] for h in range(IDX_HEADS)]
    wb = [jnp.broadcast_to(amisc[:, IDX_DIM + h:IDX_DIM + h + 1], (TQ, LANES)) for h in range(IDX_HEADS)]
    qpos = i * TQ + lax.broadcasted_iota(jnp.int32, (TQ, kt), 0)
    kcol = lax.broadcasted_iota(jnp.int32, (TQ, kt), 1)

    def score_chunk(c, carry):
        vmax, vmin = carry
        off = pl.multiple_of(c * kt, kt)
        kb = kidx_ref[pl.ds(off, kt), :]
        acc = None
        for h in range(IDX_HEADS):
            term = jnp.maximum(_dot_nt(qh[h], kb), 0.0) * jnp.tile(wb[h], (1, chunk))
            acc = term if acc is None else acc + term
        visible = off + kcol <= qpos
        key_scr[:, pl.ds(off, kt)] = jnp.where(visible, _sortable(acc), INT_MIN)
        hi_part = jnp.where(visible, acc, -jnp.inf)
        lo_part = jnp.where(visible, acc, jnp.inf)
        for u in range(chunk):
            vmax = jnp.maximum(vmax, hi_part[:, u * TKB:(u + 1) * TKB])
            vmin = jnp.minimum(vmin, lo_part[:, u * TKB:(u + 1) * TKB])
        return vmax, vmin

    vmax, vmin = lax.fori_loop(0, nchunk, score_chunk,
                               (jnp.full((TQ, TKB), -jnp.inf, F32), jnp.full((TQ, TKB), jnp.inf, F32)))
    vmax = jnp.max(vmax, axis=1, keepdims=True)
    vmin = jnp.min(vmin, axis=1, keepdims=True)
    n_valid = i * TQ + lax.broadcasted_iota(jnp.int32, (TQ, 1), 0) + 1

    cw = kt

    def count(pred):
        def body(c, acc):
            kc = key_scr[:, pl.ds(pl.multiple_of(c * cw, cw), cw)]
            hit = jnp.where(pred(kc, c * cw), 1.0, 0.0)
            part = hit[:, 0:TKB]
            for u in range(1, chunk):
                part = part + hit[:, u * TKB:(u + 1) * TKB]
            return acc + part
        acc = lax.fori_loop(0, nchunk, body, jnp.zeros((TQ, TKB), F32))
        return jnp.sum(acc, axis=1, keepdims=True).astype(jnp.int32)

    thr, c_lo, c_hi = _kth_search(lambda cand: count(lambda kc, o: kc >= cand), topk, vmin, vmax, n_valid)

    need = topk - c_hi
    tie = c_lo > topk
    any_tie = jnp.max(jnp.where(tie, 1.0, 0.0)) > 0.0
    lane_c = lax.broadcasted_iota(jnp.int32, (TQ, cw), 1)

    @pl.when(any_tie)
    def _():
        nbits = max(1, int(math.ceil(math.log2(key_scr.shape[1]))))

        def bit_body(b, lo):
            cand = lo + jnp.left_shift(jnp.int32(1), nbits - 1 - b)
            cnt = count(lambda kc, o: (kc == thr) & ((lane_c + o) < cand))
            return jnp.where(cnt < need, cand, lo)

        jmax = lax.fori_loop(0, nbits, bit_body, jnp.zeros((TQ, 1), jnp.int32))

        def demote(c, carry):
            sl = pl.ds(pl.multiple_of(c * cw, cw), cw)
            kc = key_scr[:, sl]
            drop = tie & (kc == thr) & ((lane_c + c * cw) > jmax)
            key_scr[:, sl] = jnp.where(drop, thr - 1, kc)
            return carry

        lax.fori_loop(0, nchunk, demote, 0)

    thr_eff = jnp.maximum(thr, INT_MIN + 1)
    q4 = _pad_lanes(_heads_to_rows(qa_ref[...]))
    _osm_ref_init(m_scr, l_scr, acc_scr)

    def att_chunk(c, carry):
        off = pl.multiple_of(c * kt, kt)
        kv = akv_ref[pl.ds(off, kt), :]
        sel = key_scr[:, pl.ds(off, kt)] >= thr_eff
        s = _dot_nt(q4, kv) + _tile_bias(bias_ref, i, c * chunk, chunk)
        _osm_ref_step(s, _tile_rows(sel, N_HEADS), kv, m_scr, l_scr, acc_scr)
        return carry

    lax.fori_loop(0, nchunk, att_chunk, 0)
    o4 = acc_scr[...] / l_scr[...]
    o_ref[...] = _rows_to_heads(o4[:, HEAD_DIM:2 * HEAD_DIM], TQ).astype(BF)


def _dsa_prompt(pr, bias, b, t):
    nq = t // TQ
    topk = min(DSA_TOPK, t // 4)
    rows = N_HEADS * TQ
    q_spec = lambda w: pl.BlockSpec((TQ, w), lambda bi, qi: (bi * nq + qi, 0))
    kv_spec = lambda w: pl.BlockSpec((t, w), lambda bi, qi: (bi, 0))
    return pl.pallas_call(
        functools.partial(_dsa_prompt_kernel, topk=topk, kt=_key_chunk(t)),
        grid=(b, nq),
        in_specs=[q_spec(IDX_HEADS * IDX_DIM), q_spec(128), q_spec(MIX_W), kv_spec(IDX_DIM), kv_spec(128),
                  pl.BlockSpec((3, N_HEADS, TQ, TKB), lambda bi, qi: (0, 0, 0, 0))],
        out_specs=q_spec(MIX_W),
        out_shape=jax.ShapeDtypeStruct((b * t, MIX_W), BF),
        scratch_shapes=[pltpu.VMEM((TQ, t), jnp.int32), pltpu.VMEM((rows, LANES), F32),
                        pltpu.VMEM((rows, LANES), F32), pltpu.VMEM((rows, LANES), F32)],
        compiler_params=_cparams(("arbitrary", "arbitrary")),
        name="dsa_prompt",
    )(pr['qidx'], pr['amisc'], pr['qa'], pr['kidx_b'], pr['akv_b'], bias)


def _compress_kernel(*refs, n_in):
    x_refs, (pe_ref, w_ref, o_ref) = refs[:n_in], refs[n_in:]
    x = x_refs[0][...] if n_in == 1 else jnp.concatenate([r[...] for r in x_refs], axis=0)
    o_ref[:, 0:128] = _dot((x + pe_ref[0:1, :]).astype(BF), w_ref[0])
    o_ref[:, 128:256] = _dot((x + pe_ref[1:2, :]).astype(BF), w_ref[1])


def _compress_weights(lw):
    w4 = lw['cmp_w'].reshape(2, CMP_LEN, HEAD_DIM, HEAD_DIM)
    pe = lw['cmp_pe']

    def half(rs):
        wk, wv = w4[0, rs], w4[1, rs]
        z = jnp.zeros_like(wk)
        rows = jnp.stack([jnp.concatenate([wk, z], axis=-1), jnp.concatenate([z, wv], axis=-1)], axis=1)
        return rows.reshape(CMP_STRIDE * 2 * HEAD_DIM, 2 * HEAD_DIM)

    top, bot = slice(0, CMP_STRIDE), slice(CMP_STRIDE, CMP_LEN)
    w = jnp.stack([half(top), half(bot)]).astype(BF)
    pef = jnp.stack([jnp.transpose(pe[:, top], (1, 0, 2)).reshape(-1), jnp.transpose(pe[:, bot], (1, 0, 2)).reshape(-1)])
    return pef, w


def _compress_dense(chunks, lw, rows_per_step):
    n, width = chunks.shape
    pef, w = _compress_weights(lw)
    return pl.pallas_call(
        functools.partial(_compress_kernel, n_in=1),
        grid=(n // rows_per_step,),
        in_specs=[pl.BlockSpec((rows_per_step, width), lambda i: (i, 0)), pl.BlockSpec((2, width), lambda i: (0, 0)),
                  pl.BlockSpec((2, width, 128), lambda i: (0, 0, 0))],
        out_specs=pl.BlockSpec((rows_per_step, 256), lambda i: (i, 0)),
        out_shape=jax.ShapeDtypeStruct((n, 256), F32),
        compiler_params=_cparams(("arbitrary",)),
        name="compress_prompt",
    )(chunks, pef, w)


def _combine_compressed(ab, gk):
    n = ab.shape[0]
    kv = ab[:, 0:128] + pltpu.roll(ab[:, 128:256], n - 1, axis=0)
    ck_raw = kv[:, 0:HEAD_DIM]
    ms = jnp.mean(ck_raw * ck_raw, axis=-1, keepdims=True)
    ck = (ck_raw * lax.rsqrt(ms + EPS)) * gk
    return ck.astype(BF), kv[:, HEAD_DIM:2 * HEAD_DIM].astype(BF)


def _masked_softmax_rows(s, valid):
    s = jnp.where(valid, s, NEG)
    m = jnp.max(s, axis=1, keepdims=True)
    p = jnp.where(valid, jnp.exp(s - m), 0.0)
    l = jnp.sum(p, axis=1, keepdims=True)
    return p * (1.0 / jnp.where(l > 0.0, l, 1.0))


def _gate_cols(g, c):
    return jnp.concatenate([g[:, h * 3 + c:h * 3 + c + 1] for h in range(N_HEADS)], axis=0)


def _nsa_prompt_kernel(qb_ref, gates_ref, ab_ref, gk_ref, cbias_ref, gmat_ref, emat_ref, tie_ref, slc_ref, win_ref, bias_ref,
                       o_ref, ck_scr, cv_scr, selm_scr, m_scr, l_scr, acc_scr, *, n_sel, ns, nch, kt, wt, nq):
    i = pl.program_id(1)

    @pl.when(i == 0)
    def _():
        ck, cv = _combine_compressed(ab_ref[...], gk_ref[...])
        ck_scr[...] = ck
        cv_scr[...] = cv

    q4 = _heads_to_rows(qb_ref[...])
    qpos = i * TQ + lax.broadcasted_iota(jnp.int32, (TQ, 1), 0)

    cend = lax.broadcasted_iota(jnp.int32, (TQ, nch), 1) * CMP_STRIDE + (CMP_LEN - 1)
    cvalid = cend <= qpos
    s = _dot_nt(q4, ck_scr[...]) + cbias_ref[...].reshape(N_HEADS * TQ, nch)
    pc = _masked_softmax_rows(s, _tile_rows(cvalid, N_HEADS))
    o_c = _dot(pc.astype(BF), cv_scr[...])
    pcs = pc[0:TQ] + pc[TQ:2 * TQ] + pc[2 * TQ:3 * TQ] + pc[3 * TQ:4 * TQ]
    imp = _dot3(pcs, gmat_ref[...])

    j_io = lax.broadcasted_iota(jnp.int32, (TQ, ns), 1)
    cur = qpos >> SEL_SHIFT
    gap = cur - j_io
    score = jnp.where(gap >= 0, jnp.where((j_io == 0) | (gap <= 1), FORCE, imp), NEG)
    rank = jnp.zeros((TQ, ns), F32)
    for jj in range(ns):
        col = score[:, jj:jj + 1]
        rank = rank + (jnp.where(col > score, 1.0, 0.0) + jnp.where(col == score, tie_ref[jj:jj + 1, :], 0.0))
    sel = jnp.where(gap >= 0, rank, float(ns)) < n_sel
    selm_scr[...] = _dot(jnp.where(sel, 1.0, 0.0).astype(BF), emat_ref[...])

    q4p = _pad_lanes(q4)
    chunk = kt // TKB
    qrow = i * TQ + lax.broadcasted_iota(jnp.int32, (TQ, kt), 0)
    kcol = lax.broadcasted_iota(jnp.int32, (TQ, kt), 1)
    _osm_ref_init(m_scr, l_scr, acc_scr)

    def sel_chunk(c, carry):
        off = pl.multiple_of(c * kt, kt)
        kv = slc_ref[pl.ds(off, kt), :]
        mask = (selm_scr[:, pl.ds(off, kt)] > 0.5) & (off + kcol <= qrow)
        s = _dot_nt(q4p, kv) + _tile_bias(bias_ref, i, c * chunk, chunk)
        _osm_ref_step(s, _tile_rows(mask, N_HEADS), kv, m_scr, l_scr, acc_scr)
        return carry

    lax.fori_loop(0, i // chunk + 1, sel_chunk, 0)
    o_s = (acc_scr[...] / l_scr[...])[:, HEAD_DIM:2 * HEAD_DIM]

    w0 = jnp.clip(i - (wt - 1), 0, nq - wt)
    woff = pl.multiple_of(w0 * TKB, TKB)
    kv = win_ref[pl.ds(woff, wt * TKB), :]
    wd = (i * TQ + lax.broadcasted_iota(jnp.int32, (TQ, wt * TKB), 0)) - (
        woff + lax.broadcasted_iota(jnp.int32, (TQ, wt * TKB), 1))
    s = _dot_nt(q4p, kv) + _tile_bias(bias_ref, i, w0, wt)
    pw = _masked_softmax_rows(s, _tile_rows((wd >= 0) & (wd <= WINDOW), N_HEADS))
    o_w = _dot(pw.astype(BF), kv)[:, HEAD_DIM:2 * HEAD_DIM]

    g = gates_ref[...]
    o = _gate_cols(g, 0) * o_c + _gate_cols(g, 1) * o_s + _gate_cols(g, 2) * o_w
    o_ref[...] = _rows_to_heads(o, TQ).astype(BF)


def _nsa_prompt(pr, ab, lw, tabs, b, t):
    nq = t // TQ
    nch = t // CMP_STRIDE
    ns = t // SEL_BLOCK
    n_sel = min(N_SEL, ns)
    q_spec = lambda w: pl.BlockSpec((TQ, w), lambda bi, qi: (bi * nq + qi, 0))
    kv_spec = lambda w: pl.BlockSpec((t, w), lambda bi, qi: (bi, 0))
    const2 = lambda bi, qi: (0, 0)
    return pl.pallas_call(
        functools.partial(_nsa_prompt_kernel, n_sel=n_sel, ns=ns, nch=nch, kt=_key_chunk(t),
                          wt=min(WINDOW // TKB + 1, nq), nq=nq),
        grid=(b, nq),
        in_specs=[q_spec(MIX_W), q_spec(128), pl.BlockSpec((nch, 256), lambda bi, qi: (bi, 0)),
                  pl.BlockSpec((1, HEAD_DIM), const2),
                  pl.BlockSpec((None, N_HEADS, TQ, nch), lambda bi, qi: (qi, 0, 0, 0)),
                  pl.BlockSpec((nch, ns), const2), pl.BlockSpec((ns, t), const2), pl.BlockSpec((ns, ns), const2),
                  kv_spec(128), kv_spec(128),
                  pl.BlockSpec((3, N_HEADS, TQ, TKB), lambda bi, qi: (0, 0, 0, 0))],
        out_specs=q_spec(MIX_W),
        out_shape=jax.ShapeDtypeStruct((b * t, MIX_W), BF),
        scratch_shapes=[pltpu.VMEM((nch, HEAD_DIM), BF), pltpu.VMEM((nch, HEAD_DIM), BF), pltpu.VMEM((TQ, t), F32),
                        pltpu.VMEM((N_HEADS * TQ, LANES), F32), pltpu.VMEM((N_HEADS * TQ, LANES), F32),
                        pltpu.VMEM((N_HEADS * TQ, LANES), F32)],
        compiler_params=_cparams(("arbitrary", "arbitrary")),
        name="nsa_prompt",
    )(pr['qb'], pr['gates'], ab, lw['gk_b'], tabs['cbias_p'], tabs['gmat_p'], tabs['emat_p'], tabs['tie_p'],
      pr['bslc_b'], pr['bwin_b'], tabs['nsa_tiles'])


def _fox_prompt_kernel(q_ref, cq_ref, ckv_ref, ckt_ref, o_ref, m_scr, l_scr, acc_scr, *, kt):
    i = pl.program_id(1)
    chunk = kt // TKB
    nfull = i // chunk
    q4 = _head_block_diag(q_ref[...])
    cq = cq_ref[...]
    cq4 = jnp.concatenate([jnp.broadcast_to(cq[:, h:h + 1], (TQ, LANES)) for h in range(N_HEADS)], axis=0)
    qpos = i * TQ + lax.broadcasted_iota(jnp.int32, (TQ, kt), 0)
    kcol = lax.broadcasted_iota(jnp.int32, (TQ, kt), 1)
    _osm_ref_init(m_scr, l_scr, acc_scr)

    def do_chunk(c, masked):
        off = pl.multiple_of(c * kt, kt)
        kv = ckv_ref[pl.ds(off, kt), :]
        ck4 = jnp.concatenate([jnp.broadcast_to(ckt_ref[h:h + 1, pl.ds(off, kt)], (TQ, kt)) for h in range(N_HEADS)],
                              axis=0)
        s = (_dot_nt(q4, kv[:, 0:MIX_W]) + jnp.tile(cq4, (1, chunk))) - ck4
        mask = _tile_rows(off + kcol <= qpos, N_HEADS) if masked else None
        _osm_ref_step(s, mask, kv[:, MIX_W:2 * MIX_W], m_scr, l_scr, acc_scr)

    def body(c, carry):
        do_chunk(c, False)
        return carry

    lax.fori_loop(0, nfull, body, 0)
    do_chunk(nfull, True)
    o4 = acc_scr[...] / jnp.tile(l_scr[...], (1, MIX_W // LANES))
    o_ref[...] = _head_diag_pick(o4, TQ).astype(BF)


def _fox_prompt(pr, ckt, b, t):
    nq = t // TQ
    rows = N_HEADS * TQ
    q_spec = lambda w: pl.BlockSpec((TQ, w), lambda bi, qi: (bi * nq + qi, 0))
    return pl.pallas_call(
        functools.partial(_fox_prompt_kernel, kt=_key_chunk(t)),
        grid=(b, nq),
        in_specs=[q_spec(MIX_W), q_spec(128), pl.BlockSpec((t, 2 * MIX_W), lambda bi, qi: (bi, 0)),
                  pl.BlockSpec((SUBLANES, t), lambda bi, qi: (bi, 0))],
        out_specs=q_spec(MIX_W),
        out_shape=jax.ShapeDtypeStruct((b * t, MIX_W), BF),
        scratch_shapes=[pltpu.VMEM((rows, LANES), F32), pltpu.VMEM((rows, LANES), F32), pltpu.VMEM((rows, MIX_W), F32)],
        compiler_params=_cparams(("arbitrary", "arbitrary")),
        name="fox_prompt",
    )(pr['qc'], pr['call'], pr['ckv_b'], ckt)


def _stick_terms(z):
    e = jnp.log(1.0 + jnp.exp(-jnp.abs(z)))
    nsp = -(jnp.maximum(z, 0.0) + e)
    return nsp, z + nsp


def _suffix_sums(x, uaug, ntiles):
    r = x.shape[0]
    total = lambda u: jnp.sum(x[:, u * TKB:(u + 1) * TKB], axis=1, keepdims=True)
    if ntiles % 2:
        assert ntiles == 1
        hi = x.astype(BF)
        lo = (x - hi.astype(F32)).astype(BF)
        return [(_dot(hi, uaug[0:TKB, 0:TKB]) + _dot(lo, uaug[0:TKB, 0:TKB]), total(0))]
    pairs = ntiles // 2
    stack = x if pairs == 1 else jnp.concatenate([x[:, j * 2 * TKB:(j + 1) * 2 * TKB] for j in range(pairs)], axis=0)
    hi = stack.astype(BF)
    lo = (stack - hi.astype(F32)).astype(BF)
    rs = _dot(hi, uaug) + _dot(lo, uaug)
    return [(rs[(u // 2) * r:(u // 2 + 1) * r, (u % 2) * TKB:(u % 2 + 1) * TKB], total(u)) for u in range(ntiles)]


def _stick_prompt_kernel(q_ref, dkv_ref, u_ref, o_ref, acc_scr, run_scr, *, kt):
    i = pl.program_id(1)
    chunk = kt // TKB
    last = i // chunk
    q4 = _head_block_diag(q_ref[...])
    uaug = u_ref[...]
    qpos = i * TQ + lax.broadcasted_iota(jnp.int32, (TQ, kt), 0)
    kcol = lax.broadcasted_iota(jnp.int32, (TQ, kt), 1)
    acc_scr[...] = jnp.zeros(acc_scr.shape, F32)
    run_scr[...] = jnp.zeros(run_scr.shape, F32)

    def do_chunk(c, masked):
        off = pl.multiple_of(c * kt, kt)
        kv = dkv_ref[pl.ds(off, kt), :]
        nsp, lsig = _stick_terms(_dot_nt(q4, kv[:, 0:MIX_W]))
        if masked:
            ok = _tile_rows(off + kcol < qpos, N_HEADS)
            nsp = jnp.where(ok, nsp, 0.0)
        sums = _suffix_sums(nsp, uaug, chunk)
        run = run_scr[...]
        pieces = [None] * chunk
        for u in reversed(range(chunk)):
            pieces[u] = sums[u][0] + run
            run = run + sums[u][1]
        run_scr[...] = run
        a = jnp.exp(lsig + (pieces[0] if chunk == 1 else jnp.concatenate(pieces, axis=1)))
        if masked:
            a = jnp.where(ok, a, 0.0)
        acc_scr[...] = acc_scr[...] + _dot(a.astype(BF), kv[:, MIX_W:2 * MIX_W])

    def body(jj, carry):
        @pl.when(jj == 0)
        def _():
            do_chunk(last, True)

        @pl.when(jj > 0)
        def _():
            do_chunk(last - jj, False)

        return carry

    lax.fori_loop(0, last + 1, body, 0)
    o_ref[...] = _head_diag_pick(acc_scr[...], TQ).astype(BF)


def _stick_prompt(pr, uaug, b, t):
    nq = t // TQ
    rows = N_HEADS * TQ
    q_spec = lambda w: pl.BlockSpec((TQ, w), lambda bi, qi: (bi * nq + qi, 0))
    return pl.pallas_call(
        functools.partial(_stick_prompt_kernel, kt=_key_chunk(t)),
        grid=(b, nq),
        in_specs=[q_spec(MIX_W), pl.BlockSpec((t, 2 * MIX_W), lambda bi, qi: (bi, 0)),
                  pl.BlockSpec((2 * TKB, 2 * TKB), lambda bi, qi: (0, 0))],
        out_specs=q_spec(MIX_W),
        out_shape=jax.ShapeDtypeStruct((b * t, MIX_W), BF),
        scratch_shapes=[pltpu.VMEM((rows, MIX_W), F32), pltpu.VMEM((rows, LANES), F32)],
        compiler_params=_cparams(("arbitrary", "arbitrary")),
        name="stick_prompt",
    )(pr['qd'], pr['dkv_b'], uaug)


PAGE = 128
TD = SUBLANES


def _page_specs(page_shape, l, pg, page_of):
    tail = (0,) * len(page_shape)

    def spec(i):
        def index_map(*args):
            pt = args[-1]
            return (l, pt[args[0], page_of(*args[1:-1], i)]) + tail
        return pl.BlockSpec((None, None) + page_shape, index_map)

    return [spec(i) for i in range(pg)]


def _pad_rows(a, n):
    return jnp.concatenate([a, jnp.zeros((n - a.shape[0], a.shape[1]), a.dtype)], axis=0)


def _topk_select_ref(key_ref, k):
    rows, width = key_ref.shape

    def count(pred):
        return jnp.sum(jnp.where(pred(key_ref[...]), 1.0, 0.0), axis=1, keepdims=True).astype(jnp.int32)

    keys = key_ref[...]
    valid = keys > INT_MIN
    vals = _unsortable(keys)
    vmax = jnp.max(jnp.where(valid, vals, -jnp.inf), axis=1, keepdims=True)
    vmin = jnp.min(jnp.where(valid, vals, jnp.inf), axis=1, keepdims=True)
    thr, c_lo, c_hi = _kth_search(lambda cand: count(lambda kk: kk >= cand), k, vmin, vmax,
                                  count(lambda kk: kk > INT_MIN))
    need = k - c_hi
    tie = c_lo > k
    any_tie = jnp.max(jnp.where(tie, 1.0, 0.0)) > 0.0

    @pl.when(any_tie)
    def _():
        idx = lax.broadcasted_iota(jnp.int32, (rows, width), 1)
        nbits = max(1, int(math.ceil(math.log2(width))))

        def bit_body(b, lo):
            cand = lo + jnp.left_shift(jnp.int32(1), nbits - 1 - b)
            cnt = count(lambda kk: (kk == thr) & (idx < cand))
            return jnp.where(cnt < need, cand, lo)

        jmax = lax.fori_loop(0, nbits, bit_body, jnp.zeros((rows, 1), jnp.int32))
        kk = key_ref[...]
        key_ref[...] = jnp.where(tie & (kk == thr) & (idx > jmax), thr - 1, kk)

    return jnp.maximum(thr, INT_MIN + 1)


def _osm_scratch_update(s, mask, v, m_scr, l_scr, acc_scr, v_t=False):
    m, l, acc = _osm_update(s, mask, m_scr[...], l_scr[...], acc_scr[...], v, v_t)
    m_scr[...] = m
    l_scr[...] = l
    acc_scr[...] = acc


def _osm_scratch_init(m_scr, l_scr, acc_scr):
    m_scr[...] = jnp.full(m_scr.shape, NEG, F32)
    l_scr[...] = jnp.zeros(l_scr.shape, F32)
    acc_scr[...] = jnp.zeros(acc_scr.shape, F32)


def _head_block_diag(q):
    lane = lax.broadcasted_iota(jnp.int32, q.shape, 1)
    return jnp.concatenate([jnp.where(lane >> HEAD_SHIFT == h, q, jnp.zeros_like(q)) for h in range(N_HEADS)], axis=0)


def _head_diag_pick(o, r):
    lane = lax.broadcasted_iota(jnp.int32, (r, o.shape[1]), 1)
    out = None
    for h in range(N_HEADS):
        part = jnp.where(lane >> HEAD_SHIFT == h, o[h * r:(h + 1) * r, :], 0.0)
        out = part if out is None else out + part
    return out


def _causal_new_mask(strict):
    lane = lax.broadcasted_iota(jnp.int32, (TD, PAGE), 1)
    rowi = lax.broadcasted_iota(jnp.int32, (TD, PAGE), 0)
    return lane < rowi if strict else lane <= rowi


def _compress_paged(cache, page_table, l, lw, nstep, pg):
    s = page_table.shape[0]
    pef, w = _compress_weights(lw)
    width = pef.shape[1]
    rows = PAGE // CMP_STRIDE

    def body(pt_ref, *refs):
        _compress_kernel(*refs, n_in=pg)

    return pl.pallas_call(
        body,
        grid_spec=pltpu.PrefetchScalarGridSpec(
            num_scalar_prefetch=1, grid=(s, nstep),
            in_specs=_page_specs((rows, width), l, pg, lambda p, i: p * pg + i)
            + [pl.BlockSpec((2, width), lambda si, p, pt: (0, 0)), pl.BlockSpec((2, width, 128), lambda si, p, pt: (0, 0, 0))],
            out_specs=pl.BlockSpec((None, pg * rows, 256), lambda si, p, pt: (si, p, 0))),
        out_shape=jax.ShapeDtypeStruct((s, nstep * pg * rows, 256), F32),
        compiler_params=_cparams(("arbitrary", "arbitrary")),
        name="compress_decode",
    )(page_table, *([cache] * pg), pef, w)


def _dsa_decode_kernel(pt_ref, qidx_ref, amisc_ref, qa_ref, kidxn_ref, akvn_ref, blast_ref, bfar_ref, bnew_ref, *rest,
                       topk, nstep, pg):
    kid_refs, kv_refs = rest[:pg], rest[pg:2 * pg]
    o_ref, key_scr, thr_scr, m_scr, l_scr, acc_scr = rest[2 * pg:]
    ph, p = pl.program_id(1), pl.program_id(2)
    wstep = pg * PAGE
    npast = nstep * wstep
    last = p == nstep - 1

    @pl.when(ph == 0)
    def _():
        qidx = qidx_ref[...]
        w = amisc_ref[:, IDX_DIM:IDX_DIM + IDX_HEADS]
        q64 = jnp.concatenate([qidx[:, h * IDX_DIM:(h + 1) * IDX_DIM] for h in range(IDX_HEADS)], axis=0)

        def scores(sc):
            acc = None
            for h in range(IDX_HEADS):
                term = jnp.maximum(sc[h * TD:(h + 1) * TD, :], 0.0) * w[:, h:h + 1]
                acc = term if acc is None else acc + term
            return acc

        kid_t = jnp.concatenate([r[...].astype(BF) for r in kid_refs], axis=1)
        key_scr[:, pl.ds(pl.multiple_of(p * wstep, wstep), wstep)] = _sortable(scores(_dot(q64, kid_t)))

        @pl.when(last)
        def _():
            acc = scores(_dot_nt(q64, _pad_rows(kidxn_ref[...], PAGE)))
            key_scr[:, npast:npast + PAGE] = jnp.where(_causal_new_mask(False), _sortable(acc), INT_MIN)
            thr_scr[...] = jnp.broadcast_to(_topk_select_ref(key_scr, topk), thr_scr.shape)

    @pl.when(ph == 1)
    def _():
        @pl.when(p == 0)
        def _():
            _osm_scratch_init(m_scr, l_scr, acc_scr)

        thr = thr_scr[:, 0:1]
        q4 = _heads_to_rows(qa_ref[...])
        k_t = jnp.concatenate([r[0:HEAD_DIM, :].astype(BF) for r in kv_refs], axis=1)
        v_t = jnp.concatenate([r[HEAD_DIM:2 * HEAD_DIM, :].astype(BF) for r in kv_refs], axis=1)
        bias = jnp.where(last, blast_ref[...], bfar_ref[...])
        s = _dot(q4, k_t) + bias
        sel = key_scr[:, pl.ds(pl.multiple_of(p * wstep, wstep), wstep)] >= thr
        _osm_scratch_update(s, _tile_rows(sel, N_HEADS), v_t, m_scr, l_scr, acc_scr, v_t=True)

        @pl.when(last)
        def _():
            kvn = _pad_rows(akvn_ref[...], PAGE)
            s = _dot_nt(q4, kvn[:, 0:HEAD_DIM]) + bnew_ref[...]
            sel = key_scr[:, npast:npast + PAGE] >= thr
            m, l, acc = _osm_update(s, _tile_rows(sel, N_HEADS), m_scr[...], l_scr[...], acc_scr[...],
                                    kvn[:, HEAD_DIM:2 * HEAD_DIM])
            o_ref[...] = _rows_to_heads(acc / l, TD).astype(BF)


def _seq_spec(width, ngrid):
    if ngrid == 2:
        return pl.BlockSpec((None, TD, width), lambda si, p, pt: (si, 0, 0))
    return pl.BlockSpec((None, TD, width), lambda si, ph, p, pt: (si, 0, 0))


def _const_spec(shape, ngrid):
    zeros = (0,) * len(shape)
    if ngrid == 2:
        return pl.BlockSpec(shape, lambda si, p, pt: zeros)
    return pl.BlockSpec(shape, lambda si, ph, p, pt: zeros)


def _dsa_decode(prs, cache_kidx, cache_akv, page_table, l, tabs, nstep, pg):
    s = page_table.shape[0]
    npast = nstep * pg * PAGE
    topk = min(DSA_TOPK, (npast + TD) // 4)
    wstep = pg * PAGE
    kid_specs = _page_specs((IDX_DIM, PAGE), l, pg, lambda ph, p, i: jnp.where(ph == 0, p, nstep - 1) * pg + i)
    kv_specs = _page_specs((2 * HEAD_DIM, PAGE), l, pg, lambda ph, p, i: jnp.where(ph == 0, 0, p) * pg + i)
    return pl.pallas_call(
        functools.partial(_dsa_decode_kernel, topk=topk, nstep=nstep, pg=pg),
        grid_spec=pltpu.PrefetchScalarGridSpec(
            num_scalar_prefetch=1, grid=(s, 2, nstep),
            in_specs=[_seq_spec(IDX_HEADS * IDX_DIM, 3), _seq_spec(128, 3), _seq_spec(MIX_W, 3), _seq_spec(IDX_DIM, 3),
                      _seq_spec(128, 3), _const_spec((N_HEADS * TD, wstep), 3), _const_spec((N_HEADS * TD, 1), 3),
                      _const_spec((N_HEADS * TD, PAGE), 3)] + kid_specs + kv_specs,
            out_specs=_seq_spec(MIX_W, 3),
            scratch_shapes=[pltpu.VMEM((TD, npast + PAGE), jnp.int32), pltpu.VMEM((TD, 128), jnp.int32),
                            pltpu.VMEM((N_HEADS * TD, 1), F32), pltpu.VMEM((N_HEADS * TD, 1), F32),
                            pltpu.VMEM((N_HEADS * TD, HEAD_DIM), F32)]),
        out_shape=jax.ShapeDtypeStruct((s, TD, MIX_W), BF),
        compiler_params=_cparams(("arbitrary", "arbitrary", "arbitrary")),
        name="dsa_decode",
    )(page_table, prs['qidx'], prs['amisc'], prs['qa'], prs['kidx_b'], prs['akv_b'],
      tabs['a_last'], tabs['a_far'], tabs['a_new'], *([cache_kidx] * pg), *([cache_akv] * pg))


def _nsa_decode_kernel(pt_ref, qb_ref, gates_ref, ab_ref, gk_ref, cbias_ref, gmat_ref, slcn_ref, winp_ref, winn_ref,
                       blast_ref, bfar_ref, bnew_ref, wbp_ref, *rest, n_sel, nstep, pg):
    slc_refs = rest[:pg]
    o_ref, selm_scr, bkey_scr, oc_scr, m_scr, l_scr, acc_scr = rest[pg:]
    p = pl.program_id(1)
    wstep = pg * PAGE
    npast = nstep * wstep
    nch = npast // CMP_STRIDE
    nsb = npast // SEL_BLOCK
    last = p == nstep - 1
    q4 = _heads_to_rows(qb_ref[...])

    @pl.when(p == 0)
    def _():
        ck, cv = _combine_compressed(ab_ref[...], gk_ref[...])
        qpos = npast + lax.broadcasted_iota(jnp.int32, (TD, 1), 0)
        cend = lax.broadcasted_iota(jnp.int32, (TD, nch), 1) * CMP_STRIDE + (CMP_LEN - 1)
        s = _dot_nt(q4, ck) + cbias_ref[...]
        pc = _masked_softmax_rows(s, _tile_rows(cend <= qpos, N_HEADS))
        oc_scr[...] = _dot(pc.astype(BF), cv)
        pcs = pc[0:TD] + pc[TD:2 * TD] + pc[2 * TD:3 * TD] + pc[3 * TD:4 * TD]
        imp = _dot3(pcs, gmat_ref[...])
        j_io = lax.broadcasted_iota(jnp.int32, (TD, nsb), 1)
        forced = (j_io == 0) | (j_io == nsb - 1)
        bkey_scr[...] = _sortable(jnp.where(forced, FORCE, imp))
        thr = _topk_select_ref(bkey_scr, n_sel - 1)
        selb = jnp.where(bkey_scr[...] >= thr, 1.0, 0.0).astype(BF)
        blk = lax.broadcasted_iota(jnp.int32, (nsb, wstep), 0)
        col = lax.broadcasted_iota(jnp.int32, (nsb, wstep), 1)
        for c in range(nstep):
            expand = jnp.where(blk == (c * wstep + col) >> SEL_SHIFT, 1.0, 0.0).astype(BF)
            selm_scr[:, c * wstep:(c + 1) * wstep] = _dot(selb, expand)
        _osm_scratch_init(m_scr, l_scr, acc_scr)

    k_t = jnp.concatenate([r[0:HEAD_DIM, :].astype(BF) for r in slc_refs], axis=1)
    v_t = jnp.concatenate([r[HEAD_DIM:2 * HEAD_DIM, :].astype(BF) for r in slc_refs], axis=1)
    bias = jnp.where(last, blast_ref[...], bfar_ref[...])
    s = _dot(q4, k_t) + bias
    mask = selm_scr[:, pl.ds(pl.multiple_of(p * wstep, wstep), wstep)] > 0.5
    _osm_scratch_update(s, _tile_rows(mask, N_HEADS), v_t, m_scr, l_scr, acc_scr, v_t=True)

    @pl.when(last)
    def _():
        lo, hi = slice(0, HEAD_DIM), slice(HEAD_DIM, 2 * HEAD_DIM)
        new_mask = _tile_rows(_causal_new_mask(False), N_HEADS)
        kvn = _pad_rows(slcn_ref[...], PAGE)
        s = _dot_nt(q4, kvn[:, lo]) + bnew_ref[...]
        _, l_s, acc_s = _osm_update(s, new_mask, m_scr[...], l_scr[...], acc_scr[...], kvn[:, hi])
        wp = winp_ref[...].astype(BF)
        nw = wp.shape[1]
        wd = nw + lax.broadcasted_iota(jnp.int32, (TD, nw), 0) - lax.broadcasted_iota(jnp.int32, (TD, nw), 1)
        s = _dot(q4, wp[lo, :]) + wbp_ref[...]
        carry = _osm_update(s, _tile_rows(wd <= WINDOW, N_HEADS), jnp.full((N_HEADS * TD, 1), NEG, F32),
                            jnp.zeros((N_HEADS * TD, 1), F32), jnp.zeros((N_HEADS * TD, HEAD_DIM), F32), wp[hi, :],
                            v_t=True)
        kvw = _pad_rows(winn_ref[...], PAGE)
        s = _dot_nt(q4, kvw[:, lo]) + bnew_ref[...]
        _, l_w, acc_w = _osm_update(s, new_mask, *carry, kvw[:, hi])
        g = gates_ref[...]
        o = _gate_cols(g, 0) * oc_scr[...] + _gate_cols(g, 1) * (acc_s / l_s) + _gate_cols(g, 2) * (acc_w / l_w)
        o_ref[...] = _rows_to_heads(o, TD).astype(BF)


def _nsa_decode(prs, ab, win_past, cache_slc, page_table, l, lw, tabs, nstep, pg):
    s = page_table.shape[0]
    wstep = pg * PAGE
    npast = nstep * wstep
    nch, nsb = npast // CMP_STRIDE, npast // SEL_BLOCK
    n_sel = min(N_SEL, nsb + 1)
    assert n_sel >= 2 and win_past.shape[2] == WINDOW
    return pl.pallas_call(
        functools.partial(_nsa_decode_kernel, n_sel=n_sel, nstep=nstep, pg=pg),
        grid_spec=pltpu.PrefetchScalarGridSpec(
            num_scalar_prefetch=1, grid=(s, nstep),
            in_specs=[_seq_spec(MIX_W, 2), _seq_spec(128, 2),
                      pl.BlockSpec((None, nch, 256), lambda si, p, pt: (si, 0, 0)), _const_spec((1, HEAD_DIM), 2),
                      _const_spec((N_HEADS * TD, nch), 2), _const_spec((nch, nsb), 2), _seq_spec(128, 2),
                      pl.BlockSpec((None, 2 * HEAD_DIM, WINDOW), lambda si, p, pt: (si, 0, 0)), _seq_spec(128, 2),
                      _const_spec((N_HEADS * TD, wstep), 2), _const_spec((N_HEADS * TD, 1), 2),
                      _const_spec((N_HEADS * TD, PAGE), 2), _const_spec((N_HEADS * TD, WINDOW), 2)]
            + _page_specs((2 * HEAD_DIM, PAGE), l, pg, lambda p, i: p * pg + i),
            out_specs=_seq_spec(MIX_W, 2),
            scratch_shapes=[pltpu.VMEM((TD, npast), F32), pltpu.VMEM((TD, nsb), jnp.int32),
                            pltpu.VMEM((N_HEADS * TD, HEAD_DIM), F32), pltpu.VMEM((N_HEADS * TD, 1), F32),
                            pltpu.VMEM((N_HEADS * TD, 1), F32), pltpu.VMEM((N_HEADS * TD, HEAD_DIM), F32)]),
        out_shape=jax.ShapeDtypeStruct((s, TD, MIX_W), BF),
        compiler_params=_cparams(("arbitrary", "arbitrary")),
        name="nsa_decode",
    )(page_table, prs['qb'], prs['gates'], ab, lw['gk_b'], tabs['cbias_d'], tabs['gmat_d'], prs['bslc_b'], win_past,
      prs['bwin_b'], tabs['b_last'], tabs['b_far'], tabs['b_new'], tabs['wb_past'], *([cache_slc] * pg))


def _fox_decode_kernel(pt_ref, q_ref, lfn_ref, lftn_ref, ckvn_ref, u_ref, tinc_ref, *rest, nstep, pg):
    kv_refs, lf_refs = rest[:pg], rest[pg:2 * pg]
    o_ref, qbd_scr, cq_scr, m_scr, l_scr, acc_scr, carry_scr = rest[2 * pg:]
    p = pl.program_id(1)

    @pl.when(p == 0)
    def _():
        qbd = _head_block_diag(q_ref[...])
        qbd_scr[...] = qbd
        lf = lfn_ref[...]
        rows = [lf[0:1, :]]
        for r in range(1, TD):
            rows.append(rows[-1] + lf[r:r + 1, :])
        npf = jnp.concatenate(rows, axis=0)
        cq4 = jnp.concatenate([npf[:, h:h + 1] for h in range(N_HEADS)], axis=0)
        cq_scr[...] = cq4
        npt = _dot3(lftn_ref[...], tinc_ref[...])
        ck4 = jnp.concatenate([jnp.broadcast_to(npt[h:h + 1, :], (TD, PAGE)) for h in range(N_HEADS)], axis=0)
        kvn = _pad_rows(ckvn_ref[...], PAGE)
        s = (_dot_nt(qbd, kvn[:, 0:MIX_W]) + cq4) - ck4
        _osm_scratch_init(m_scr, l_scr, acc_scr)
        _osm_scratch_update(s, _tile_rows(_causal_new_mask(False), N_HEADS), kvn[:, MIX_W:2 * MIX_W], m_scr, l_scr, acc_scr)
        carry_scr[...] = jnp.zeros_like(carry_scr)

    x = jnp.concatenate([r[...] for r in lf_refs], axis=0)
    rs = _dot3(x, u_ref[...])
    tot = jnp.sum(x, axis=1, keepdims=True)
    off = carry_scr[:, 0:1]
    pieces = [None] * pg
    for i in reversed(range(pg)):
        s_i = rs[i * SUBLANES:(i + 1) * SUBLANES, :] + off
        pieces[i] = jnp.concatenate([jnp.broadcast_to(s_i[h:h + 1, :], (TD, PAGE)) for h in range(N_HEADS)], axis=0)
        off = off + tot[i * SUBLANES:(i + 1) * SUBLANES, :]
    carry_scr[...] = jnp.broadcast_to(off, carry_scr.shape)
    k_t = jnp.concatenate([r[0:MIX_W, :].astype(BF) for r in kv_refs], axis=1)
    v_t = jnp.concatenate([r[MIX_W:2 * MIX_W, :].astype(BF) for r in kv_refs], axis=1)
    s = (_dot(qbd_scr[...], k_t) + cq_scr[...]) + jnp.concatenate(pieces, axis=1)
    _osm_scratch_update(s, None, v_t, m_scr, l_scr, acc_scr, v_t=True)

    @pl.when(p == nstep - 1)
    def _():
        o_ref[...] = _head_diag_pick(acc_scr[...] / l_scr[...], TD).astype(BF)


def _fox_decode(prs, lft_new, cache_ckv, cache_lft, page_table, l, tabs, nstep, pg):
    s = page_table.shape[0]
    rev = lambda p, i: (nstep - 1 - p) * pg + i
    return pl.pallas_call(
        functools.partial(_fox_decode_kernel, nstep=nstep, pg=pg),
        grid_spec=pltpu.PrefetchScalarGridSpec(
            num_scalar_prefetch=1, grid=(s, nstep),
            in_specs=[_seq_spec(MIX_W, 2), _seq_spec(128, 2), _seq_spec(128, 2), _seq_spec(2 * MIX_W, 2),
                      _const_spec((PAGE, PAGE), 2), _const_spec((PAGE, PAGE), 2)]
            + _page_specs((2 * MIX_W, PAGE), l, pg, rev) + _page_specs((SUBLANES, PAGE), l, pg, rev),
            out_specs=_seq_spec(MIX_W, 2),
            scratch_shapes=[pltpu.VMEM((N_HEADS * TD, MIX_W), BF), pltpu.VMEM((N_HEADS * TD, 1), F32),
                            pltpu.VMEM((N_HEADS * TD, 1), F32), pltpu.VMEM((N_HEADS * TD, 1), F32),
                            pltpu.VMEM((N_HEADS * TD, MIX_W), F32), pltpu.VMEM((SUBLANES, 128), F32)]),
        out_shape=jax.ShapeDtypeStruct((s, TD, MIX_W), BF),
        compiler_params=_cparams(("arbitrary", "arbitrary")),
        name="fox_decode",
    )(page_table, prs['qc'], prs['logf'], lft_new, prs['ckv_b'], tabs['umat'], tabs['tinc'],
      *([cache_ckv] * pg), *([cache_lft] * pg))


def _stick_decode_kernel(pt_ref, q_ref, dkvn_ref, u_ref, *rest, nstep, pg):
    kv_refs = rest[:pg]
    o_ref, qbd_scr, acc_scr, run_scr = rest[pg:]
    p = pl.program_id(1)
    umat = u_ref[...]
    rows = N_HEADS * TD

    @pl.when(p == 0)
    def _():
        qbd = _head_block_diag(q_ref[...])
        qbd_scr[...] = qbd
        kvn = _pad_rows(dkvn_ref[...], PAGE)
        strict = _tile_rows(_causal_new_mask(True), N_HEADS)
        nsp, lsig = _stick_terms(_dot_nt(qbd, kvn[:, 0:MIX_W]))
        nsp = jnp.where(strict, nsp, 0.0)
        a = jnp.where(strict, jnp.exp(lsig + _dot3(nsp, umat)), 0.0)
        acc_scr[...] = _dot(a.astype(BF), kvn[:, MIX_W:2 * MIX_W])
        run_scr[...] = jnp.sum(nsp, axis=1, keepdims=True)

    k_t = jnp.concatenate([r[0:MIX_W, :].astype(BF) for r in kv_refs], axis=1)
    v_t = jnp.concatenate([r[MIX_W:2 * MIX_W, :].astype(BF) for r in kv_refs], axis=1)
    nsp, lsig = _stick_terms(_dot(qbd_scr[...], k_t))
    stack = jnp.concatenate([nsp[:, i * PAGE:(i + 1) * PAGE] for i in range(pg)], axis=0)
    rs = _dot3(stack, umat)
    tot = jnp.sum(stack, axis=1, keepdims=True)
    off = run_scr[...]
    pieces = [None] * pg
    for i in reversed(range(pg)):
        pieces[i] = rs[i * rows:(i + 1) * rows, :] + off
        off = off + tot[i * rows:(i + 1) * rows, :]
    run_scr[...] = off
    a = jnp.exp(lsig + jnp.concatenate(pieces, axis=1))
    acc_scr[...] = acc_scr[...] + _dot_nt(a.astype(BF), v_t)

    @pl.when(p == nstep - 1)
    def _():
        o_ref[...] = _head_diag_pick(acc_scr[...], TD).astype(BF)


def _stick_decode(prs, cache_dkv, page_table, l, tabs, nstep, pg):
    s = page_table.shape[0]
    return pl.pallas_call(
        functools.partial(_stick_decode_kernel, nstep=nstep, pg=pg),
        grid_spec=pltpu.PrefetchScalarGridSpec(
            num_scalar_prefetch=1, grid=(s, nstep),
            in_specs=[_seq_spec(MIX_W, 2), _seq_spec(2 * MIX_W, 2), _const_spec((PAGE, PAGE), 2)]
            + _page_specs((2 * MIX_W, PAGE), l, pg, lambda p, i: (nstep - 1 - p) * pg + i),
            out_specs=_seq_spec(MIX_W, 2),
            scratch_shapes=[pltpu.VMEM((N_HEADS * TD, MIX_W), BF), pltpu.VMEM((N_HEADS * TD, MIX_W), F32),
                            pltpu.VMEM((N_HEADS * TD, 1), F32)]),
        out_shape=jax.ShapeDtypeStruct((s, TD, MIX_W), BF),
        compiler_params=_cparams(("arbitrary", "arbitrary")),
        name="stick_decode",
    )(page_table, prs['qd'], prs['dkv_b'], tabs['umat'], *([cache_dkv] * pg))


_IN_A, _IN_B, _IN_C, _IN_D = 968, 1620, 2392, 3160


def _bucket(dist):
    n = jnp.maximum(dist, 0)
    exact = N_BUCKETS // 2
    nf = jnp.maximum(n, 1).astype(F32)
    large = exact + (jnp.log(nf / exact) / math.log(MAX_DIST / exact) * (N_BUCKETS - exact)).astype(jnp.int32)
    return jnp.where(n < exact, n, jnp.minimum(large, N_BUCKETS - 1))


def _bias_table(tab, dist):
    b = _bucket(dist)
    out = jnp.zeros((tab.shape[1],) + dist.shape, F32)
    for j in range(N_BUCKETS):
        out = jnp.where(b[None] == j, tab[j].reshape((-1,) + (1,) * dist.ndim), out)
    return out


def _prep_layer(l, p, tm_max):
    w_in = p['w_in'][l]
    d = w_in.shape[0]
    z = lambda n: jnp.zeros((d, n), w_in.dtype)
    w_proj = jnp.concatenate([w_in[:, 0:_IN_A], z(1024 - _IN_A), w_in[:, _IN_A:_IN_B], z(768 - (_IN_B - _IN_A)),
                              w_in[:, _IN_B:_IN_C], z(896 - (_IN_C - _IN_B)), w_in[:, _IN_C:_IN_D]], axis=1)
    assert w_proj.shape[1] == N_PROJ
    qk = p['qk_gain'][l]
    gain = jnp.ones((N_PROJ,), F32)
    nmask = jnp.zeros((N_PROJ,), F32)
    for c0, g, rep in ((C_AQ, qk[0, 0], 4), (C_AKV, qk[0, 1], 1), (C_BQ, qk[1, 0], 4), (C_BSLC, qk[1, 1], 1),
                       (C_BWIN, qk[1, 1], 1), (C_CQ, qk[2, 0], 4), (C_CK, qk[2, 1], 4)):
        gain = gain.at[c0:c0 + rep * HEAD_DIM].set(jnp.tile(g, rep))
        nmask = nmask.at[c0:c0 + rep * HEAD_DIM].set(1.0)
    gidx = np.arange(MIX_W) // HEAD_DIM
    bd = jnp.asarray((gidx[:, None] == gidx[None, :]).astype(np.float32) / HEAD_DIM, BF)
    tri = jnp.asarray(np.tril(np.ones((tm_max, tm_max), np.float32)), BF)
    return {
        'g_attn': p['norm_attn'][l][None, :], 'w_proj': w_proj.astype(BF), 'gain': gain[None, :],
        'nmask': nmask[None, :], 'b_f': jnp.zeros((1, 128), F32).at[0, :N_HEADS].set(p['b_forget'][l]),
        'bd': bd, 'tri': tri,
        'w_gate': w_in[:, _IN_D:].astype(BF), 'w_br': p['w_branch'][l].astype(BF), 'w_o': p['w_out'][l].astype(BF),
        'g_ffn': p['norm_ffn'][l][None, :], 'w_up': p['w_up'][l].astype(BF), 'conv_w': p['conv_w'][l],
        'conv_b': p['conv_b'][l][None, :], 'w_down': p['w_down'][l].astype(BF),
        'gk_b': qk[1, 1][None, :], 'cmp_w': p['cmp_w'][l], 'cmp_pe': p['cmp_pe'][l],
    }


def _toeplitz_bias(tab):
    r = jnp.arange(TQ)[:, None]
    c = jnp.arange(TKB)[None, :]
    return jnp.stack([_bias_table(tab, dd * TKB + r - c) for dd in range(3)])


def _prompt_tables(rel_bias, t):
    tab_a, tab_b = rel_bias[:, :N_HEADS], rel_bias[:, N_HEADS:]
    nq, nch, ns = t // TQ, t // CMP_STRIDE, t // SEL_BLOCK
    qpos = jnp.arange(t).reshape(nq, TQ)
    cend = jnp.arange(nch) * CMP_STRIDE + (CMP_LEN - 1)
    cbias = jnp.transpose(_bias_table(tab_b, qpos[:, :, None] - cend[None, None, :]), (1, 0, 2, 3))
    n = np.arange(nch)
    gmat = ((n[:, None] // (SEL_BLOCK // CMP_STRIDE) == np.arange(ns)[None, :]) & (n[:, None] < nch - 1))
    emat = np.arange(t)[None, :] // SEL_BLOCK == np.arange(ns)[:, None]
    c = np.arange(TKB)
    return {
        'dsa_tiles': _toeplitz_bias(tab_a), 'nsa_tiles': _toeplitz_bias(tab_b), 'cbias_p': cbias,
        'gmat_p': jnp.asarray(gmat.astype(np.float32), BF), 'emat_p': jnp.asarray(emat.astype(np.float32), BF),
        'tie_p': jnp.asarray((np.arange(ns)[None, :] > np.arange(ns)[:, None]).astype(np.float32)),
        'uaug': jnp.asarray(np.kron(np.eye(2), (c[:, None] > c[None, :])).astype(np.float32), BF),
    }


def _layer_prompt(x, lw, tabs):
    b, t, d = x.shape
    assert t % TQ == 0
    tm = next(c for c in (DENSE_TM, 256, TQ) if t % c == 0)
    x2 = x.reshape(b * t, d)
    pr = _proj(x2, lw, tm=tm, tiles_per_seq=t // tm, with_cumsum=True)
    o_a = _dsa_prompt(pr, tabs['dsa_tiles'], b, t)
    ab = _compress_dense(pr['bcmp'].reshape(b * t // CMP_STRIDE, CMP_STRIDE * 2 * HEAD_DIM), lw, t // CMP_STRIDE)
    o_b = _nsa_prompt(pr, ab, lw, tabs, b, t)
    ckt = jnp.transpose(pr['call'].reshape(b, t, 128)[:, :, :SUBLANES], (0, 2, 1)).reshape(b * SUBLANES, t)
    o_c = _fox_prompt(pr, ckt, b, t)
    o_d = _stick_prompt(pr, tabs['uaug'], b, t)
    xm = _merge(x2, o_a, o_b, o_c, o_d, lw, tm=tm)
    dummy = jnp.zeros((SUBLANES, lw['w_up'].shape[1]), F32)
    y2, conv = _ffn(xm, lw, dummy, dummy, tm=tm, carry_mode=True, tiles_per_seq=t // tm)
    keep = min(WINDOW, t)
    new = {
        'a_kv': pr['akv'].reshape(b, t, 2, HEAD_DIM),
        'a_kidx': pr['amisc'][:, :IDX_DIM].reshape(b, t, IDX_DIM),
        'b_cmp_kv': pr['bcmp'].reshape(b, t, 2, HEAD_DIM),
        'b_slc_kv': pr['bslc'].reshape(b, t, 2, HEAD_DIM),
        'b_win_kv': pr['bwin'].reshape(b, t, 2, HEAD_DIM)[:, t - keep:],
        'c_kv': pr['ckv'].reshape(b, t, 2, N_HEADS, HEAD_DIM),
        'c_logf': pr['logf'][:, :N_HEADS].reshape(b, t, N_HEADS),
        'd_kv': pr['dkv'].reshape(b, t, 2, N_HEADS, HEAD_DIM),
        'ffn_conv': conv.reshape(b, SUBLANES, -1)[:, SUBLANES - (CONV_W - 1):],
    }
    return y2.reshape(b, t, d), new, (o_a, o_b, o_c, o_d)


def _decode_tables(rel_bias, npast, pg):
    tab_a, tab_b = rel_bias[:, :N_HEADS], rel_bias[:, N_HEADS:]
    wstep = pg * PAGE
    tq = jnp.arange(TD)
    rows = lambda tab, dist: _bias_table(tab, dist).reshape(N_HEADS * TD, -1)
    far = lambda tab: jnp.repeat(tab[N_BUCKETS - 1], TD)[:, None]
    d_last = (npast + tq)[:, None] - (npast - wstep + jnp.arange(wstep))[None, :]
    d_new = tq[:, None] - jnp.arange(PAGE)[None, :]
    nch, nsb = npast // CMP_STRIDE, npast // SEL_BLOCK
    d_cmp = (npast + tq)[:, None] - (jnp.arange(nch) * CMP_STRIDE + (CMP_LEN - 1))[None, :]
    d_win = WINDOW + tq[:, None] - jnp.arange(WINDOW)[None, :]
    n = np.arange(nch)
    gmat = (n[:, None] // (SEL_BLOCK // CMP_STRIDE) == np.arange(nsb)[None, :]) & (n[:, None] < nch - 1)
    c = np.arange(PAGE)
    return {
        'a_last': rows(tab_a, d_last), 'a_far': far(tab_a), 'a_new': rows(tab_a, d_new),
        'b_last': rows(tab_b, d_last), 'b_far': far(tab_b), 'b_new': rows(tab_b, d_new),
        'cbias_d': rows(tab_b, d_cmp), 'wb_past': rows(tab_b, d_win),
        'gmat_d': jnp.asarray(gmat.astype(np.float32), BF),
        'umat': jnp.asarray((c[:, None] > c[None, :]).astype(np.float32), BF),
        'tinc': jnp.asarray((c[:, None] <= c[None, :]).astype(np.float32), BF),
    }


def _layer_decode(x, lw, tabs, l, caches, ffn_state, page_table, nstep, pg):
    s, td, d = x.shape
    assert td == TD
    m = s * td
    x2 = x.reshape(m, d)
    pr = _proj(x2, lw, tm=m, tiles_per_seq=1, with_cumsum=False)
    prs = {k: v.reshape(s, td, v.shape[-1]) for k, v in pr.items()}
    o_a = _dsa_decode(prs, caches['a_kidx'], caches['a_kv'], page_table, l, tabs, nstep, pg)
    ab = _compress_paged(caches['b_cmp'], page_table, l, lw, nstep, pg)
    o_b = _nsa_decode(prs, ab, caches['b_win_t'][l], caches['b_slc'], page_table, l, lw, tabs, nstep, pg)
    lft_new = jnp.pad(jnp.transpose(prs['logf'][:, :, :SUBLANES], (0, 2, 1)), ((0, 0), (0, 0), (0, PAGE - td)))
    o_c = _fox_decode(prs, lft_new, caches['c_kv'], caches['c_lft'], page_table, l, tabs, nstep, pg)
    o_d = _stick_decode(prs, caches['d_kv'], page_table, l, tabs, nstep, pg)
    flat = lambda o: o.reshape(m, MIX_W)
    xm = _merge(x2, flat(o_a), flat(o_b), flat(o_c), flat(o_d), lw, tm=m)
    st0 = jnp.repeat(ffn_state[:, 0], td, axis=0)
    st1 = jnp.repeat(ffn_state[:, 1], td, axis=0)
    y2, conv = _ffn(xm, lw, st0, st1, tm=m, carry_mode=False, tiles_per_seq=1)
    win_new = pr['bwin'].reshape(s, td, 2, HEAD_DIM)
    win_all = jnp.concatenate([caches['b_win'][l], win_new], axis=1)
    keep = min(WINDOW, win_all.shape[1])
    new = {
        'a_kv': pr['akv'].reshape(s, td, 2, HEAD_DIM),
        'a_kidx': pr['amisc'][:, :IDX_DIM].reshape(s, td, IDX_DIM),
        'b_cmp_kv': pr['bcmp'].reshape(s, td, 2, HEAD_DIM),
        'b_slc_kv': pr['bslc'].reshape(s, td, 2, HEAD_DIM),
        'b_win_kv': win_all[:, win_all.shape[1] - keep:],
        'c_kv': pr['ckv'].reshape(s, td, 2, N_HEADS, HEAD_DIM),
        'c_logf': pr['logf'][:, :N_HEADS].reshape(s, td, N_HEADS),
        'd_kv': pr['dkv'].reshape(s, td, 2, N_HEADS, HEAD_DIM),
        'ffn_conv': conv.reshape(s, td, -1)[:, td - (CONV_W - 1):],
    }
    return y2.reshape(s, td, d), new, (o_a, o_b, o_c, o_d)


_STATE_KEYS = ('a_kv', 'a_kidx', 'b_cmp_kv', 'b_slc_kv', 'b_win_kv', 'c_kv', 'c_logf', 'd_kv', 'ffn_conv')


def kernel(x_prompt, x_sample, cache_a_kv, cache_a_kidx, cache_b_cmp_kv, cache_b_slc_kv, state_b_win_kv,
           cache_c_kv, cache_c_logf, cache_d_kv, state_ffn_conv, page_table, rel_bias, norm_attn, w_in,
           b_forget, qk_gain, cmp_w, cmp_pe, w_branch, w_out, norm_ffn, w_up, conv_w, conv_b, w_down):
    params = dict(norm_attn=norm_attn, w_in=w_in, b_forget=b_forget, qk_gain=qk_gain, cmp_w=cmp_w, cmp_pe=cmp_pe,
                  w_branch=w_branch, w_out=w_out, norm_ffn=norm_ffn, w_up=w_up, conv_w=conv_w, conv_b=conv_b,
                  w_down=w_down)
    depth, n_pool = w_in.shape[0], cache_a_kv.shape[1]
    n_seq, n_pages = page_table.shape
    assert cache_a_kv.shape[2] == PAGE
    pg = PAGES_PER_STEP if n_pages % PAGES_PER_STEP == 0 else 1
    nstep = n_pages // pg
    npast = n_pages * PAGE
    def token_last(a):
        nd = a.ndim
        return jnp.transpose(a, (0, 1) + tuple(range(3, nd)) + (2,)).reshape(a.shape[0], a.shape[1], -1, a.shape[2])

    caches = {
        'a_kv': token_last(cache_a_kv), 'a_kidx': token_last(cache_a_kidx),
        'b_cmp': cache_b_cmp_kv.reshape(depth, n_pool, PAGE // CMP_STRIDE, CMP_STRIDE * 2 * HEAD_DIM),
        'b_slc': token_last(cache_b_slc_kv), 'b_win': state_b_win_kv, 'b_win_t': token_last(state_b_win_kv),
        'c_kv': token_last(cache_c_kv), 'd_kv': token_last(cache_d_kv),
        'c_lft': jnp.pad(token_last(cache_c_logf.astype(F32)), ((0, 0), (0, 0), (0, SUBLANES - N_HEADS), (0, 0))),
    }
    tabs_p = _prompt_tables(rel_bias, x_prompt.shape[1])
    tabs_d = _decode_tables(rel_bias, npast, pg)
    y_p, y_s = x_prompt, x_sample
    new_p = {k: [] for k in _STATE_KEYS}
    new_s = {k: [] for k in _STATE_KEYS}
    for l in range(depth):
        lw = _prep_layer(l, params, DENSE_TM)
        y_p, st, _ = _layer_prompt(y_p, lw, tabs_p)
        for k in _STATE_KEYS:
            new_p[k].append(st[k])
        y_s, st, _ = _layer_decode(y_s, lw, tabs_d, l, caches, state_ffn_conv[l], page_table, nstep, pg)
        for k in _STATE_KEYS:
            new_s[k].append(st[k])
    sp = {k: jnp.stack(v) for k, v in new_p.items()}
    ss = {k: jnp.stack(v) for k, v in new_s.items()}
    return (y_p, y_s,
            sp['a_kv'], ss['a_kv'], sp['a_kidx'], ss['a_kidx'],
            sp['b_cmp_kv'], ss['b_cmp_kv'], sp['b_slc_kv'], ss['b_slc_kv'],
            sp['b_win_kv'], ss['b_win_kv'], sp['c_kv'], ss['c_kv'],
            sp['c_logf'], ss['c_logf'], sp['d_kv'], ss['d_kv'],
            sp['ffn_conv'], ss['ffn_conv'])
```
